```python
import jax, jax.numpy as jnp
from jax import lax
import numpy as np

D_MODEL = 1024
BATCH = 2
SEQ = 8192
DEPTH = 1

N_MEM = 256
BRANCH_WIDTH = D_MODEL
N_BRANCH = 3
EPS = 1e-6
CONV_WIDTH = 4
LRU_WIDTH = BRANCH_WIDTH
LRU_BLOCKS = 4
LRU_BLOCK = LRU_WIDTH // LRU_BLOCKS
LRU_C = 8.0
MLSTM_WIDTH = BRANCH_WIDTH
MLSTM_HEADS = 4
MLSTM_HEAD_DIM = MLSTM_WIDTH // MLSTM_HEADS
MLSTM_CHUNK = 128
XATTN_WIDTH = BRANCH_WIDTH
XATTN_HEADS = 4
XATTN_HEAD_DIM = XATTN_WIDTH // XATTN_HEADS
IN_SPLITS = (LRU_WIDTH, LRU_WIDTH, MLSTM_WIDTH, MLSTM_WIDTH, XATTN_WIDTH, XATTN_WIDTH, N_BRANCH * D_MODEL)
IN_TOTAL = sum(IN_SPLITS)

kernel_name = "hybrid_rglru_mlstm_memxattn_gated"


def rms_norm(x, g):
    xf = x.astype(jnp.float32)
    y = xf * lax.rsqrt(jnp.mean(xf * xf, axis=-1, keepdims=True) + EPS) * g.astype(jnp.float32)
    return y.astype(x.dtype)


def causal_conv(u, w, b):
    K = w.shape[0]
    S = u.shape[1]
    up = jnp.pad(u, ((0, 0), (K - 1, 0), (0, 0)))
    out = b
    for k in range(K):
        out = out + w[k] * up[:, k:k + S]
    return out


def block_diag(u, w):
    G, I, O = w.shape
    ug = u.reshape(u.shape[:-1] + (G, I))
    return jnp.einsum('bsgi,gio->bsgo', ug, w).reshape(u.shape[:-1] + (G * O,))


def _lin_rec_combine(e1, e2):
    a1, b1 = e1
    a2, b2 = e2
    return a1 * a2, a2 * b1 + b2


def rg_lru(u, w_a, b_a, w_x, b_x, lam):
    uf = u.astype(jnp.float32)
    r = jax.nn.sigmoid(block_diag(u, w_a).astype(jnp.float32) + b_a.astype(jnp.float32))
    i = jax.nn.sigmoid(block_diag(u, w_x).astype(jnp.float32) + b_x.astype(jnp.float32))
    log_a = LRU_C * r * jax.nn.log_sigmoid(lam.astype(jnp.float32))
    a = jnp.exp(log_a)
    mult = jnp.sqrt(-jnp.expm1(2.0 * log_a))
    _, h = lax.associative_scan(_lin_rec_combine, (a, mult * i * uf), axis=1)
    return h.astype(u.dtype)


def mlstm_chunkwise(q, k, v, ig, lf):
    Bsz, S, H, d = q.shape
    L = MLSTM_CHUNK
    nc = S // L

    def chunks(t):
        return jnp.moveaxis(t.reshape((Bsz, nc, L, H) + t.shape[3:]), (1, 3), (0, 2))

    causal = jnp.tril(jnp.ones((L, L), dtype=bool))

    def step(carry, xs):
        C, n, m = carry
        qc, kc, vc, igc, lfc = xs
        b = jnp.cumsum(lfc, axis=-1)
        dmat = b[..., :, None] - b[..., None, :] + igc[..., None, :]
        dmat = jnp.where(causal, dmat, -jnp.inf)
        inter = b + m[..., None]
        m_t = jnp.maximum(inter, jnp.max(dmat, axis=-1))
        s = jnp.einsum('bhtd,bhsd->bhts', qc, kc) * jnp.exp(dmat - m_t[..., None])
        sc = jnp.exp(inter - m_t)
        num = jnp.einsum('bhts,bhsd->bhtd', s, vc) + sc[..., None] * jnp.einsum('bhvk,bhtk->bhtv', C, qc)
        den = jnp.sum(s, axis=-1) + sc * jnp.einsum('bhk,bhtk->bht', n, qc)
        h = num / jnp.maximum(jnp.abs(den), jnp.exp(-m_t))[..., None]
        bL = b[..., -1]
        g = bL[..., None] - b + igc
        m_new = jnp.maximum(bL + m, jnp.max(g, axis=-1))
        wk = jnp.exp(g - m_new[..., None])
        decay = jnp.exp(bL + m - m_new)
        C_new = decay[..., None, None] * C + jnp.einsum('bhs,bhsv,bhsk->bhvk', wk, vc, kc)
        n_new = decay[..., None] * n + jnp.einsum('bhs,bhsk->bhk', wk, kc)
        return (C_new, n_new, m_new), h

    init = (jnp.zeros((Bsz, H, d, d), jnp.float32), jnp.zeros((Bsz, H, d), jnp.float32),
            jnp.zeros((Bsz, H), jnp.float32))
    _, hs = lax.scan(step, init, (chunks(q), chunks(k), chunks(v), chunks(ig), chunks(lf)))
    return jnp.moveaxis(hs, (0, 2), (1, 3)).reshape(Bsz, S, H, d)


def mlstm_branch(xm, z, conv_w, conv_b, w_q, w_k, w_v, w_o, b_o, w_if, b_if, norm_g, skip):
    Bsz, S, W = xm.shape
    H, d = MLSTM_HEADS, MLSTM_HEAD_DIM
    xc = jax.nn.silu(causal_conv(xm, conv_w, conv_b))
    q = block_diag(xc, w_q)
    k = block_diag(xc, w_k)
    v = block_diag(xm, w_v)
    o = jax.nn.sigmoid(block_diag(xm, w_o).astype(jnp.float32) + b_o.astype(jnp.float32))
    gates = (jnp.concatenate([q, k, v], axis=-1) @ w_if).astype(jnp.float32) + b_if.astype(jnp.float32)
    ig = gates[..., :H]
    lf = jax.nn.log_sigmoid(gates[..., H:])
    f32 = jnp.float32
    h = mlstm_chunkwise(q.astype(f32).reshape(Bsz, S, H, d),
                        (k.astype(f32) * (d ** -0.5)).reshape(Bsz, S, H, d),
                        v.astype(f32).reshape(Bsz, S, H, d), ig, lf)
    h = o.reshape(Bsz, S, H, d) * h
    mu = jnp.mean(h, axis=-1, keepdims=True)
    var = jnp.mean(jnp.square(h - mu), axis=-1, keepdims=True)
    hn = ((h - mu) * lax.rsqrt(var + EPS)).reshape(Bsz, S, W) * norm_g.astype(f32)
    out = (hn + skip.astype(f32) * xc.astype(f32)) * jax.nn.silu(z.astype(f32))
    return out.astype(xm.dtype)


def memory_cross_attention(q_in, z, mem_n, w_mem_kv):
    Bsz, S, _ = q_in.shape
    H, d = XATTN_HEADS, XATTN_HEAD_DIM
    kv = mem_n @ w_mem_kv
    k = kv[..., :XATTN_WIDTH].reshape(Bsz, -1, H, d)
    v = kv[..., XATTN_WIDTH:].reshape(Bsz, -1, H, d)
    q = q_in.reshape(Bsz, S, H, d)
    s = jnp.einsum('bshd,bmhd->bhsm', q, k).astype(jnp.float32) * (d ** -0.5)
    p = jax.nn.softmax(s, axis=-1).astype(v.dtype)
    o = jnp.einsum('bhsm,bmhd->bshd', p, v).reshape(Bsz, S, XATTN_WIDTH)
    return o * jax.nn.silu(z)


def setup_inputs(seed: int = 0) -> dict:
    key = jax.random.key(seed)
    ks = iter(jax.random.split(key, 40))
    f32 = jnp.float32
    L, D, H, hd = DEPTH, D_MODEL, MLSTM_HEADS, MLSTM_HEAD_DIM

    def nrm(shape, scale):
        return jax.random.normal(next(ks), shape, f32) * scale

    u = jax.random.uniform(next(ks), (L, LRU_WIDTH), f32, 0.9, 0.999)
    a = u ** (1.0 / LRU_C)
    lru_lambda = jnp.log(a) - jnp.log1p(-a)
    b_if = jnp.concatenate([nrm((L, H), 0.1),
                            jnp.broadcast_to(jnp.linspace(3.0, 6.0, H), (L, H)) + nrm((L, H), 0.1)], axis=-1)
    return {
        "x": nrm((BATCH, SEQ, D), 1.0),
        "mem": nrm((BATCH, N_MEM, D), 1.0),
        "norm_g": 1.0 + nrm((L, D), 0.02),
        "w_in": nrm((L, D, IN_TOTAL), D ** -0.5),
        "b_in": nrm((L, IN_TOTAL), 0.02),
        "lru_conv_w": nrm((L, CONV_WIDTH, LRU_WIDTH), CONV_WIDTH ** -0.5),
        "lru_conv_b": nrm((L, LRU_WIDTH), 0.02),
        "lru_w_a": nrm((L, LRU_BLOCKS, LRU_BLOCK, LRU_BLOCK), LRU_BLOCK ** -0.5),
        "lru_b_a": nrm((L, LRU_WIDTH), 0.02),
        "lru_w_x": nrm((L, LRU_BLOCKS, LRU_BLOCK, LRU_BLOCK), LRU_BLOCK ** -0.5),
        "lru_b_x": nrm((L, LRU_WIDTH), 0.02),
        "lru_lambda": lru_lambda,
        "m_conv_w": nrm((L, CONV_WIDTH, MLSTM_WIDTH), CONV_WIDTH ** -0.5),
        "m_conv_b": nrm((L, MLSTM_WIDTH), 0.02),
        "m_w_q": nrm((L, H, hd, hd), hd ** -0.5),
        "m_w_k": nrm((L, H, hd, hd), hd ** -0.5),
        "m_w_v": nrm((L, H, hd, hd), hd ** -0.5),
        "m_w_o": nrm((L, H, hd, hd), hd ** -0.5),
        "m_b_o": nrm((L, MLSTM_WIDTH), 0.02),
        "m_w_if": nrm((L, 3 * MLSTM_WIDTH, 2 * H), (3 * MLSTM_WIDTH) ** -0.5),
        "m_b_if": b_if,
        "m_norm_g": 1.0 + nrm((L, MLSTM_WIDTH), 0.02),
        "m_skip": 1.0 + nrm((L, MLSTM_WIDTH), 0.02),
        "mem_norm_g": 1.0 + nrm((L, D), 0.02),
        "w_mem_kv": nrm((L, D, 2 * XATTN_WIDTH), D ** -0.5),
        "w_branch": nrm((L, N_BRANCH, BRANCH_WIDTH, D), BRANCH_WIDTH ** -0.5),
        "w_out": nrm((L, D, D), D ** -0.5),
        "final_norm_g": 1.0 + nrm((D,), 0.02),
    }


def reference(x, mem, norm_g, w_in, b_in, lru_conv_w, lru_conv_b, lru_w_a, lru_b_a, lru_w_x, lru_b_x,
              lru_lambda, m_conv_w, m_conv_b, m_w_q, m_w_k, m_w_v, m_w_o, m_b_o, m_w_if, m_b_if,
              m_norm_g, m_skip, mem_norm_g, w_mem_kv, w_branch, w_out, final_norm_g):
    Bsz, S, D = x.shape
    cuts = [int(c) for c in np.cumsum(IN_SPLITS)[:-1]]
    for l in range(DEPTH):
        h = rms_norm(x, norm_g[l])
        proj = h @ w_in[l] + b_in[l]
        lru_x, lru_z, m_x, m_z, xa_q, xa_z, gate_logits = jnp.split(proj, cuts, axis=-1)
        y_a = rg_lru(causal_conv(lru_x, lru_conv_w[l], lru_conv_b[l]),
                     lru_w_a[l], lru_b_a[l], lru_w_x[l], lru_b_x[l], lru_lambda[l]) * jax.nn.silu(lru_z)
        y_b = mlstm_branch(m_x, m_z, m_conv_w[l], m_conv_b[l], m_w_q[l], m_w_k[l], m_w_v[l], m_w_o[l],
                           m_b_o[l], m_w_if[l], m_b_if[l], m_norm_g[l], m_skip[l])
        y_c = memory_cross_attention(xa_q, xa_z, rms_norm(mem, mem_norm_g[l]), w_mem_kv[l])
        ys = jnp.stack([y_a, y_b, y_c], axis=2)
        branch_out = jnp.einsum('bsnw,nwd->bsnd', ys, w_branch[l])
        gates = jax.nn.sigmoid(gate_logits.astype(jnp.float32)).reshape(Bsz, S, N_BRANCH, D).astype(x.dtype)
        merged = jnp.sum(gates * branch_out, axis=2)
        x = x + merged @ w_out[l]
    return rms_norm(x, final_norm_g)
```

```python
import functools

import jax
import jax.numpy as jnp
from jax import lax
from jax.experimental import pallas as pl
from jax.experimental.pallas import tpu as pltpu

D_MODEL = 1024
N_HEADS = 4
HEAD_DIM = D_MODEL // N_HEADS
N_MEM = 256
EPS = 1e-6
CONV_WIDTH = 4
LRU_C = 8.0
MLSTM_CHUNK = 128
GATE_LANES = 128
SUBLANES = 8
TOKEN_TILE = 256
VMEM_LIMIT_BYTES = 56 * 1024 * 1024

BF16 = jnp.bfloat16
F32 = jnp.float32


def _dot(a, b):
    return jnp.dot(a, b, preferred_element_type=F32)


def _dot_nt(a, b):
    return lax.dot_general(a, b, (((1,), (1,)), ((), ())), preferred_element_type=F32)


def _dot_tn(a, b):
    return lax.dot_general(a, b, (((0,), (0,)), ((), ())), preferred_element_type=F32)


def _sigmoid(x):
    return 1.0 / (1.0 + jnp.exp(-x))


def _silu(x):
    return x * _sigmoid(x)


def _log_sigmoid(x):
    return jnp.minimum(x, 0.0) - jnp.log1p(jnp.exp(-jnp.abs(x)))


def _rms_norm(x, g):
    ms = jnp.mean(x * x, axis=-1, keepdims=True)
    return x * lax.rsqrt(ms + EPS) * g


def _head(h):
    return slice(h * HEAD_DIM, (h + 1) * HEAD_DIM)


def _causal_conv(xpad_ref, px, w_ref, b_ref, tm):
    xpad_ref[pl.ds(SUBLANES, tm), :] = px
    acc = b_ref[...] + w_ref[CONV_WIDTH - 1:CONV_WIDTH, :] * px
    for k in range(CONV_WIDTH - 1):
        shifted = xpad_ref[pl.ds(SUBLANES - (CONV_WIDTH - 1) + k, tm), :]
        acc = acc + w_ref[k:k + 1, :] * shifted
    xpad_ref[pl.ds(0, SUBLANES), :] = xpad_ref[pl.ds(tm, SUBLANES), :]
    return acc


def _gated_out(hb, y, wg_ref, bg_ref, wbr_ref):
    bo = _dot(y.astype(BF16), wbr_ref[...])
    gl = _dot(hb, wg_ref[...]) + bg_ref[...]
    return _sigmoid(gl) * bo


def _kv_kernel(mem_ref, g_ref, w_ref, o_ref):
    mn = _rms_norm(mem_ref[...], g_ref[...]).astype(BF16)
    o_ref[...] = _dot(mn, w_ref[...]).astype(BF16)


def _lru_kernel(x_ref, ng_ref, wxz_ref, bxz_ref, cw_ref, cb_ref, wa_ref, ba_ref,
                wx_ref, bx_ref, lam_ref, wg_ref, bg_ref, wbr_ref, o_ref,
                xpad, a_s, b_s, hcar, *, tm):
    @pl.when(pl.program_id(1) == 0)
    def _():
        xpad[pl.ds(0, SUBLANES), :] = jnp.zeros((SUBLANES, D_MODEL), F32)
        hcar[...] = jnp.zeros((SUBLANES, D_MODEL), F32)

    hb = _rms_norm(x_ref[...], ng_ref[...]).astype(BF16)
    px = _dot(hb, wxz_ref[:, :D_MODEL]) + bxz_ref[:, :D_MODEL]
    u = _causal_conv(xpad, px, cw_ref, cb_ref, tm)
    ub = u.astype(BF16)
    log_sig_lam = _log_sigmoid(lam_ref[...])
    for g in range(N_HEADS):
        sl = _head(g)
        r = _sigmoid(_dot(ub[:, sl], wa_ref[g]) + ba_ref[:, sl])
        ig = _sigmoid(_dot(ub[:, sl], wx_ref[g]) + bx_ref[:, sl])
        log_a = LRU_C * r * log_sig_lam[:, sl]
        a = jnp.exp(log_a)
        a_s[:, sl] = a
        b_s[:, sl] = jnp.sqrt(-jnp.tanh(log_a) * (1.0 + a * a)) * ig * u[:, sl]

    row = lax.broadcasted_iota(jnp.int32, (SUBLANES, D_MODEL), 0)

    def group(j, hprev):
        off = pl.multiple_of(j * SUBLANES, SUBLANES)
        a = a_s[pl.ds(off, SUBLANES), :]
        b = b_s[pl.ds(off, SUBLANES), :]
        for k in (1, 2, 4):
            valid = row >= k
            a_sh = pltpu.roll(a, k, 0)
            b_sh = pltpu.roll(b, k, 0)
            b = b + a * jnp.where(valid, b_sh, 0.0)
            a = a * jnp.where(valid, a_sh, 1.0)
        h = b + a * hprev
        b_s[pl.ds(off, SUBLANES), :] = h
        return jnp.broadcast_to(h[SUBLANES - 1:SUBLANES, :], (SUBLANES, D_MODEL))

    hcar[...] = lax.fori_loop(0, tm // SUBLANES, group, hcar[...])

    pz = _dot(hb, wxz_ref[:, D_MODEL:]) + bxz_ref[:, D_MODEL:]
    y = b_s[...] * _silu(pz)
    o_ref[...] = _gated_out(hb, y, wg_ref, bg_ref, wbr_ref)


def _cumsum_rows(x):
    n = x.shape[0]
    row = lax.broadcasted_iota(jnp.int32, x.shape, 0)
    k = 1
    while k < n:
        x = x + jnp.where(row >= k, pltpu.roll(x, k, 0), 0.0)
        k *= 2
    return x


def _mlstm_kernel(x_ref, ng_ref, wxz_ref, bxz_ref, cw_ref, cb_ref, wq_ref, wk_ref,
                  wv_ref, wo_ref, bo_ref, wif_ref, bif_ref, mng_ref, msk_ref,
                  wg_ref, bg_ref, wbr_ref, o_ref,
                  xpad, xc_s, xm_s, q_s, k_s, v_s, h_s, g_s, c_s, n_s, m_s, *, tm):
    L = MLSTM_CHUNK

    @pl.when(pl.program_id(1) == 0)
    def _():
        xpad[pl.ds(0, SUBLANES), :] = jnp.zeros((SUBLANES, D_MODEL), F32)
        c_s[...] = jnp.zeros(c_s.shape, F32)
        n_s[...] = jnp.zeros(n_s.shape, F32)
        m_s[...] = jnp.zeros(m_s.shape, F32)

    hb = _rms_norm(x_ref[...], ng_ref[...]).astype(BF16)
    xm = _dot(hb, wxz_ref[:, :D_MODEL]) + bxz_ref[:, :D_MODEL]
    xc = _silu(_causal_conv(xpad, xm, cw_ref, cb_ref, tm))
    xc_s[...] = xc
    xcb = xc.astype(BF16)
    xmb = xm.astype(BF16)
    xm_s[...] = xmb
    for h in range(N_HEADS):
        sl = _head(h)
        q_s[:, sl] = _dot(xcb[:, sl], wq_ref[h]).astype(BF16)
        k_s[:, sl] = _dot(xcb[:, sl], wk_ref[h]).astype(BF16)
        v_s[:, sl] = _dot(xmb[:, sl], wv_ref[h]).astype(BF16)
    g_s[...] = (_dot(q_s[...], wif_ref[0:D_MODEL, :])
                + _dot(k_s[...], wif_ref[D_MODEL:2 * D_MODEL, :])
                + _dot(v_s[...], wif_ref[2 * D_MODEL:3 * D_MODEL, :])
                + bif_ref[...])

    ri = lax.broadcasted_iota(jnp.int32, (L, L), 0)
    ci = lax.broadcasted_iota(jnp.int32, (L, L), 1)
    causal = ri >= ci

    def chunk(c, carry):
        r0 = pl.multiple_of(c * L, L)
        gates = g_s[pl.ds(r0, L), :]
        bcum = _cumsum_rows(_log_sigmoid(gates))
        gates_t = gates.T
        bcum_t = bcum.T
        for h in range(N_HEADS):
            sl = _head(h)
            ig_col = gates[:, h:h + 1]
            b_col = bcum[:, N_HEADS + h:N_HEADS + h + 1]
            ig_row = gates_t[h:h + 1, :]
            b_row = bcum_t[N_HEADS + h:N_HEADS + h + 1, :]
            m_prev = m_s[h, 0:1, 0:1]
            n_prev = n_s[h, 0:1, :]
            c_prev = c_s[h]
            qh = q_s[pl.ds(r0, L), sl]
            kh = (k_s[pl.ds(r0, L), sl] * (HEAD_DIM ** -0.5)).astype(BF16)
            vh = v_s[pl.ds(r0, L), sl]

            dmat = jnp.where(causal, b_col - b_row + ig_row, -jnp.inf)
            inter = b_col + m_prev
            m_t = jnp.maximum(inter, jnp.max(dmat, axis=-1, keepdims=True))
            s = _dot_nt(qh, kh) * jnp.exp(dmat - m_t)
            sc = jnp.exp(inter - m_t)
            num = _dot(s.astype(BF16), vh) + sc * _dot_nt(qh, c_prev.astype(BF16))
            qn = jnp.sum(qh.astype(F32) * n_prev, axis=-1, keepdims=True)
            den = jnp.sum(s, axis=-1, keepdims=True) + sc * qn
            h_s[pl.ds(r0, L), sl] = num / jnp.maximum(jnp.abs(den), jnp.exp(-m_t))

            b_last = b_col[L - 1:L, :]
            g_col = b_last - b_col + ig_col
            m_new = jnp.maximum(b_last + m_prev, jnp.max(g_col, axis=0, keepdims=True))
            wk = jnp.exp(g_col - m_new)
            decay = jnp.exp(b_last + m_prev - m_new)
            wv = (wk * vh.astype(F32)).astype(BF16)
            c_s[h] = decay * c_prev + _dot_tn(wv, kh)
            n_new = decay * n_prev + jnp.sum(wk * kh.astype(F32), axis=0, keepdims=True)
            n_s[h] = jnp.broadcast_to(n_new, (SUBLANES, HEAD_DIM))
            m_s[h] = jnp.broadcast_to(m_new, (SUBLANES, GATE_LANES))
        return carry

    lax.fori_loop(0, tm // L, chunk, 0)

    z = _dot(hb, wxz_ref[:, D_MODEL:]) + bxz_ref[:, D_MODEL:]
    for h in range(N_HEADS):
        sl = _head(h)
        o = _sigmoid(_dot(xm_s[:, sl], wo_ref[h]) + bo_ref[:, sl])
        hh = o * h_s[:, sl]
        mu = jnp.mean(hh, axis=-1, keepdims=True)
        dl = hh - mu
        var = jnp.mean(dl * dl, axis=-1, keepdims=True)
        hn = dl * lax.rsqrt(var + EPS) * mng_ref[:, sl]
        h_s[:, sl] = (hn + msk_ref[:, sl] * xc_s[:, sl]) * _silu(z[:, sl])
    o_ref[...] = _gated_out(hb, h_s[...], wg_ref, bg_ref, wbr_ref)


def _xattn_kernel(x_ref, ng_ref, wqz_ref, bqz_ref, kv_ref, wg_ref, bg_ref, wbr_ref,
                  o_ref, y_s):
    hb = _rms_norm(x_ref[...], ng_ref[...]).astype(BF16)
    q = (_dot(hb, wqz_ref[:, :D_MODEL]) + bqz_ref[:, :D_MODEL]).astype(BF16)
    z = _dot(hb, wqz_ref[:, D_MODEL:]) + bqz_ref[:, D_MODEL:]
    for h in range(N_HEADS):
        sl = _head(h)
        kh = kv_ref[:, sl]
        vh = kv_ref[:, D_MODEL + h * HEAD_DIM:D_MODEL + (h + 1) * HEAD_DIM]
        s = _dot_nt(q[:, sl], kh) * (HEAD_DIM ** -0.5)
        e = jnp.exp(s - jnp.max(s, axis=-1, keepdims=True))
        p = e * (1.0 / jnp.sum(e, axis=-1, keepdims=True))
        y_s[:, sl] = _dot(p.astype(BF16), vh)
    y = y_s[...] * _silu(z)
    o_ref[...] = _gated_out(hb, y, wg_ref, bg_ref, wbr_ref)


def _merge_kernel(x_ref, ca_ref, cb_ref, cc_ref, wo_ref, fg_ref, o_ref):
    merged = ca_ref[...] + cb_ref[...] + cc_ref[...]
    xo = x_ref[...] + _dot(merged.astype(BF16), wo_ref[...])
    o_ref[...] = _rms_norm(xo, fg_ref[...])


def _full(a):
    nd = a.ndim
    return pl.BlockSpec(a.shape, lambda b, i: (0,) * nd)


def _branch_call(body, name, x, weights, scratch, tm, extra_specs=None):
    B, S, D = x.shape
    tok = pl.BlockSpec((None, tm, D), lambda b, i: (b, i, 0))
    specs = [tok] + [_full(w) for w in weights]
    if extra_specs:
        for idx, spec in extra_specs.items():
            specs[idx] = spec
    return pl.pallas_call(
        body,
        grid=(B, S // tm),
        in_specs=specs,
        out_specs=tok,
        out_shape=jax.ShapeDtypeStruct((B, S, D), F32),
        scratch_shapes=scratch,
        compiler_params=pltpu.CompilerParams(
            dimension_semantics=("arbitrary", "arbitrary"),
            vmem_limit_bytes=VMEM_LIMIT_BYTES),
        name=name,
    )(x, *weights)


def kernel(x, mem, norm_g, w_in, b_in, lru_conv_w, lru_conv_b, lru_w_a, lru_b_a, lru_w_x, lru_b_x,
           lru_lambda, m_conv_w, m_conv_b, m_w_q, m_w_k, m_w_v, m_w_o, m_b_o, m_w_if, m_b_if,
           m_norm_g, m_skip, mem_norm_g, w_mem_kv, w_branch, w_out, final_norm_g):
    B, S, D = x.shape
    assert D == D_MODEL and S % TOKEN_TILE == 0 and norm_g.shape[0] == 1
    tm = TOKEN_TILE
    H = N_HEADS
    row = lambda v: v.reshape(1, -1).astype(F32)

    w_in_b = w_in[0].astype(BF16)
    b_in_r = row(b_in[0])
    w_br = w_branch[0].astype(BF16)
    ng = row(norm_g[0])

    def in_cols(lo, hi):
        return w_in_b[:, lo * D:hi * D], b_in_r[:, lo * D:hi * D]

    w_lru, b_lru = in_cols(0, 2)
    w_ml, b_ml = in_cols(2, 4)
    w_xa, b_xa = in_cols(4, 6)
    w_g, b_g = in_cols(6, 9)

    w_kv = w_mem_kv[0].astype(BF16)
    kv = pl.pallas_call(
        _kv_kernel,
        grid=(B,),
        in_specs=[pl.BlockSpec((None, N_MEM, D), lambda b: (b, 0, 0)),
                  pl.BlockSpec((1, D), lambda b: (0, 0)),
                  pl.BlockSpec(w_kv.shape, lambda b: (0, 0))],
        out_specs=pl.BlockSpec((None, N_MEM, 2 * D), lambda b: (b, 0, 0)),
        out_shape=jax.ShapeDtypeStruct((B, N_MEM, 2 * D), BF16),
        compiler_params=pltpu.CompilerParams(vmem_limit_bytes=VMEM_LIMIT_BYTES),
        name="mem_kv",
    )(mem, row(mem_norm_g[0]), w_kv)

    lru_w = [ng, w_lru, b_lru, lru_conv_w[0], row(lru_conv_b[0]),
             lru_w_a[0].astype(BF16), row(lru_b_a[0]), lru_w_x[0].astype(BF16), row(lru_b_x[0]),
             row(lru_lambda[0]), w_g[:, 0:D], b_g[:, 0:D], w_br[0]]
    lru_scratch = [pltpu.VMEM((tm + SUBLANES, D), F32), pltpu.VMEM((tm, D), F32),
                   pltpu.VMEM((tm, D), F32), pltpu.VMEM((SUBLANES, D), F32)]
    c_a = _branch_call(functools.partial(_lru_kernel, tm=tm), "branch_lru", x, lru_w, lru_scratch, tm)

    w_if = jnp.pad(m_w_if[0], ((0, 0), (0, GATE_LANES - 2 * H))).astype(BF16)
    b_if = jnp.pad(m_b_if[0], (0, GATE_LANES - 2 * H)).reshape(1, GATE_LANES).astype(F32)
    ml_w = [ng, w_ml, b_ml, m_conv_w[0], row(m_conv_b[0]),
            m_w_q[0].astype(BF16), m_w_k[0].astype(BF16), m_w_v[0].astype(BF16),
            m_w_o[0].astype(BF16), row(m_b_o[0]), w_if, b_if, row(m_norm_g[0]), row(m_skip[0]),
            w_g[:, D:2 * D], b_g[:, D:2 * D], w_br[1]]
    ml_scratch = [pltpu.VMEM((tm + SUBLANES, D), F32),
                  pltpu.VMEM((tm, D), F32),
                  pltpu.VMEM((tm, D), BF16),
                  pltpu.VMEM((tm, D), BF16),
                  pltpu.VMEM((tm, D), BF16),
                  pltpu.VMEM((tm, D), BF16),
                  pltpu.VMEM((tm, D), F32),
                  pltpu.VMEM((tm, GATE_LANES), F32),
                  pltpu.VMEM((H, HEAD_DIM, HEAD_DIM), F32),
                  pltpu.VMEM((H, SUBLANES, HEAD_DIM), F32),
                  pltpu.VMEM((H, SUBLANES, GATE_LANES), F32)]
    c_b = _branch_call(functools.partial(_mlstm_kernel, tm=tm), "branch_mlstm", x, ml_w, ml_scratch, tm)

    xa_w = [ng, w_xa, b_xa, kv, w_g[:, 2 * D:3 * D], b_g[:, 2 * D:3 * D], w_br[2]]
    kv_spec = pl.BlockSpec((None, N_MEM, 2 * D), lambda b, i: (b, 0, 0))
    c_c = _branch_call(_xattn_kernel, "branch_xattn", x, xa_w, [pltpu.VMEM((tm, D), F32)], tm,
                       extra_specs={4: kv_spec})

    tok = pl.BlockSpec((None, tm, D), lambda b, i: (b, i, 0))
    w_o = w_out[0].astype(BF16)
    return pl.pallas_call(
        _merge_kernel,
        grid=(B, S // tm),
        in_specs=[tok, tok, tok, tok, _full(w_o), pl.BlockSpec((1, D), lambda b, i: (0, 0))],
        out_specs=tok,
        out_shape=jax.ShapeDtypeStruct((B, S, D), F32),
        compiler_params=pltpu.CompilerParams(
            dimension_semantics=("arbitrary", "arbitrary"),
            vmem_limit_bytes=VMEM_LIMIT_BYTES),
        name="merge_out",
    )(x, c_a, c_b, c_c, w_o, row(final_norm_g))
```

```python
import functools

import jax
import jax.numpy as jnp
from jax import lax
from jax.experimental import pallas as pl
from jax.experimental.pallas import tpu as pltpu

D_MODEL = 1024
N_HEADS = 4
HEAD_DIM = D_MODEL // N_HEADS
N_MEM = 256
EPS = 1e-6
CONV_WIDTH = 4
LRU_C = 8.0
MLSTM_CHUNK = 128
LANES = 128
GATE_LANES = 2 * LANES
SUBLANES = 8
TOKEN_TILE = 256
VMEM_LIMIT_BYTES = 56 * 1024 * 1024

BF16 = jnp.bfloat16
F32 = jnp.float32


def _dot(a, b):
    return jnp.dot(a, b, preferred_element_type=F32)


def _dot_nt(a, b):
    return lax.dot_general(a, b, (((1,), (1,)), ((), ())), preferred_element_type=F32)


def _dot_tn(a, b):
    return lax.dot_general(a, b, (((0,), (0,)), ((), ())), preferred_element_type=F32)


def _sigmoid(x):
    return 0.5 * jnp.tanh(0.5 * x) + 0.5


def _silu(x):
    return x * _sigmoid(x)


def _log_sigmoid(x):
    return jnp.minimum(x, 0.0) - jnp.log1p(jnp.exp(-jnp.abs(x)))


def _rms_norm(x, g):
    ms = jnp.mean(x * x, axis=-1, keepdims=True)
    return x * lax.rsqrt(ms + EPS) * g


def _head(h):
    return slice(h * HEAD_DIM, (h + 1) * HEAD_DIM)


def _causal_conv(xpad_ref, px, w_ref, b_ref, tm):
    xpad_ref[pl.ds(SUBLANES, tm), :] = px
    acc = b_ref[...] + w_ref[CONV_WIDTH - 1:CONV_WIDTH, :] * px
    for k in range(CONV_WIDTH - 1):
        shifted = xpad_ref[pl.ds(SUBLANES - (CONV_WIDTH - 1) + k, tm), :]
        acc = acc + w_ref[k:k + 1, :] * shifted
    xpad_ref[pl.ds(0, SUBLANES), :] = xpad_ref[pl.ds(tm, SUBLANES), :]
    return acc


def _gated_out(hb, y, wg_ref, bg_ref, wbr_ref):
    bo = _dot(y.astype(BF16), wbr_ref[...])
    gl = _dot(hb, wg_ref[...]) + bg_ref[...]
    return _sigmoid(gl) * bo


def _kv_kernel(mem_ref, g_ref, w_ref, o_ref):
    mn = _rms_norm(mem_ref[...], g_ref[...]).astype(BF16)
    o_ref[...] = _dot(mn, w_ref[...]).astype(BF16)


def _lru_kernel(x_ref, ng_ref, wxz_ref, bxz_ref, cw_ref, cb_ref, wa_ref, ba_ref,
                wx_ref, bx_ref, lam_ref, wg_ref, bg_ref, wbr_ref, o_ref,
                xpad, a_s, b_s, hcar, *, tm):
    @pl.when(pl.program_id(1) == 0)
    def _():
        xpad[pl.ds(0, SUBLANES), :] = jnp.zeros((SUBLANES, D_MODEL), F32)
        hcar[...] = jnp.zeros((SUBLANES, D_MODEL), F32)

    hb = _rms_norm(x_ref[...], ng_ref[...]).astype(BF16)
    px = _dot(hb, wxz_ref[:, :D_MODEL]) + bxz_ref[:, :D_MODEL]
    u = _causal_conv(xpad, px, cw_ref, cb_ref, tm)
    ub = u.astype(BF16)
    log_sig_lam = _log_sigmoid(lam_ref[...])
    for g in range(N_HEADS):
        sl = _head(g)
        r = _sigmoid(_dot(ub[:, sl], wa_ref[g]) + ba_ref[:, sl])
        ig = _sigmoid(_dot(ub[:, sl], wx_ref[g]) + bx_ref[:, sl])
        log_a = LRU_C * r * log_sig_lam[:, sl]
        a = jnp.exp(log_a)
        a_s[:, sl] = a
        b_s[:, sl] = jnp.sqrt(-jnp.tanh(log_a) * (1.0 + a * a)) * ig * u[:, sl]

    row = lax.broadcasted_iota(jnp.int32, (SUBLANES, D_MODEL), 0)

    def group(j, hprev):
        off = pl.multiple_of(j * SUBLANES, SUBLANES)
        a = a_s[pl.ds(off, SUBLANES), :]
        b = b_s[pl.ds(off, SUBLANES), :]
        for k in (1, 2, 4):
            valid = row >= k
            a_sh = pltpu.roll(a, k, 0)
            b_sh = pltpu.roll(b, k, 0)
            b = b + a * jnp.where(valid, b_sh, 0.0)
            a = a * jnp.where(valid, a_sh, 1.0)
        h = b + a * hprev
        b_s[pl.ds(off, SUBLANES), :] = h
        return jnp.broadcast_to(h[SUBLANES - 1:SUBLANES, :], (SUBLANES, D_MODEL))

    hcar[...] = lax.fori_loop(0, tm // SUBLANES, group, hcar[...])

    pz = _dot(hb, wxz_ref[:, D_MODEL:]) + bxz_ref[:, D_MODEL:]
    y = b_s[...] * _silu(pz)
    o_ref[...] = _gated_out(hb, y, wg_ref, bg_ref, wbr_ref)


def _prefix_rows(x, op, identity):
    n = x.shape[0]
    row = lax.broadcasted_iota(jnp.int32, x.shape, 0)
    k = 1
    while k < n:
        x = op(x, jnp.where(row >= k, pltpu.roll(x, k, 0), identity))
        k *= 2
    return x


def _lane_tile(x, n):
    reps = n // LANES
    return x if reps == 1 else jnp.concatenate([x] * reps, axis=-1)


def _lane_bcast(x, lane):
    return jnp.broadcast_to(x[:, lane:lane + 1], x.shape)


def _mlstm_kernel(x_ref, ng_ref, wxz_ref, bxz_ref, cw_ref, cb_ref, wq_ref, wk_ref,
                  wv_ref, wo_ref, bo_ref, wif_ref, bif_ref, mng_ref, msk_ref,
                  wg_ref, bg_ref, wbr_ref, o_ref,
                  xpad, xc_s, xm_s, q_s, k_s, v_s, h_s, g_s, st_s, m_s, *, tm):
    L = MLSTM_CHUNK

    @pl.when(pl.program_id(1) == 0)
    def _():
        xpad[pl.ds(0, SUBLANES), :] = jnp.zeros((SUBLANES, D_MODEL), F32)
        st_s[...] = jnp.zeros(st_s.shape, F32)
        m_s[...] = jnp.zeros(m_s.shape, F32)

    hb = _rms_norm(x_ref[...], ng_ref[...]).astype(BF16)
    xm = _dot(hb, wxz_ref[:, :D_MODEL]) + bxz_ref[:, :D_MODEL]
    xc = _silu(_causal_conv(xpad, xm, cw_ref, cb_ref, tm))
    xc_s[...] = xc
    xcb = xc.astype(BF16)
    xmb = xm.astype(BF16)
    xm_s[...] = xmb
    for h in range(N_HEADS):
        sl = _head(h)
        q_s[:, sl] = _dot(xcb[:, sl], wq_ref[h]).astype(BF16)
        k_s[:, sl] = _dot(xcb[:, sl], wk_ref[h]).astype(BF16)
        v_s[:, sl] = _dot(xmb[:, sl], wv_ref[h]).astype(BF16)
    g_s[...] = (_dot(q_s[...], wif_ref[0:D_MODEL, :])
                + _dot(k_s[...], wif_ref[D_MODEL:2 * D_MODEL, :])
                + _dot(v_s[...], wif_ref[2 * D_MODEL:3 * D_MODEL, :])
                + bif_ref[...])

    ri = lax.broadcasted_iota(jnp.int32, (L, L), 0)
    ci = lax.broadcasted_iota(jnp.int32, (L, L), 1)
    causal = ri >= ci

    def chunk(c, carry):
        r0 = pl.multiple_of(c * L, L)
        ig = g_s[pl.ds(r0, L), 0:LANES]
        bcum = _prefix_rows(_log_sigmoid(g_s[pl.ds(r0, L), LANES:2 * LANES]), jnp.add, 0.0)
        gq = ig - bcum
        cmax = _prefix_rows(gq, jnp.maximum, -jnp.inf)
        gq_t = gq.T
        for h in range(N_HEADS):
            sl = _head(h)
            b_rep = _lane_bcast(bcum, h)
            gq_rep = _lane_bcast(gq, h)
            gq_row = gq_t[h:h + 1, :]
            m_prev = m_s[h, 0:1, :]
            mm = jnp.maximum(m_prev, _lane_bcast(cmax, h))
            mm_last = mm[L - 1:L, :]
            p = jnp.exp(jnp.where(causal, gq_row - _lane_tile(mm, L), -jnp.inf))
            sc = jnp.exp(m_prev - mm)
            qh = q_s[pl.ds(r0, L), sl]
            kh = (k_s[pl.ds(r0, L), sl] * (HEAD_DIM ** -0.5)).astype(BF16)
            vh = v_s[pl.ds(r0, L), sl]
            kt = kh.T
            st = st_s[h]
            stb = st.astype(BF16)

            qk = _dot(qh, jnp.concatenate([kt, stb[:, HEAD_DIM:]], axis=1))
            s = qk[:, :L] * p
            num = _dot(s.astype(BF16), vh) + _lane_tile(sc, HEAD_DIM) * _dot(qh, stb[:, :HEAD_DIM])
            den = jnp.sum(s, axis=-1, keepdims=True) + sc * qk[:, L:]
            inv = 1.0 / jnp.maximum(jnp.abs(den), jnp.exp(-(b_rep + mm)))
            h_s[pl.ds(r0, L), sl] = num * _lane_tile(inv, HEAD_DIM)

            wk = jnp.exp(gq_rep - mm_last)
            wkv = jnp.concatenate([_lane_tile(wk, HEAD_DIM) * vh.astype(F32), wk], axis=1).astype(BF16)
            decay = sc[L - 1:L, :]
            st_s[h] = _lane_tile(decay, HEAD_DIM + LANES) * st + _dot(kt, wkv)
            m_s[h] = jnp.broadcast_to(b_rep[L - 1:L, :] + mm_last, (SUBLANES, LANES))
        return carry

    lax.fori_loop(0, tm // L, chunk, 0)

    z = _dot(hb, wxz_ref[:, D_MODEL:]) + bxz_ref[:, D_MODEL:]
    for h in range(N_HEADS):
        sl = _head(h)
        o = _sigmoid(_dot(xm_s[:, sl], wo_ref[h]) + bo_ref[:, sl])
        hh = o * h_s[:, sl]
        mu = jnp.mean(hh, axis=-1, keepdims=True)
        dl = hh - mu
        var = jnp.mean(dl * dl, axis=-1, keepdims=True)
        hn = dl * lax.rsqrt(var + EPS) * mng_ref[:, sl]
        h_s[:, sl] = (hn + msk_ref[:, sl] * xc_s[:, sl]) * _silu(z[:, sl])
    o_ref[...] = _gated_out(hb, h_s[...], wg_ref, bg_ref, wbr_ref)


def _xattn_kernel(x_ref, ng_ref, wqz_ref, bqz_ref, kv_ref, wg_ref, bg_ref, wbr_ref,
                  o_ref, y_s):
    hb = _rms_norm(x_ref[...], ng_ref[...]).astype(BF16)
    q = (_dot(hb, wqz_ref[:, :D_MODEL]) + bqz_ref[:, :D_MODEL]).astype(BF16)
    z = _dot(hb, wqz_ref[:, D_MODEL:]) + bqz_ref[:, D_MODEL:]
    for h in range(N_HEADS):
        sl = _head(h)
        kh = kv_ref[:, sl]
        vh = kv_ref[:, D_MODEL + h * HEAD_DIM:D_MODEL + (h + 1) * HEAD_DIM]
        s = _dot_nt(q[:, sl], kh) * (HEAD_DIM ** -0.5)
        e = jnp.exp(s - jnp.max(s, axis=-1, keepdims=True))
        p = e * (1.0 / jnp.sum(e, axis=-1, keepdims=True))
        y_s[:, sl] = _dot(p.astype(BF16), vh)
    y = y_s[...] * _silu(z)
    o_ref[...] = _gated_out(hb, y, wg_ref, bg_ref, wbr_ref)


def _merge_kernel(x_ref, ca_ref, cb_ref, cc_ref, wo_ref, fg_ref, o_ref):
    merged = ca_ref[...] + cb_ref[...] + cc_ref[...]
    xo = x_ref[...] + _dot(merged.astype(BF16), wo_ref[...])
    o_ref[...] = _rms_norm(xo, fg_ref[...])


def _full(a):
    nd = a.ndim
    return pl.BlockSpec(a.shape, lambda b, i: (0,) * nd)


def _branch_call(body, name, x, weights, scratch, tm, extra_specs=None):
    B, S, D = x.shape
    tok = pl.BlockSpec((None, tm, D), lambda b, i: (b, i, 0))
    specs = [tok] + [_full(w) for w in weights]
    if extra_specs:
        for idx, spec in extra_specs.items():
            specs[idx] = spec
    return pl.pallas_call(
        body,
        grid=(B, S // tm),
        in_specs=specs,
        out_specs=tok,
        out_shape=jax.ShapeDtypeStruct((B, S, D), F32),
        scratch_shapes=scratch,
        compiler_params=pltpu.CompilerParams(
            dimension_semantics=("arbitrary", "arbitrary"),
            vmem_limit_bytes=VMEM_LIMIT_BYTES),
        name=name,
    )(x, *weights)


def kernel(x, mem, norm_g, w_in, b_in, lru_conv_w, lru_conv_b, lru_w_a, lru_b_a, lru_w_x, lru_b_x,
           lru_lambda, m_conv_w, m_conv_b, m_w_q, m_w_k, m_w_v, m_w_o, m_b_o, m_w_if, m_b_if,
           m_norm_g, m_skip, mem_norm_g, w_mem_kv, w_branch, w_out, final_norm_g):
    B, S, D = x.shape
    assert D == D_MODEL and S % TOKEN_TILE == 0 and norm_g.shape[0] == 1
    tm = TOKEN_TILE
    H = N_HEADS
    row = lambda v: v.reshape(1, -1).astype(F32)

    w_in_b = w_in[0].astype(BF16)
    b_in_r = row(b_in[0])
    w_br = w_branch[0].astype(BF16)
    ng = row(norm_g[0])

    def in_cols(lo, hi):
        return w_in_b[:, lo * D:hi * D], b_in_r[:, lo * D:hi * D]

    w_lru, b_lru = in_cols(0, 2)
    w_ml, b_ml = in_cols(2, 4)
    w_xa, b_xa = in_cols(4, 6)
    w_g, b_g = in_cols(6, 9)

    w_kv = w_mem_kv[0].astype(BF16)
    kv = pl.pallas_call(
        _kv_kernel,
        grid=(B,),
        in_specs=[pl.BlockSpec((None, N_MEM, D), lambda b: (b, 0, 0)),
                  pl.BlockSpec((1, D), lambda b: (0, 0)),
                  pl.BlockSpec(w_kv.shape, lambda b: (0, 0))],
        out_specs=pl.BlockSpec((None, N_MEM, 2 * D), lambda b: (b, 0, 0)),
        out_shape=jax.ShapeDtypeStruct((B, N_MEM, 2 * D), BF16),
        compiler_params=pltpu.CompilerParams(vmem_limit_bytes=VMEM_LIMIT_BYTES),
        name="mem_kv",
    )(mem, row(mem_norm_g[0]), w_kv)

    lru_w = [ng, w_lru, b_lru, lru_conv_w[0], row(lru_conv_b[0]),
             lru_w_a[0].astype(BF16), row(lru_b_a[0]), lru_w_x[0].astype(BF16), row(lru_b_x[0]),
             row(lru_lambda[0]), w_g[:, 0:D], b_g[:, 0:D], w_br[0]]
    lru_scratch = [pltpu.VMEM((tm + SUBLANES, D), F32), pltpu.VMEM((tm, D), F32),
                   pltpu.VMEM((tm, D), F32), pltpu.VMEM((SUBLANES, D), F32)]
    c_a = _branch_call(functools.partial(_lru_kernel, tm=tm), "branch_lru", x, lru_w, lru_scratch, tm)

    pad_lanes = lambda a: jnp.pad(a, ((0, 0), (0, LANES - H)))
    w_if = jnp.concatenate([pad_lanes(m_w_if[0][:, :H]), pad_lanes(m_w_if[0][:, H:])], axis=1).astype(BF16)
    b_if2 = m_b_if[0].reshape(1, 2 * H).astype(F32)
    b_if = jnp.concatenate([pad_lanes(b_if2[:, :H]), pad_lanes(b_if2[:, H:])], axis=1)
    ml_w = [ng, w_ml, b_ml, m_conv_w[0], row(m_conv_b[0]),
            m_w_q[0].astype(BF16), m_w_k[0].astype(BF16), m_w_v[0].astype(BF16),
            m_w_o[0].astype(BF16), row(m_b_o[0]), w_if, b_if, row(m_norm_g[0]), row(m_skip[0]),
            w_g[:, D:2 * D], b_g[:, D:2 * D], w_br[1]]
    ml_scratch = [pltpu.VMEM((tm + SUBLANES, D), F32),
                  pltpu.VMEM((tm, D), F32),
                  pltpu.VMEM((tm, D), BF16),
                  pltpu.VMEM((tm, D), BF16),
                  pltpu.VMEM((tm, D), BF16),
                  pltpu.VMEM((tm, D), BF16),
                  pltpu.VMEM((tm, D), F32),
                  pltpu.VMEM((tm, GATE_LANES), F32),
                  pltpu.VMEM((H, HEAD_DIM, HEAD_DIM + LANES), F32),
                  pltpu.VMEM((H, SUBLANES, LANES), F32)]
    c_b = _branch_call(functools.partial(_mlstm_kernel, tm=tm), "branch_mlstm", x, ml_w, ml_scratch, tm)

    xa_w = [ng, w_xa, b_xa, kv, w_g[:, 2 * D:3 * D], b_g[:, 2 * D:3 * D], w_br[2]]
    kv_spec = pl.BlockSpec((None, N_MEM, 2 * D), lambda b, i: (b, 0, 0))
    c_c = _branch_call(_xattn_kernel, "branch_xattn", x, xa_w, [pltpu.VMEM((tm, D), F32)], tm,
                       extra_specs={4: kv_spec})

    tok = pl.BlockSpec((None, tm, D), lambda b, i: (b, i, 0))
    w_o = w_out[0].astype(BF16)
    return pl.pallas_call(
        _merge_kernel,
        grid=(B, S // tm),
        in_specs=[tok, tok, tok, tok, _full(w_o), pl.BlockSpec((1, D), lambda b, i: (0, 0))],
        out_specs=tok,
        out_shape=jax.ShapeDtypeStruct((B, S, D), F32),
        compiler_params=pltpu.CompilerParams(
            dimension_semantics=("arbitrary", "arbitrary"),
            vmem_limit_bytes=VMEM_LIMIT_BYTES),
        name="merge_out",
    )(x, c_a, c_b, c_c, w_o, row(final_norm_g))
```

```python
import functools

import jax
import jax.numpy as jnp
from jax import lax
from jax.experimental import pallas as pl
from jax.experimental.pallas import tpu as pltpu

D_MODEL = 1024
N_HEADS = 4
HEAD_DIM = D_MODEL // N_HEADS
N_MEM = 256
EPS = 1e-6
CONV_WIDTH = 4
LRU_C = 8.0
MLSTM_CHUNK = 128
LANES = 128
GATE_LANES = 2 * LANES
SUBLANES = 8
TOKEN_TILE = 256
VMEM_LIMIT_BYTES = 56 * 1024 * 1024

BF16 = jnp.bfloat16
F32 = jnp.float32


def _dot(a, b):
    return jnp.dot(a, b, preferred_element_type=F32)


def _dot_nt(a, b):
    return lax.dot_general(a, b, (((1,), (1,)), ((), ())), preferred_element_type=F32)


def _dot_tn(a, b):
    return lax.dot_general(a, b, (((0,), (0,)), ((), ())), preferred_element_type=F32)


def _sigmoid(x):
    return 0.5 * jnp.tanh(0.5 * x) + 0.5


def _silu(x):
    return x * _sigmoid(x)


def _log_sigmoid(x):
    return jnp.minimum(x, 0.0) - jnp.log1p(jnp.exp(-jnp.abs(x)))


def _rms_norm(x, g):
    ms = jnp.mean(x * x, axis=-1, keepdims=True)
    return x * lax.rsqrt(ms + EPS) * g


def _head(h):
    return slice(h * HEAD_DIM, (h + 1) * HEAD_DIM)


def _causal_conv(xpad_ref, px, w_ref, b_ref, tm):
    xpad_ref[pl.ds(SUBLANES, tm), :] = px
    acc = b_ref[...] + w_ref[CONV_WIDTH - 1:CONV_WIDTH, :] * px
    for k in range(CONV_WIDTH - 1):
        shifted = xpad_ref[pl.ds(SUBLANES - (CONV_WIDTH - 1) + k, tm), :]
        acc = acc + w_ref[k:k + 1, :] * shifted
    xpad_ref[pl.ds(0, SUBLANES), :] = xpad_ref[pl.ds(tm, SUBLANES), :]
    return acc


def _gated_out(hb, y, wg_ref, bg_ref, wbr_ref):
    bo = _dot(y.astype(BF16), wbr_ref[...])
    gl = _dot(hb, wg_ref[...]) + bg_ref[...]
    return _sigmoid(gl) * bo


def _kv_kernel(mem_ref, g_ref, w_ref, o_ref):
    mn = _rms_norm(mem_ref[...], g_ref[...]).astype(BF16)
    o_ref[...] = _dot(mn, w_ref[...]).astype(BF16)


def _lru_kernel(x_ref, ng_ref, wxz_ref, bxz_ref, cw_ref, cb_ref, wa_ref, ba_ref,
                wx_ref, bx_ref, lam_ref, wg_ref, bg_ref, wbr_ref, o_ref,
                xpad, a_s, b_s, z_s, hcar, *, tm):
    @pl.when(pl.program_id(1) == 0)
    def _():
        xpad[pl.ds(0, SUBLANES), :] = jnp.zeros((SUBLANES, D_MODEL), F32)
        hcar[...] = jnp.zeros((SUBLANES, D_MODEL), F32)

    hb = _rms_norm(x_ref[...], ng_ref[...]).astype(BF16)
    px = _dot(hb, wxz_ref[:, :D_MODEL]) + bxz_ref[:, :D_MODEL]
    u = _causal_conv(xpad, px, cw_ref, cb_ref, tm)
    ub = u.astype(BF16)
    log_sig_lam = _log_sigmoid(lam_ref[...])
    for g in range(N_HEADS):
        sl = _head(g)
        r = _sigmoid(_dot(ub[:, sl], wa_ref[g]) + ba_ref[:, sl])
        ig = _sigmoid(_dot(ub[:, sl], wx_ref[g]) + bx_ref[:, sl])
        log_a = LRU_C * r * log_sig_lam[:, sl]
        a = jnp.exp(log_a)
        a_s[:, sl] = a
        b_s[:, sl] = jnp.sqrt(-jnp.tanh(log_a) * (1.0 + a * a)) * ig * u[:, sl]

    row = lax.broadcasted_iota(jnp.int32, (SUBLANES, D_MODEL), 0)

    def group(j, hprev):
        off = pl.multiple_of(j * SUBLANES, SUBLANES)
        a = a_s[pl.ds(off, SUBLANES), :]
        b = b_s[pl.ds(off, SUBLANES), :]
        for k in (1, 2, 4):
            valid = row >= k
            a_sh = pltpu.roll(a, k, 0)
            b_sh = pltpu.roll(b, k, 0)
            b = b + a * jnp.where(valid, b_sh, 0.0)
            a = a * jnp.where(valid, a_sh, 1.0)
        h = b + a * hprev
        b_s[pl.ds(off, SUBLANES), :] = h
        return jnp.broadcast_to(h[SUBLANES - 1:SUBLANES, :], (SUBLANES, D_MODEL))

    o_ref[...] = _sigmoid(_dot(hb, wg_ref[...]) + bg_ref[...])
    z_s[...] = _silu(_dot(hb, wxz_ref[:, D_MODEL:]) + bxz_ref[:, D_MODEL:])

    hcar[...] = lax.fori_loop(0, tm // SUBLANES, group, hcar[...])

    y = b_s[...] * z_s[...]
    o_ref[...] = o_ref[...] * _dot(y.astype(BF16), wbr_ref[...])


def _prefix_rows(x, op, identity):
    n = x.shape[0]
    row = lax.broadcasted_iota(jnp.int32, x.shape, 0)
    k = 1
    while k < n:
        x = op(x, jnp.where(row >= k, pltpu.roll(x, k, 0), identity))
        k *= 2
    return x


def _lane_tile(x, n):
    reps = n // LANES
    return x if reps == 1 else jnp.concatenate([x] * reps, axis=-1)


def _lane_bcast(x, lane):
    return jnp.broadcast_to(x[:, lane:lane + 1], x.shape)


def _mlstm_kernel(x_ref, ng_ref, wxz_ref, bxz_ref, cw_ref, cb_ref, wq_ref, wk_ref,
                  wv_ref, wo_ref, bo_ref, wif_ref, bif_ref, mng_ref, msk_ref,
                  wg_ref, bg_ref, wbr_ref, o_ref,
                  xpad, xc_s, xm_s, q_s, k_s, v_s, h_s, g_s, z_s, og_s, st_s, m_s, *, tm):
    L = MLSTM_CHUNK

    @pl.when(pl.program_id(1) == 0)
    def _():
        xpad[pl.ds(0, SUBLANES), :] = jnp.zeros((SUBLANES, D_MODEL), F32)
        st_s[...] = jnp.zeros(st_s.shape, F32)
        m_s[...] = jnp.zeros(m_s.shape, F32)

    hb = _rms_norm(x_ref[...], ng_ref[...]).astype(BF16)
    xm = _dot(hb, wxz_ref[:, :D_MODEL]) + bxz_ref[:, :D_MODEL]
    xc = _silu(_causal_conv(xpad, xm, cw_ref, cb_ref, tm))
    xc_s[...] = xc
    xcb = xc.astype(BF16)
    xmb = xm.astype(BF16)
    xm_s[...] = xmb
    for h in range(N_HEADS):
        sl = _head(h)
        q_s[:, sl] = _dot(xcb[:, sl], wq_ref[h]).astype(BF16)
        k_s[:, sl] = _dot(xcb[:, sl], wk_ref[h]).astype(BF16)
        v_s[:, sl] = _dot(xmb[:, sl], wv_ref[h]).astype(BF16)
    g_s[...] = (_dot(q_s[...], wif_ref[0:D_MODEL, :])
                + _dot(k_s[...], wif_ref[D_MODEL:2 * D_MODEL, :])
                + _dot(v_s[...], wif_ref[2 * D_MODEL:3 * D_MODEL, :])
                + bif_ref[...])

    ri = lax.broadcasted_iota(jnp.int32, (L, L), 0)
    ci = lax.broadcasted_iota(jnp.int32, (L, L), 1)
    causal = ri >= ci

    def chunk(c, carry):
        r0 = c * L
        ig = g_s[pl.ds(r0, L), 0:LANES]
        bcum = _prefix_rows(_log_sigmoid(g_s[pl.ds(r0, L), LANES:2 * LANES]), jnp.add, 0.0)
        gq = ig - bcum
        cmax = _prefix_rows(gq, jnp.maximum, -jnp.inf)
        gq_t = gq.T
        for h in range(N_HEADS):
            sl = _head(h)
            b_rep = _lane_bcast(bcum, h)
            gq_rep = _lane_bcast(gq, h)
            gq_row = gq_t[h:h + 1, :]
            m_prev = m_s[h, 0:1, :]
            mm = jnp.maximum(m_prev, _lane_bcast(cmax, h))
            mm_last = mm[L - 1:L, :]
            p = jnp.exp(jnp.where(causal, gq_row - _lane_tile(mm, L), -jnp.inf))
            sc = jnp.exp(m_prev - mm)
            qh = q_s[pl.ds(r0, L), sl]
            kh = (k_s[pl.ds(r0, L), sl] * (HEAD_DIM ** -0.5)).astype(BF16)
            vh = v_s[pl.ds(r0, L), sl]
            kt = kh.T
            st = st_s[h]
            stb = st.astype(BF16)

            qk = _dot(qh, jnp.concatenate([kt, stb[:, HEAD_DIM:]], axis=1))
            s = qk[:, :L] * p
            num = _dot(s.astype(BF16), vh) + _lane_tile(sc, HEAD_DIM) * _dot(qh, stb[:, :HEAD_DIM])
            den = jnp.sum(s, axis=-1, keepdims=True) + sc * qk[:, L:]
            inv = 1.0 / jnp.maximum(jnp.abs(den), jnp.exp(-(b_rep + mm)))
            h_s[pl.ds(r0, L), sl] = num * _lane_tile(inv, HEAD_DIM)

            wk = jnp.exp(gq_rep - mm_last)
            wkv = jnp.concatenate([_lane_tile(wk, HEAD_DIM) * vh.astype(F32), wk], axis=1).astype(BF16)
            decay = sc[L - 1:L, :]
            st_s[h] = _lane_tile(decay, HEAD_DIM + LANES) * st + _dot(kt, wkv)
            m_s[h] = jnp.broadcast_to(b_rep[L - 1:L, :] + mm_last, (SUBLANES, LANES))
        return carry

    o_ref[...] = _sigmoid(_dot(hb, wg_ref[...]) + bg_ref[...])
    z_s[...] = _silu(_dot(hb, wxz_ref[:, D_MODEL:]) + bxz_ref[:, D_MODEL:])
    for h in range(N_HEADS):
        sl = _head(h)
        og_s[:, sl] = _sigmoid(_dot(xm_s[:, sl], wo_ref[h]) + bo_ref[:, sl])

    for c in range(tm // L):
        chunk(c, 0)

    for h in range(N_HEADS):
        sl = _head(h)
        hh = og_s[:, sl] * h_s[:, sl]
        mu = jnp.mean(hh, axis=-1, keepdims=True)
        dl = hh - mu
        var = jnp.mean(dl * dl, axis=-1, keepdims=True)
        hn = dl * lax.rsqrt(var + EPS) * mng_ref[:, sl]
        h_s[:, sl] = (hn + msk_ref[:, sl] * xc_s[:, sl]) * z_s[:, sl]
    o_ref[...] = o_ref[...] * _dot(h_s[...].astype(BF16), wbr_ref[...])


def _xattn_kernel(x_ref, ng_ref, wqz_ref, bqz_ref, kv_ref, wg_ref, bg_ref, wbr_ref,
                  o_ref, y_s):
    hb = _rms_norm(x_ref[...], ng_ref[...]).astype(BF16)
    q = (_dot(hb, wqz_ref[:, :D_MODEL]) + bqz_ref[:, :D_MODEL]).astype(BF16)
    z = _dot(hb, wqz_ref[:, D_MODEL:]) + bqz_ref[:, D_MODEL:]
    for h in range(N_HEADS):
        sl = _head(h)
        kh = kv_ref[:, sl]
        vh = kv_ref[:, D_MODEL + h * HEAD_DIM:D_MODEL + (h + 1) * HEAD_DIM]
        s = _dot_nt(q[:, sl], kh) * (HEAD_DIM ** -0.5)
        e = jnp.exp(s - jnp.max(s, axis=-1, keepdims=True))
        p = e * (1.0 / jnp.sum(e, axis=-1, keepdims=True))
        y_s[:, sl] = _dot(p.astype(BF16), vh)
    y = y_s[...] * _silu(z)
    o_ref[...] = _gated_out(hb, y, wg_ref, bg_ref, wbr_ref)


def _merge_kernel(x_ref, ca_ref, cb_ref, cc_ref, wo_ref, fg_ref, o_ref):
    merged = ca_ref[...] + cb_ref[...] + cc_ref[...]
    xo = x_ref[...] + _dot(merged.astype(BF16), wo_ref[...])
    o_ref[...] = _rms_norm(xo, fg_ref[...])


def _full(a):
    nd = a.ndim
    return pl.BlockSpec(a.shape, lambda b, i: (0,) * nd)


def _branch_call(body, name, x, weights, scratch, tm, extra_specs=None):
    B, S, D = x.shape
    tok = pl.BlockSpec((None, tm, D), lambda b, i: (b, i, 0))
    specs = [tok] + [_full(w) for w in weights]
    if extra_specs:
        for idx, spec in extra_specs.items():
            specs[idx] = spec
    return pl.pallas_call(
        body,
        grid=(B, S // tm),
        in_specs=specs,
        out_specs=tok,
        out_shape=jax.ShapeDtypeStruct((B, S, D), F32),
        scratch_shapes=scratch,
        compiler_params=pltpu.CompilerParams(
            dimension_semantics=("arbitrary", "arbitrary"),
            vmem_limit_bytes=VMEM_LIMIT_BYTES),
        name=name,
    )(x, *weights)


def kernel(x, mem, norm_g, w_in, b_in, lru_conv_w, lru_conv_b, lru_w_a, lru_b_a, lru_w_x, lru_b_x,
           lru_lambda, m_conv_w, m_conv_b, m_w_q, m_w_k, m_w_v, m_w_o, m_b_o, m_w_if, m_b_if,
           m_norm_g, m_skip, mem_norm_g, w_mem_kv, w_branch, w_out, final_norm_g):
    B, S, D = x.shape
    assert D == D_MODEL and S % TOKEN_TILE == 0 and norm_g.shape[0] == 1
    tm = TOKEN_TILE
    H = N_HEADS
    row = lambda v: v.reshape(1, -1).astype(F32)

    w_in_b = w_in[0].astype(BF16)
    b_in_r = row(b_in[0])
    w_br = w_branch[0].astype(BF16)
    ng = row(norm_g[0])

    def in_cols(lo, hi):
        return w_in_b[:, lo * D:hi * D], b_in_r[:, lo * D:hi * D]

    w_lru, b_lru = in_cols(0, 2)
    w_ml, b_ml = in_cols(2, 4)
    w_xa, b_xa = in_cols(4, 6)
    w_g, b_g = in_cols(6, 9)

    w_kv = w_mem_kv[0].astype(BF16)
    kv = pl.pallas_call(
        _kv_kernel,
        grid=(B,),
        in_specs=[pl.BlockSpec((None, N_MEM, D), lambda b: (b, 0, 0)),
                  pl.BlockSpec((1, D), lambda b: (0, 0)),
                  pl.BlockSpec(w_kv.shape, lambda b: (0, 0))],
        out_specs=pl.BlockSpec((None, N_MEM, 2 * D), lambda b: (b, 0, 0)),
        out_shape=jax.ShapeDtypeStruct((B, N_MEM, 2 * D), BF16),
        compiler_params=pltpu.CompilerParams(vmem_limit_bytes=VMEM_LIMIT_BYTES),
        name="mem_kv",
    )(mem, row(mem_norm_g[0]), w_kv)

    lru_w = [ng, w_lru, b_lru, lru_conv_w[0], row(lru_conv_b[0]),
             lru_w_a[0].astype(BF16), row(lru_b_a[0]), lru_w_x[0].astype(BF16), row(lru_b_x[0]),
             row(lru_lambda[0]), w_g[:, 0:D], b_g[:, 0:D], w_br[0]]
    lru_scratch = [pltpu.VMEM((tm + SUBLANES, D), F32), pltpu.VMEM((tm, D), F32),
                   pltpu.VMEM((tm, D), F32), pltpu.VMEM((tm, D), F32), pltpu.VMEM((SUBLANES, D), F32)]
    c_a = _branch_call(functools.partial(_lru_kernel, tm=tm), "branch_lru", x, lru_w, lru_scratch, tm)

    pad_lanes = lambda a: jnp.pad(a, ((0, 0), (0, LANES - H)))
    w_if = jnp.concatenate([pad_lanes(m_w_if[0][:, :H]), pad_lanes(m_w_if[0][:, H:])], axis=1).astype(BF16)
    b_if2 = m_b_if[0].reshape(1, 2 * H).astype(F32)
    b_if = jnp.concatenate([pad_lanes(b_if2[:, :H]), pad_lanes(b_if2[:, H:])], axis=1)
    ml_w = [ng, w_ml, b_ml, m_conv_w[0], row(m_conv_b[0]),
            m_w_q[0].astype(BF16), m_w_k[0].astype(BF16), m_w_v[0].astype(BF16),
            m_w_o[0].astype(BF16), row(m_b_o[0]), w_if, b_if, row(m_norm_g[0]), row(m_skip[0]),
            w_g[:, D:2 * D], b_g[:, D:2 * D], w_br[1]]
    ml_scratch = [pltpu.VMEM((tm + SUBLANES, D), F32),
                  pltpu.VMEM((tm, D), F32),
                  pltpu.VMEM((tm, D), BF16),
                  pltpu.VMEM((tm, D), BF16),
                  pltpu.VMEM((tm, D), BF16),
                  pltpu.VMEM((tm, D), BF16),
                  pltpu.VMEM((tm, D), F32),
                  pltpu.VMEM((tm, GATE_LANES), F32),
                  pltpu.VMEM((tm, D), F32),
                  pltpu.VMEM((tm, D), F32),
                  pltpu.VMEM((H, HEAD_DIM, HEAD_DIM + LANES), F32),
                  pltpu.VMEM((H, SUBLANES, LANES), F32)]
    c_b = _branch_call(functools.partial(_mlstm_kernel, tm=tm), "branch_mlstm", x, ml_w, ml_scratch, tm)

    xa_w = [ng, w_xa, b_xa, kv, w_g[:, 2 * D:3 * D], b_g[:, 2 * D:3 * D], w_br[2]]
    kv_spec = pl.BlockSpec((None, N_MEM, 2 * D), lambda b, i: (b, 0, 0))
    c_c = _branch_call(_xattn_kernel, "branch_xattn", x, xa_w, [pltpu.VMEM((tm, D), F32)], tm,
                       extra_specs={4: kv_spec})

    tok = pl.BlockSpec((None, tm, D), lambda b, i: (b, i, 0))
    w_o = w_out[0].astype(BF16)
    return pl.pallas_call(
        _merge_kernel,
        grid=(B, S // tm),
        in_specs=[tok, tok, tok, tok, _full(w_o), pl.BlockSpec((1, D), lambda b, i: (0, 0))],
        out_specs=tok,
        out_shape=jax.ShapeDtypeStruct((B, S, D), F32),
        compiler_params=pltpu.CompilerParams(
            dimension_semantics=("arbitrary", "arbitrary"),
            vmem_limit_bytes=VMEM_LIMIT_BYTES),
        name="merge_out",
    )(x, c_a, c_b, c_c, w_o, row(final_norm_g))
```

```python
import functools

import jax
import jax.numpy as jnp
from jax import lax
from jax.experimental import pallas as pl
from jax.experimental.pallas import tpu as pltpu

D_MODEL = 1024
N_HEADS = 4
HEAD_DIM = D_MODEL // N_HEADS
N_MEM = 256
N_BRANCH = 3
EPS = 1e-6
CONV_WIDTH = 4
LRU_C = 8.0
LANES = 128
SUBLANES = 8
MXU_COLS = 256
TIME_TILE = 128
VMEM_LIMIT_BYTES = 60 * 1024 * 1024

LRU_X, LRU_Z, M_X, M_Z, XA_Q, XA_Z, GATE = 0, 4, 8, 12, 16, 20, 24

BF16 = jnp.bfloat16
F32 = jnp.float32


def _dot(a, b):
    return jnp.dot(a, b, preferred_element_type=F32)


def _dot_nt(a, b):
    return lax.dot_general(a, b, (((1,), (1,)), ((), ())), preferred_element_type=F32)


def _sigmoid(x):
    return 0.5 * jnp.tanh(0.5 * x) + 0.5


def _silu(x):
    return x * _sigmoid(x)


def _log_sigmoid(x):
    return jnp.minimum(x, 0.0) - jnp.log1p(jnp.exp(-jnp.abs(x)))


def _rms_norm(x, g):
    ms = jnp.mean(x * x, axis=-1, keepdims=True)
    return x * lax.rsqrt(ms + EPS) * g


def _blk(j):
    return slice(j * HEAD_DIM, (j + 1) * HEAD_DIM)


def _prefix_rows(x, op, identity):
    n = x.shape[0]
    row = lax.broadcasted_iota(jnp.int32, x.shape, 0)
    k = 1
    while k < n:
        x = op(x, jnp.where(row >= k, pltpu.roll(x, k, 0), identity))
        k *= 2
    return x


def _lane_tile(x, n):
    reps = n // LANES
    return x if reps == 1 else jnp.concatenate([x] * reps, axis=-1)


def _lane_bcast(x, lane):
    return jnp.broadcast_to(x[:, lane:lane + 1], x.shape)


def _causal_conv(xb, tail, cw, cb):
    t, c = xb.shape
    groups = t // SUBLANES
    full = jnp.concatenate([tail, xb], axis=0).reshape(groups + 1, SUBLANES, c)
    sub = lax.broadcasted_iota(jnp.int32, (groups, SUBLANES, c), 1)
    acc = cb + cw[CONV_WIDTH - 1:CONV_WIDTH, :] * xb
    for k in range(1, CONV_WIDTH):
        r = pltpu.roll(full, k, 1)
        shifted = jnp.where(sub < k, r[:-1], r[1:]).reshape(t, c)
        acc = acc + cw[CONV_WIDTH - 1 - k:CONV_WIDTH - k, :] * shifted
    return acc


def _lru_scan(a, b, h0):
    t, c = a.shape
    row = lax.broadcasted_iota(jnp.int32, (SUBLANES, c), 0)
    out = []
    for j in range(t // SUBLANES):
        aj = a[j * SUBLANES:(j + 1) * SUBLANES, :]
        bj = b[j * SUBLANES:(j + 1) * SUBLANES, :]
        for k in (1, 2, 4):
            valid = row >= k
            a_sh = pltpu.roll(aj, k, 0)
            b_sh = pltpu.roll(bj, k, 0)
            bj = bj + aj * jnp.where(valid, b_sh, 0.0)
            aj = aj * jnp.where(valid, a_sh, 1.0)
        hj = bj + aj * h0
        out.append(hj)
        h0 = jnp.broadcast_to(hj[SUBLANES - 1:SUBLANES, :], (SUBLANES, c))
    return jnp.concatenate(out, axis=0), h0


def _kv_kernel(mem_ref, g_ref, w_ref, o_ref):
    mn = _rms_norm(mem_ref[...], g_ref[...]).astype(BF16)
    for j in range(2 * N_HEADS):
        o_ref[j] = _dot(mn, w_ref[j]).astype(BF16)


def _layer_kernel(x_ref, ng_ref, win_ref, bin_ref,
                  lcw_ref, lcb_ref, lwa_ref, lba_ref, lwx_ref, lbx_ref, lam_ref,
                  mcw_ref, mcb_ref, mwq_ref, mwk_ref, mwv_ref, mwo_ref, mbo_ref, wif_ref, bif_ref,
                  mng_ref, msk_ref, kv_ref, wbr_ref, wout_ref, fg_ref,
                  o_ref,
                  hb_s, ltail_s, mtail_s, hcar_s, q_s, k_s, v_s, og_s, xc_s, g_s, st_s, m_s, y_s, acc_s, mg_s,
                  *, nb, tr):
    L = tr
    R = nb * tr
    rows = [slice(b * tr, (b + 1) * tr) for b in range(nb)]

    @pl.when(pl.program_id(0) == 0)
    def _():
        ltail_s[...] = jnp.zeros(ltail_s.shape, F32)
        mtail_s[...] = jnp.zeros(mtail_s.shape, F32)
        hcar_s[...] = jnp.zeros(hcar_s.shape, F32)
        st_s[...] = jnp.zeros(st_s.shape, F32)
        m_s[...] = jnp.zeros(m_s.shape, F32)

    hb_s[...] = _rms_norm(x_ref[...].reshape(R, D_MODEL), ng_ref[...]).astype(BF16)

    def proj(j):
        return _dot(hb_s[...], win_ref[j]) + bin_ref[:, _blk(j)]

    def conv(x, tail_ref, cw_ref, cb_ref, sl):
        parts = []
        for b in range(nb):
            xb = x[rows[b], :]
            parts.append(_causal_conv(xb, tail_ref[b, :, sl], cw_ref[:, sl], cb_ref[:, sl]))
            tail_ref[b, :, sl] = xb[tr - SUBLANES:, :]
        return jnp.concatenate(parts, axis=0)


    def lru_block(g):
        sl = _blk(g)
        u = conv(proj(LRU_X + g), ltail_s, lcw_ref, lcb_ref, sl)
        ub = u.astype(BF16)
        r = _sigmoid(_dot(ub, lwa_ref[g]) + lba_ref[:, sl])
        ig = _sigmoid(_dot(ub, lwx_ref[g]) + lbx_ref[:, sl])
        log_a = LRU_C * r * _log_sigmoid(lam_ref[:, sl])
        a = jnp.exp(log_a)
        bb = jnp.sqrt(-jnp.tanh(log_a) * (1.0 + a * a)) * ig * u
        parts = []
        for b in range(nb):
            hpart, hlast = _lru_scan(a[rows[b], :], bb[rows[b], :], hcar_s[b, :, sl])
            hcar_s[b, :, sl] = hlast
            parts.append(hpart)
        h = jnp.concatenate(parts, axis=0)
        y_s[0, :, sl] = (h * _silu(proj(LRU_Z + g))).astype(BF16)

    def xattn_head(h):
        sl = _blk(h)
        q = proj(XA_Q + h).astype(BF16)
        oparts = []
        for b in range(nb):
            s = _dot_nt(q[rows[b], :], kv_ref[b, h]) * (HEAD_DIM ** -0.5)
            e = jnp.exp(s - jnp.max(s, axis=-1, keepdims=True))
            p = e * (1.0 / jnp.sum(e, axis=-1, keepdims=True))
            oparts.append(_dot(p.astype(BF16), kv_ref[b, N_HEADS + h]))
        o = jnp.concatenate(oparts, axis=0)
        y_s[2, :, sl] = (o * _silu(proj(XA_Z + h))).astype(BF16)

    def mlstm_front(h):
        sl = _blk(h)
        xm = proj(M_X + h)
        xc = _silu(conv(xm, mtail_s, mcw_ref, mcb_ref, sl))
        xc_s[:, sl] = xc
        xcb = xc.astype(BF16)
        xmb = xm.astype(BF16)
        q_s[:, sl] = _dot(xcb, mwq_ref[h]).astype(BF16)
        k_s[:, sl] = _dot(xcb, mwk_ref[h]).astype(BF16)
        v_s[:, sl] = _dot(xmb, mwv_ref[h]).astype(BF16)
        og_s[:, sl] = _sigmoid(_dot(xmb, mwo_ref[h]) + mbo_ref[:, sl])

    def mlstm_gates():
        g_s[...] = (_dot(q_s[...], wif_ref[0:D_MODEL, :])
                    + _dot(k_s[...], wif_ref[D_MODEL:2 * D_MODEL, :])
                    + _dot(v_s[...], wif_ref[2 * D_MODEL:3 * D_MODEL, :])
                    + bif_ref[...])
        terms = []
        for b in range(nb):
            bcum = _prefix_rows(_log_sigmoid(g_s[rows[b], LANES:2 * LANES]), jnp.add, 0.0)
            gq = g_s[rows[b], 0:LANES] - bcum
            cmax = _prefix_rows(gq, jnp.maximum, -jnp.inf)
            terms.append((bcum, gq, cmax, gq.T))
        return terms

    causal = (lax.broadcasted_iota(jnp.int32, (L, L), 0) >= lax.broadcasted_iota(jnp.int32, (L, L), 1))

    def mlstm_chunk(b, h, terms):
        sl = _blk(h)
        bcum, gq, cmax, gq_t = terms
        b_rep = _lane_bcast(bcum, h)
        gq_rep = _lane_bcast(gq, h)
        gq_row = gq_t[h:h + 1, :]
        m_prev = m_s[b, h, 0:1, :]
        mm = jnp.maximum(m_prev, _lane_bcast(cmax, h))
        mm_last = mm[L - 1:L, :]
        p = jnp.exp(jnp.where(causal, gq_row - _lane_tile(mm, L), -jnp.inf))
        sc = jnp.exp(m_prev - mm)
        qh = q_s[rows[b], sl]
        kh = (k_s[rows[b], sl] * (HEAD_DIM ** -0.5)).astype(BF16)
        vh = v_s[rows[b], sl]
        kt = kh.T
        st = st_s[b, h]
        stb = st.astype(BF16)

        qk = _dot(qh, jnp.concatenate([kt, stb[:, HEAD_DIM:]], axis=1))
        s = qk[:, :L] * p
        num = _dot(s.astype(BF16), vh) + _lane_tile(sc, HEAD_DIM) * _dot(qh, stb[:, :HEAD_DIM])
        den = jnp.sum(s, axis=-1, keepdims=True) + sc * qk[:, L:]
        inv = 1.0 / jnp.maximum(jnp.abs(den), jnp.exp(-(b_rep + mm)))

        wk = jnp.exp(gq_rep - mm_last)
        wkv = jnp.concatenate([_lane_tile(wk, HEAD_DIM) * vh.astype(F32), wk], axis=1).astype(BF16)
        decay = sc[L - 1:L, :]
        st_s[b, h] = _lane_tile(decay, HEAD_DIM + LANES) * st + _dot(kt, wkv)
        m_s[b, h] = jnp.broadcast_to(b_rep[L - 1:L, :] + mm_last, (SUBLANES, LANES))
        return num * _lane_tile(inv, HEAD_DIM)

    def mlstm_back(h, hparts):
        sl = _blk(h)
        hh = og_s[:, sl] * jnp.concatenate(hparts, axis=0)
        mu = jnp.mean(hh, axis=-1, keepdims=True)
        dl = hh - mu
        var = jnp.mean(dl * dl, axis=-1, keepdims=True)
        hn = dl * lax.rsqrt(var + EPS) * mng_ref[:, sl]
        y = (hn + msk_ref[:, sl] * xc_s[:, sl]) * _silu(proj(M_Z + h))
        y_s[1, :, sl] = y.astype(BF16)

    def branch_out(n, j):
        return _sigmoid(proj(GATE + N_HEADS * n + j)) * _dot(y_s[n], wbr_ref[n, j])

    for g in range(N_HEADS):
        lru_block(g)
        xattn_head(g)
    for h in range(N_HEADS):
        mlstm_front(h)
    gate_terms = mlstm_gates()
    for h in range(N_HEADS):
        hparts = [mlstm_chunk(b, h, gate_terms[b]) for b in range(nb)]
        mlstm_back(h, hparts)
        acc_s[:, _blk(h)] = branch_out(0, h) + branch_out(2, h)
    for j in range(N_HEADS):
        mg_s[:, _blk(j)] = (acc_s[:, _blk(j)] + branch_out(1, j)).astype(BF16)
    ssq = None
    for j in range(N_HEADS):
        sl = _blk(j)
        xo = x_ref[:, :, sl].reshape(R, HEAD_DIM) + _dot(mg_s[...], wout_ref[j])
        o_ref[:, :, sl] = xo.reshape(nb, tr, HEAD_DIM)
        part = jnp.sum(xo * xo, axis=-1, keepdims=True)
        ssq = part if ssq is None else ssq + part
    scale = lax.rsqrt(ssq * (1.0 / D_MODEL) + EPS)
    o_ref[...] = (o_ref[...].reshape(R, D_MODEL) * scale * fg_ref[...]).reshape(nb, tr, D_MODEL)


def _resident(a):
    nd = a.ndim
    return pl.BlockSpec(a.shape, lambda i: (0,) * nd, pipeline_mode=pl.Buffered(1))


def _col_blocks(w):
    k, n = w.shape
    return w.reshape(k, n // MXU_COLS, MXU_COLS).transpose(1, 0, 2).astype(BF16)


def kernel(x, mem, norm_g, w_in, b_in, lru_conv_w, lru_conv_b, lru_w_a, lru_b_a, lru_w_x, lru_b_x,
           lru_lambda, m_conv_w, m_conv_b, m_w_q, m_w_k, m_w_v, m_w_o, m_b_o, m_w_if, m_b_if,
           m_norm_g, m_skip, mem_norm_g, w_mem_kv, w_branch, w_out, final_norm_g):
    B, S, D = x.shape
    assert D == D_MODEL and S % TIME_TILE == 0 and norm_g.shape[0] == 1
    tr = TIME_TILE
    H = N_HEADS
    row = lambda v: v.reshape(1, -1).astype(F32)

    w_kv = _col_blocks(w_mem_kv[0])
    kv = pl.pallas_call(
        _kv_kernel,
        grid=(B,),
        in_specs=[pl.BlockSpec((None, N_MEM, D), lambda b: (b, 0, 0)),
                  pl.BlockSpec((1, D), lambda b: (0, 0)),
                  pl.BlockSpec(w_kv.shape, lambda b: (0, 0, 0))],
        out_specs=pl.BlockSpec((None, 2 * H, N_MEM, HEAD_DIM), lambda b: (b, 0, 0, 0)),
        out_shape=jax.ShapeDtypeStruct((B, 2 * H, N_MEM, HEAD_DIM), BF16),
        compiler_params=pltpu.CompilerParams(vmem_limit_bytes=VMEM_LIMIT_BYTES),
        name="mem_kv",
    )(mem, row(mem_norm_g[0]), w_kv)

    pad_lanes = lambda a: jnp.pad(a, ((0, 0), (0, LANES - H)))
    w_if = jnp.concatenate([pad_lanes(m_w_if[0][:, :H]), pad_lanes(m_w_if[0][:, H:])], axis=1).astype(BF16)
    b_if2 = m_b_if[0].reshape(1, 2 * H).astype(F32)
    b_if = jnp.concatenate([pad_lanes(b_if2[:, :H]), pad_lanes(b_if2[:, H:])], axis=1)
    w_br = jnp.stack([_col_blocks(w_branch[0, n]) for n in range(N_BRANCH)])

    weights = [row(norm_g[0]), _col_blocks(w_in[0]), row(b_in[0]),
               lru_conv_w[0], row(lru_conv_b[0]), lru_w_a[0].astype(BF16), row(lru_b_a[0]),
               lru_w_x[0].astype(BF16), row(lru_b_x[0]), row(lru_lambda[0]),
               m_conv_w[0], row(m_conv_b[0]), m_w_q[0].astype(BF16), m_w_k[0].astype(BF16),
               m_w_v[0].astype(BF16), m_w_o[0].astype(BF16), row(m_b_o[0]), w_if, b_if,
               row(m_norm_g[0]), row(m_skip[0]), kv, w_br, _col_blocks(w_out[0]), row(final_norm_g)]

    R = B * tr
    tok = pl.BlockSpec((B, tr, D), lambda i: (0, i, 0))
    scratch = [pltpu.VMEM((R, D), BF16),
               pltpu.VMEM((B, SUBLANES, D), F32),
               pltpu.VMEM((B, SUBLANES, D), F32),
               pltpu.VMEM((B, SUBLANES, D), F32),
               pltpu.VMEM((R, D), BF16),
               pltpu.VMEM((R, D), BF16),
               pltpu.VMEM((R, D), BF16),
               pltpu.VMEM((R, D), F32),
               pltpu.VMEM((R, D), F32),
               pltpu.VMEM((R, 2 * LANES), F32),
               pltpu.VMEM((B, H, HEAD_DIM, HEAD_DIM + LANES), F32),
               pltpu.VMEM((B, H, SUBLANES, LANES), F32),
               pltpu.VMEM((N_BRANCH, R, D), BF16),
               pltpu.VMEM((R, D), F32),
               pltpu.VMEM((R, D), BF16)]
    return pl.pallas_call(
        functools.partial(_layer_kernel, nb=B, tr=tr),
        grid=(S // tr,),
        in_specs=[tok] + [_resident(w) for w in weights],
        out_specs=tok,
        out_shape=jax.ShapeDtypeStruct((B, S, D), F32),
        scratch_shapes=scratch,
        compiler_params=pltpu.CompilerParams(
            dimension_semantics=("arbitrary",),
            vmem_limit_bytes=VMEM_LIMIT_BYTES),
        name="hybrid_layer",
    )(x, *weights)
```

```python
import functools

import jax
import jax.numpy as jnp
from jax import lax
from jax.experimental import pallas as pl
from jax.experimental.pallas import tpu as pltpu

D_MODEL = 1024
N_HEADS = 4
HEAD_DIM = D_MODEL // N_HEADS
N_MEM = 256
N_BRANCH = 3
EPS = 1e-6
CONV_WIDTH = 4
LRU_C = 8.0
LANES = 128
SUBLANES = 8
MXU_COLS = 256
TIME_TILE = 128
ROW_CHUNK = 32
VMEM_LIMIT_BYTES = 60 * 1024 * 1024

LRU_X, LRU_Z, M_X, M_Z, XA_Q, XA_Z, GATE = 0, 4, 8, 12, 16, 20, 24

BF16 = jnp.bfloat16
F32 = jnp.float32


def _dot(a, b):
    return jnp.dot(a, b, preferred_element_type=F32)


def _dot_nt(a, b):
    return lax.dot_general(a, b, (((1,), (1,)), ((), ())), preferred_element_type=F32)


def _tanh1(v):
    return jnp.tanh(v) + 1.0


def _log_sigmoid(x):
    return jnp.minimum(x, 0.0) - jnp.log1p(jnp.exp(-jnp.abs(x)))


def _rms_norm(x, g):
    ms = jnp.mean(x * x, axis=-1, keepdims=True)
    return x * lax.rsqrt(ms + EPS) * g


def _blk(j):
    return slice(j * HEAD_DIM, (j + 1) * HEAD_DIM)


def _prefix_rows(x, op, identity):
    n = x.shape[0]
    row = lax.broadcasted_iota(jnp.int32, x.shape, 0)
    k = 1
    while k < n:
        x = op(x, jnp.where(row >= k, pltpu.roll(x, k, 0), identity))
        k *= 2
    return x


def _lane_tile(x, n):
    reps = n // LANES
    return x if reps == 1 else jnp.concatenate([x] * reps, axis=-1)


def _lane_bcast(x, lane):
    return jnp.broadcast_to(x[:, lane:lane + 1], x.shape)


def _causal_conv(xb, tail, cw, cb):
    t, c = xb.shape
    groups = t // SUBLANES
    full = jnp.concatenate([tail, xb], axis=0).reshape(groups + 1, SUBLANES, c)
    sub = lax.broadcasted_iota(jnp.int32, (groups, SUBLANES, c), 1)
    acc = cb + cw[CONV_WIDTH - 1:CONV_WIDTH, :] * xb
    for k in range(1, CONV_WIDTH):
        r = pltpu.roll(full, k, 1)
        shifted = jnp.where(sub < k, r[:-1], r[1:]).reshape(t, c)
        acc = acc + cw[CONV_WIDTH - 1 - k:CONV_WIDTH - k, :] * shifted
    return acc


def _lru_scan(a, b, h0):
    t, c = a.shape
    row = lax.broadcasted_iota(jnp.int32, (SUBLANES, c), 0)
    out = []
    for j in range(t // SUBLANES):
        aj = a[j * SUBLANES:(j + 1) * SUBLANES, :]
        bj = b[j * SUBLANES:(j + 1) * SUBLANES, :]
        for k in (1, 2, 4):
            valid = row >= k
            a_sh = pltpu.roll(aj, k, 0)
            b_sh = pltpu.roll(bj, k, 0)
            bj = bj + aj * jnp.where(valid, b_sh, 0.0)
            aj = aj * jnp.where(valid, a_sh, 1.0)
        hj = bj + aj * h0
        out.append(hj)
        h0 = jnp.broadcast_to(hj[SUBLANES - 1:SUBLANES, :], (SUBLANES, c))
    return jnp.concatenate(out, axis=0), h0


def _zip_stages(gens):
    gens = list(gens)
    while gens:
        alive = []
        for g in gens:
            try:
                next(g)
                alive.append(g)
            except StopIteration:
                pass
        gens = alive
        if gens:
            yield


def _interleave(gens, skew=1):
    pending = list(gens)
    active = []
    rnd = 0
    while pending or active:
        while pending and (len(gens) - len(pending)) * skew <= rnd:
            active.append(pending.pop(0))
        alive = []
        for g in active:
            try:
                next(g)
                alive.append(g)
            except StopIteration:
                pass
        active = alive
        rnd += 1


def _kv_kernel(mem_ref, g_ref, w_ref, o_ref):
    mn = _rms_norm(mem_ref[...], g_ref[...]).astype(BF16)
    for j in range(2 * N_HEADS):
        o_ref[j] = _dot(mn, w_ref[j]).astype(BF16)


def _layer_kernel(x_ref, ng_ref, win_ref, bin_ref,
                  lcw_ref, lcb_ref, lwa_ref, lba_ref, lwx_ref, lbx_ref, lam_ref,
                  mcw_ref, mcb_ref, mwq_ref, mwk_ref, mwv_ref, mwo_ref, mbo_ref, wif_ref, bif_ref,
                  mng_ref, msk_ref, kv_ref, wbr_ref, wout_ref, fg_ref,
                  o_ref,
                  hb_s, ltail_s, mtail_s, hcar_s, q_s, k_s, v_s, og_s, xc_s, g_s, st_s, m_s, y_s, acc_s, mg_s,
                  *, nb, tr):
    L = tr
    R = nb * tr
    rows = [slice(b * tr, (b + 1) * tr) for b in range(nb)]

    @pl.when(pl.program_id(0) == 0)
    def _():
        ltail_s[...] = jnp.zeros(ltail_s.shape, F32)
        mtail_s[...] = jnp.zeros(mtail_s.shape, F32)
        hcar_s[...] = jnp.zeros(hcar_s.shape, F32)
        st_s[...] = jnp.zeros(st_s.shape, F32)
        m_s[...] = jnp.zeros(m_s.shape, F32)

    hb_s[...] = _rms_norm(x_ref[...].reshape(R, D_MODEL), ng_ref[...]).astype(BF16)

    def proj(j):
        return _dot(hb_s[...], win_ref[j]) + bin_ref[:, _blk(j)]

    def conv(x, tail_ref, cw_ref, cb_ref, sl):
        parts = []
        for b in range(nb):
            xb = x[rows[b], :]
            parts.append(_causal_conv(xb, tail_ref[b, :, sl], cw_ref[:, sl], cb_ref[:, sl]))
            tail_ref[b, :, sl] = xb[tr - SUBLANES:, :]
        return jnp.concatenate(parts, axis=0)


    def lru_block(g):
        sl = _blk(g)
        px = proj(LRU_X + g)
        yield
        u = conv(px, ltail_s, lcw_ref, lcb_ref, sl)
        ub = u.astype(BF16)
        yield
        r_pre = _dot(ub, lwa_ref[g]) + lba_ref[:, sl]
        i_pre = _dot(ub, lwx_ref[g]) + lbx_ref[:, sl]
        yield
        zh = proj(LRU_Z + g)
        ls_c = (0.5 * LRU_C) * _log_sigmoid(lam_ref[:, sl])
        yield
        for b in range(nb):
            h0 = hcar_s[b, :, sl]
            for c in range(tr // ROW_CHUNK):
                rs = slice(b * tr + c * ROW_CHUNK, b * tr + (c + 1) * ROW_CHUNK)
                log_a = ls_c * _tanh1(r_pre[rs, :])
                a = jnp.exp(log_a)
                om = -jnp.tanh(log_a) * (1.0 + a * a)
                mult = jnp.where(om > 0.0, om * lax.rsqrt(om), 0.0)
                bb = mult * _tanh1(i_pre[rs, :]) * u[rs, :]
                h, h0 = _lru_scan(a, bb, h0)
                y_s[0, rs, sl] = (h * (zh[rs, :] * _tanh1(zh[rs, :]))).astype(BF16)
                yield
            hcar_s[b, :, sl] = h0

    def xattn_head(h):
        sl = _blk(h)
        q = proj(XA_Q + h).astype(BF16)
        yield
        scores = [_dot_nt(q[rows[b], :], kv_ref[b, h]) * (HEAD_DIM ** -0.5) for b in range(nb)]
        yield
        probs = []
        for s in scores:
            e = jnp.exp(s - jnp.max(s, axis=-1, keepdims=True))
            probs.append((e * (1.0 / jnp.sum(e, axis=-1, keepdims=True))).astype(BF16))
            yield
        o = jnp.concatenate([_dot(probs[b], kv_ref[b, N_HEADS + h]) for b in range(nb)], axis=0)
        yield
        zh = proj(XA_Z + h)
        yield
        y_s[2, :, sl] = (o * (zh * _tanh1(zh))).astype(BF16)

    def mlstm_front(h):
        sl = _blk(h)
        xm = proj(M_X + h)
        yield
        ch = conv(xm, mtail_s, mcw_ref, mcb_ref, sl)
        xc = ch * _tanh1(ch)
        xc_s[:, sl] = xc
        xcb = xc.astype(BF16)
        xmb = xm.astype(BF16)
        yield
        q_s[:, sl] = _dot(xcb, mwq_ref[h]).astype(BF16)
        k_s[:, sl] = _dot(xcb, mwk_ref[h]).astype(BF16)
        yield
        v_s[:, sl] = _dot(xmb, mwv_ref[h]).astype(BF16)
        og_s[:, sl] = _tanh1(_dot(xmb, mwo_ref[h]) + mbo_ref[:, sl])

    def mlstm_gates():
        g_s[...] = (_dot(q_s[...], wif_ref[0:D_MODEL, :])
                    + _dot(k_s[...], wif_ref[D_MODEL:2 * D_MODEL, :])
                    + _dot(v_s[...], wif_ref[2 * D_MODEL:3 * D_MODEL, :])
                    + bif_ref[...])
        terms = []
        for b in range(nb):
            bcum = _prefix_rows(_log_sigmoid(g_s[rows[b], LANES:2 * LANES]), jnp.add, 0.0)
            gq = g_s[rows[b], 0:LANES] - bcum
            cmax = _prefix_rows(gq, jnp.maximum, -jnp.inf)
            terms.append((bcum, gq, cmax, gq.T))
        return terms

    causal = (lax.broadcasted_iota(jnp.int32, (L, L), 0) >= lax.broadcasted_iota(jnp.int32, (L, L), 1))

    def mlstm_chunk(b, h, terms, out):
        sl = _blk(h)
        bcum, gq, cmax, gq_t = terms
        b_rep = _lane_bcast(bcum, h)
        gq_rep = _lane_bcast(gq, h)
        gq_row = gq_t[h:h + 1, :]
        m_prev = m_s[b, h, 0:1, :]
        mm = jnp.maximum(m_prev, _lane_bcast(cmax, h))
        mm_last = mm[L - 1:L, :]
        p = jnp.exp(jnp.where(causal, gq_row - _lane_tile(mm, L), -jnp.inf))
        sc = jnp.exp(m_prev - mm)
        qh = q_s[rows[b], sl]
        kh = (k_s[rows[b], sl] * (HEAD_DIM ** -0.5)).astype(BF16)
        vh = v_s[rows[b], sl]
        kt = kh.T
        st = st_s[b, h]
        stb = st.astype(BF16)
        yield
        qk = _dot(qh, jnp.concatenate([kt, stb[:, HEAD_DIM:]], axis=1))
        qc = _dot(qh, stb[:, :HEAD_DIM])
        yield
        s = qk[:, :L] * p
        den = jnp.sum(s, axis=-1, keepdims=True) + sc * qk[:, L:]
        inv = 0.5 / jnp.maximum(jnp.abs(den), jnp.exp(-(b_rep + mm)))
        wk = jnp.exp(gq_rep - mm_last)
        wkv = jnp.concatenate([_lane_tile(wk, HEAD_DIM) * vh.astype(F32), wk], axis=1).astype(BF16)
        yield
        num = _dot(s.astype(BF16), vh) + _lane_tile(sc, HEAD_DIM) * qc
        upd = _dot(kt, wkv)
        yield
        decay = sc[L - 1:L, :]
        st_s[b, h] = _lane_tile(decay, HEAD_DIM + LANES) * st + upd
        m_s[b, h] = jnp.broadcast_to(b_rep[L - 1:L, :] + mm_last, (SUBLANES, LANES))
        out[b] = num * _lane_tile(inv, HEAD_DIM)

    def mlstm_head(h, terms):
        sl = _blk(h)
        hparts = [None] * nb
        yield from _zip_stages([mlstm_chunk(b, h, terms[b], hparts) for b in range(nb)])
        hh = og_s[:, sl] * jnp.concatenate(hparts, axis=0)
        mu = jnp.mean(hh, axis=-1, keepdims=True)
        dl = hh - mu
        var = jnp.mean(dl * dl, axis=-1, keepdims=True)
        hn = dl * lax.rsqrt(var + EPS) * mng_ref[:, sl]
        yield
        zh = proj(M_Z + h)
        yield
        y = (hn + msk_ref[:, sl] * xc_s[:, sl]) * (zh * _tanh1(zh))
        y_s[1, :, sl] = y.astype(BF16)

    def branch_out(n, j):
        gate2 = _tanh1(proj(GATE + N_HEADS * n + j))
        yield
        return gate2 * _dot(y_s[n], wbr_ref[n, j])

    def branch_pair(j):
        first = yield from branch_out(0, j)
        yield
        second = yield from branch_out(2, j)
        acc_s[:, _blk(j)] = first + second

    def branch_last(j):
        last = yield from branch_out(1, j)
        mg_s[:, _blk(j)] = (acc_s[:, _blk(j)] + last).astype(BF16)

    heads = range(N_HEADS)
    _interleave([f(i) for i in heads for f in (lru_block, xattn_head, mlstm_front)])
    gate_terms = mlstm_gates()
    _interleave([f(i) for i in heads for f in (lambda i: mlstm_head(i, gate_terms), branch_pair)])
    _interleave([branch_last(j) for j in heads])
    ssq = None
    for j in range(N_HEADS):
        sl = _blk(j)
        xo = x_ref[:, :, sl].reshape(R, HEAD_DIM) + _dot(mg_s[...], wout_ref[j])
        o_ref[:, :, sl] = xo.reshape(nb, tr, HEAD_DIM)
        part = jnp.sum(xo * xo, axis=-1, keepdims=True)
        ssq = part if ssq is None else ssq + part
    scale = lax.rsqrt(ssq * (1.0 / D_MODEL) + EPS)
    o_ref[...] = (o_ref[...].reshape(R, D_MODEL) * scale * fg_ref[...]).reshape(nb, tr, D_MODEL)


def _resident(a):
    nd = a.ndim
    return pl.BlockSpec(a.shape, lambda i: (0,) * nd, pipeline_mode=pl.Buffered(1))


def _col_blocks(w):
    k, n = w.shape
    return w.reshape(k, n // MXU_COLS, MXU_COLS).transpose(1, 0, 2).astype(BF16)


def kernel(x, mem, norm_g, w_in, b_in, lru_conv_w, lru_conv_b, lru_w_a, lru_b_a, lru_w_x, lru_b_x,
           lru_lambda, m_conv_w, m_conv_b, m_w_q, m_w_k, m_w_v, m_w_o, m_b_o, m_w_if, m_b_if,
           m_norm_g, m_skip, mem_norm_g, w_mem_kv, w_branch, w_out, final_norm_g):
    B, S, D = x.shape
    assert D == D_MODEL and S % TIME_TILE == 0 and norm_g.shape[0] == 1
    tr = TIME_TILE
    H = N_HEADS
    row = lambda v: v.reshape(1, -1).astype(F32)

    w_kv = _col_blocks(w_mem_kv[0])
    kv = pl.pallas_call(
        _kv_kernel,
        grid=(B,),
        in_specs=[pl.BlockSpec((None, N_MEM, D), lambda b: (b, 0, 0)),
                  pl.BlockSpec((1, D), lambda b: (0, 0)),
                  pl.BlockSpec(w_kv.shape, lambda b: (0, 0, 0))],
        out_specs=pl.BlockSpec((None, 2 * H, N_MEM, HEAD_DIM), lambda b: (b, 0, 0, 0)),
        out_shape=jax.ShapeDtypeStruct((B, 2 * H, N_MEM, HEAD_DIM), BF16),
        compiler_params=pltpu.CompilerParams(vmem_limit_bytes=VMEM_LIMIT_BYTES),
        name="mem_kv",
    )(mem, row(mem_norm_g[0]), w_kv)

    pad_lanes = lambda a: jnp.pad(a, ((0, 0), (0, LANES - H)))
    w_if = jnp.concatenate([pad_lanes(m_w_if[0][:, :H]), pad_lanes(m_w_if[0][:, H:])], axis=1).astype(BF16)
    b_if2 = m_b_if[0].reshape(1, 2 * H).astype(F32)
    b_if = jnp.concatenate([pad_lanes(b_if2[:, :H]), pad_lanes(b_if2[:, H:])], axis=1)
    w_br = jnp.stack([_col_blocks(0.5 * w_branch[0, n]) for n in range(N_BRANCH)])

    in_scale = jnp.repeat(jnp.array([1.0, 0.5, 1.0, 0.5, 1.0, 0.5, 0.5, 0.5, 0.5], F32), D)
    weights = [row(norm_g[0]), _col_blocks(w_in[0] * in_scale), row(b_in[0] * in_scale),
               0.5 * lru_conv_w[0], row(0.5 * lru_conv_b[0]), lru_w_a[0].astype(BF16), row(0.5 * lru_b_a[0]),
               lru_w_x[0].astype(BF16), row(0.5 * lru_b_x[0]), row(lru_lambda[0]),
               0.5 * m_conv_w[0], row(0.5 * m_conv_b[0]), m_w_q[0].astype(BF16), m_w_k[0].astype(BF16),
               m_w_v[0].astype(BF16), (0.5 * m_w_o[0]).astype(BF16), row(0.5 * m_b_o[0]), w_if, b_if,
               row(m_norm_g[0]), row(m_skip[0]), kv, w_br, _col_blocks(w_out[0]), row(final_norm_g)]

    R = B * tr
    tok = pl.BlockSpec((B, tr, D), lambda i: (0, i, 0))
    scratch = [pltpu.VMEM((R, D), BF16),
               pltpu.VMEM((B, SUBLANES, D), F32),
               pltpu.VMEM((B, SUBLANES, D), F32),
               pltpu.VMEM((B, SUBLANES, D), F32),
               pltpu.VMEM((R, D), BF16),
               pltpu.VMEM((R, D), BF16),
               pltpu.VMEM((R, D), BF16),
               pltpu.VMEM((R, D), F32),
               pltpu.VMEM((R, D), F32),
               pltpu.VMEM((R, 2 * LANES), F32),
               pltpu.VMEM((B, H, HEAD_DIM, HEAD_DIM + LANES), F32),
               pltpu.VMEM((B, H, SUBLANES, LANES), F32),
               pltpu.VMEM((N_BRANCH, R, D), BF16),
               pltpu.VMEM((R, D), F32),
               pltpu.VMEM((R, D), BF16)]
    return pl.pallas_call(
        functools.partial(_layer_kernel, nb=B, tr=tr),
        grid=(S // tr,),
        in_specs=[tok] + [_resident(w) for w in weights],
        out_specs=tok,
        out_shape=jax.ShapeDtypeStruct((B, S, D), F32),
        scratch_shapes=scratch,
        compiler_params=pltpu.CompilerParams(
            dimension_semantics=("arbitrary",),
            vmem_limit_bytes=VMEM_LIMIT_BYTES),
        name="hybrid_layer",
    )(x, *weights)
```

```python
import functools

import jax
import jax.numpy as jnp
from jax import lax
from jax.experimental import pallas as pl
from jax.experimental.pallas import tpu as pltpu

D_MODEL = 1024
N_HEADS = 4
HEAD_DIM = D_MODEL // N_HEADS
N_MEM = 256
N_BRANCH = 3
EPS = 1e-6
CONV_WIDTH = 4
LRU_C = 8.0
LANES = 128
SUBLANES = 8
MXU_COLS = 256
TIME_TILE = 128
ROW_CHUNK = 32
VMEM_LIMIT_BYTES = 60 * 1024 * 1024

LRU_X, LRU_Z, M_X, M_Z, XA_Q, XA_Z, GATE = 0, 4, 8, 12, 16, 20, 24
IN_GROUPS = 9
(V_NORM_G, V_LRU_CONV_B, V_LRU_B_A, V_LRU_B_X, V_LRU_LAMBDA, V_M_CONV_B, V_M_B_O, V_M_NORM_G, V_M_SKIP,
 V_FINAL_G, V_B_IF) = range(11)
V_LRU_CONV_W = 11
V_M_CONV_W = V_LRU_CONV_W + CONV_WIDTH
V_B_IN = V_M_CONV_W + CONV_WIDTH
N_VEC_ROWS = V_B_IN + IN_GROUPS
HW_LRU_A, HW_LRU_X, HW_M_Q, HW_M_K, HW_M_V, HW_M_O = range(6)

BF16 = jnp.bfloat16
F32 = jnp.float32


def _dot(a, b):
    return jnp.dot(a, b, preferred_element_type=F32)


def _dot_nt(a, b):
    return lax.dot_general(a, b, (((1,), (1,)), ((), ())), preferred_element_type=F32)


def _tanh1(v):
    return jnp.tanh(v) + 1.0


def _log_sigmoid(x):
    return jnp.minimum(x, 0.0) - jnp.log1p(jnp.exp(-jnp.abs(x)))


def _rms_norm(x, g):
    ms = jnp.mean(x * x, axis=-1, keepdims=True)
    return x * lax.rsqrt(ms + EPS) * g


def _blk(j):
    return slice(j * HEAD_DIM, (j + 1) * HEAD_DIM)


def _prefix_rows(x, op, identity):
    n = x.shape[0]
    row = lax.broadcasted_iota(jnp.int32, x.shape, 0)
    k = 1
    while k < n:
        x = op(x, jnp.where(row >= k, pltpu.roll(x, k, 0), identity))
        k *= 2
    return x


def _lane_tile(x, n):
    reps = n // LANES
    return x if reps == 1 else jnp.concatenate([x] * reps, axis=-1)


def _lane_bcast(x, lane):
    return jnp.broadcast_to(x[:, lane:lane + 1], x.shape)


def _causal_conv(xb, tail, cw, cb):
    t, c = xb.shape
    groups = t // SUBLANES
    full = jnp.concatenate([tail, xb], axis=0).reshape(groups + 1, SUBLANES, c)
    sub = lax.broadcasted_iota(jnp.int32, (groups, SUBLANES, c), 1)
    acc = cb + cw[CONV_WIDTH - 1:CONV_WIDTH, :] * xb
    for k in range(1, CONV_WIDTH):
        r = pltpu.roll(full, k, 1)
        shifted = jnp.where(sub < k, r[:-1], r[1:]).reshape(t, c)
        acc = acc + cw[CONV_WIDTH - 1 - k:CONV_WIDTH - k, :] * shifted
    return acc


def _lru_scan(a, b, h0):
    t, c = a.shape
    row = lax.broadcasted_iota(jnp.int32, (SUBLANES, c), 0)
    out = []
    for j in range(t // SUBLANES):
        aj = a[j * SUBLANES:(j + 1) * SUBLANES, :]
        bj = b[j * SUBLANES:(j + 1) * SUBLANES, :]
        for k in (1, 2, 4):
            valid = row >= k
            a_sh = pltpu.roll(aj, k, 0)
            b_sh = pltpu.roll(bj, k, 0)
            bj = bj + aj * jnp.where(valid, b_sh, 0.0)
            aj = aj * jnp.where(valid, a_sh, 1.0)
        hj = bj + aj * h0
        out.append(hj)
        h0 = jnp.broadcast_to(hj[SUBLANES - 1:SUBLANES, :], (SUBLANES, c))
    return jnp.concatenate(out, axis=0), h0


def _zip_stages(gens):
    gens = list(gens)
    while gens:
        alive = []
        for g in gens:
            try:
                next(g)
                alive.append(g)
            except StopIteration:
                pass
        gens = alive
        if gens:
            yield


def _interleave(gens, skew=1):
    pending = list(gens)
    active = []
    rnd = 0
    while pending or active:
        while pending and (len(gens) - len(pending)) * skew <= rnd:
            active.append(pending.pop(0))
        alive = []
        for g in active:
            try:
                next(g)
                alive.append(g)
            except StopIteration:
                pass
        active = alive
        rnd += 1


def _kv_kernel(mem_ref, g_ref, w_ref, o_ref):
    nb = mem_ref.shape[0]
    mn = _rms_norm(mem_ref[...].reshape(nb * N_MEM, D_MODEL), g_ref[...]).astype(BF16)
    w = w_ref[...].astype(BF16)
    for j in range(N_HEADS):
        o_ref[:, j] = _dot(mn, w[:, _blk(j)]).astype(BF16).reshape(nb, N_MEM, HEAD_DIM)


def _col_block_kernel(scale_ref, w_ref, o_ref):
    step = pl.program_id(0) * pl.num_programs(1) + pl.program_id(1)
    w = w_ref[...] * scale_ref[step]
    for j in range(o_ref.shape[0]):
        o_ref[j] = w[:, j * MXU_COLS:(j + 1) * MXU_COLS].astype(BF16)


def _layer_kernel(x_ref, vec_ref, win_ref, hw_ref, wif_ref, kv_ref, wbr_ref, wout_ref,
                  o_ref,
                  hb_s, ltail_s, mtail_s, hcar_s, q_s, k_s, v_s, og_s, xc_s, g_s, st_s, m_s, y_s, acc_s, mg_s,
                  *, nb, tr):
    L = tr
    R = nb * tr
    rows = [slice(b * tr, (b + 1) * tr) for b in range(nb)]

    @pl.when(pl.program_id(0) == 0)
    def _():
        ltail_s[...] = jnp.zeros(ltail_s.shape, F32)
        mtail_s[...] = jnp.zeros(mtail_s.shape, F32)
        hcar_s[...] = jnp.zeros(hcar_s.shape, F32)
        st_s[...] = jnp.zeros(st_s.shape, F32)
        m_s[...] = jnp.zeros(m_s.shape, F32)

    def vec(i, sl=slice(None)):
        return vec_ref[i:i + 1, sl]

    hb_s[...] = _rms_norm(x_ref[...].reshape(R, D_MODEL), vec(V_NORM_G)).astype(BF16)

    def proj(j):
        return _dot(hb_s[...], win_ref[j]) + vec(V_B_IN + j // N_HEADS, _blk(j % N_HEADS))

    def conv(x, tail_ref, cw_row, cb_row, sl):
        cw = vec_ref[cw_row:cw_row + CONV_WIDTH, sl]
        parts = []
        for b in range(nb):
            xb = x[rows[b], :]
            parts.append(_causal_conv(xb, tail_ref[b, :, sl], cw, vec(cb_row, sl)))
            tail_ref[b, :, sl] = xb[tr - SUBLANES:, :]
        return jnp.concatenate(parts, axis=0)


    def lru_block(g):
        sl = _blk(g)
        px = proj(LRU_X + g)
        yield
        u = conv(px, ltail_s, V_LRU_CONV_W, V_LRU_CONV_B, sl)
        ub = u.astype(BF16)
        yield
        r_pre = _dot(ub, hw_ref[HW_LRU_A, g]) + vec(V_LRU_B_A, sl)
        i_pre = _dot(ub, hw_ref[HW_LRU_X, g]) + vec(V_LRU_B_X, sl)
        yield
        zh = proj(LRU_Z + g)
        ls_c = (0.5 * LRU_C) * _log_sigmoid(vec(V_LRU_LAMBDA, sl))
        yield
        for b in range(nb):
            h0 = hcar_s[b, :, sl]
            for c in range(tr // ROW_CHUNK):
                rs = slice(b * tr + c * ROW_CHUNK, b * tr + (c + 1) * ROW_CHUNK)
                log_a = ls_c * _tanh1(r_pre[rs, :])
                a = jnp.exp(log_a)
                om = -jnp.tanh(log_a) * (1.0 + a * a)
                mult = jnp.where(om > 0.0, om * lax.rsqrt(om), 0.0)
                bb = mult * _tanh1(i_pre[rs, :]) * u[rs, :]
                h, h0 = _lru_scan(a, bb, h0)
                y_s[0, rs, sl] = (h * (zh[rs, :] * _tanh1(zh[rs, :]))).astype(BF16)
                yield
            hcar_s[b, :, sl] = h0

    def xattn_head(h):
        sl = _blk(h)
        q = proj(XA_Q + h).astype(BF16)
        yield
        scores = [_dot_nt(q[rows[b], :], kv_ref[b, h]) * (HEAD_DIM ** -0.5) for b in range(nb)]
        yield
        probs = []
        for s in scores:
            e = jnp.exp(s - jnp.max(s, axis=-1, keepdims=True))
            probs.append((e * (1.0 / jnp.sum(e, axis=-1, keepdims=True))).astype(BF16))
            yield
        o = jnp.concatenate([_dot(probs[b], kv_ref[b, N_HEADS + h]) for b in range(nb)], axis=0)
        yield
        zh = proj(XA_Z + h)
        yield
        y_s[2, :, sl] = (o * (zh * _tanh1(zh))).astype(BF16)

    def mlstm_front(h):
        sl = _blk(h)
        xm = proj(M_X + h)
        yield
        ch = conv(xm, mtail_s, V_M_CONV_W, V_M_CONV_B, sl)
        xc = ch * _tanh1(ch)
        xc_s[:, sl] = xc
        xcb = xc.astype(BF16)
        xmb = xm.astype(BF16)
        yield
        q_s[:, sl] = _dot(xcb, hw_ref[HW_M_Q, h]).astype(BF16)
        k_s[:, sl] = _dot(xcb, hw_ref[HW_M_K, h]).astype(BF16)
        yield
        v_s[:, sl] = _dot(xmb, hw_ref[HW_M_V, h]).astype(BF16)
        og_s[:, sl] = _tanh1(_dot(xmb, hw_ref[HW_M_O, h]) + vec(V_M_B_O, sl))

    def mlstm_gates():
        g_s[...] = (_dot(q_s[...], wif_ref[0:D_MODEL, :])
                    + _dot(k_s[...], wif_ref[D_MODEL:2 * D_MODEL, :])
                    + _dot(v_s[...], wif_ref[2 * D_MODEL:3 * D_MODEL, :])
                    + vec(V_B_IF, slice(0, 2 * LANES)))
        terms = []
        for b in range(nb):
            bcum = _prefix_rows(_log_sigmoid(g_s[rows[b], LANES:2 * LANES]), jnp.add, 0.0)
            gq = g_s[rows[b], 0:LANES] - bcum
            cmax = _prefix_rows(gq, jnp.maximum, -jnp.inf)
            terms.append((bcum, gq, cmax, gq.T))
        return terms

    causal = (lax.broadcasted_iota(jnp.int32, (L, L), 0) >= lax.broadcasted_iota(jnp.int32, (L, L), 1))

    def mlstm_chunk(b, h, terms, out):
        sl = _blk(h)
        bcum, gq, cmax, gq_t = terms
        b_rep = _lane_bcast(bcum, h)
        gq_rep = _lane_bcast(gq, h)
        gq_row = gq_t[h:h + 1, :]
        m_prev = m_s[b, h, 0:1, :]
        mm = jnp.maximum(m_prev, _lane_bcast(cmax, h))
        mm_last = mm[L - 1:L, :]
        p = jnp.exp(jnp.where(causal, gq_row - _lane_tile(mm, L), -jnp.inf))
        sc = jnp.exp(m_prev - mm)
        qh = q_s[rows[b], sl]
        kh = (k_s[rows[b], sl] * (HEAD_DIM ** -0.5)).astype(BF16)
        vh = v_s[rows[b], sl]
        kt = kh.T
        st = st_s[b, h]
        stb = st.astype(BF16)
        yield
        qk = _dot(qh, jnp.concatenate([kt, stb[:, HEAD_DIM:]], axis=1))
        qc = _dot(qh, stb[:, :HEAD_DIM])
        yield
        s = qk[:, :L] * p
        den = jnp.sum(s, axis=-1, keepdims=True) + sc * qk[:, L:]
        inv = 0.5 / jnp.maximum(jnp.abs(den), jnp.exp(-(b_rep + mm)))
        wk = jnp.exp(gq_rep - mm_last)
        wkv = jnp.concatenate([_lane_tile(wk, HEAD_DIM) * vh.astype(F32), wk], axis=1).astype(BF16)
        yield
        num = _dot(s.astype(BF16), vh) + _lane_tile(sc, HEAD_DIM) * qc
        upd = _dot(kt, wkv)
        yield
        decay = sc[L - 1:L, :]
        st_s[b, h] = _lane_tile(decay, HEAD_DIM + LANES) * st + upd
        m_s[b, h] = jnp.broadcast_to(b_rep[L - 1:L, :] + mm_last, (SUBLANES, LANES))
        out[b] = num * _lane_tile(inv, HEAD_DIM)

    def mlstm_head(h, terms):
        sl = _blk(h)
        hparts = [None] * nb
        yield from _zip_stages([mlstm_chunk(b, h, terms[b], hparts) for b in range(nb)])
        hh = og_s[:, sl] * jnp.concatenate(hparts, axis=0)
        mu = jnp.mean(hh, axis=-1, keepdims=True)
        dl = hh - mu
        var = jnp.mean(dl * dl, axis=-1, keepdims=True)
        hn = dl * lax.rsqrt(var + EPS) * vec(V_M_NORM_G, sl)
        yield
        zh = proj(M_Z + h)
        yield
        y = (hn + vec(V_M_SKIP, sl) * xc_s[:, sl]) * (zh * _tanh1(zh))
        y_s[1, :, sl] = y.astype(BF16)

    def branch_out(n, j):
        gate2 = _tanh1(proj(GATE + N_HEADS * n + j))
        yield
        return gate2 * _dot(y_s[n], wbr_ref[n, j])

    def branch_pair(j):
        first = yield from branch_out(0, j)
        yield
        second = yield from branch_out(2, j)
        acc_s[:, _blk(j)] = first + second

    def branch_last(j):
        last = yield from branch_out(1, j)
        mg_s[:, _blk(j)] = (acc_s[:, _blk(j)] + last).astype(BF16)

    heads = range(N_HEADS)
    _interleave([f(i) for i in heads for f in (lru_block, xattn_head, mlstm_front)])
    gate_terms = mlstm_gates()
    _interleave([f(i) for i in heads for f in (lambda i: mlstm_head(i, gate_terms), branch_pair)])
    _interleave([branch_last(j) for j in heads])
    ssq = None
    for j in range(N_HEADS):
        sl = _blk(j)
        xo = x_ref[:, :, sl].reshape(R, HEAD_DIM) + _dot(mg_s[...], wout_ref[j])
        o_ref[:, :, sl] = xo.reshape(nb, tr, HEAD_DIM)
        part = jnp.sum(xo * xo, axis=-1, keepdims=True)
        ssq = part if ssq is None else ssq + part
    scale = lax.rsqrt(ssq * (1.0 / D_MODEL) + EPS)
    o_ref[...] = (o_ref[...].reshape(R, D_MODEL) * scale * vec(V_FINAL_G)).reshape(nb, tr, D_MODEL)


def _resident(a):
    nd = a.ndim
    return pl.BlockSpec(a.shape, lambda i: (0,) * nd, pipeline_mode=pl.Buffered(1))


def _col_blocks(w, scales):
    g, k, n = w.shape
    groups = n // D_MODEL
    per = D_MODEL // MXU_COLS
    return pl.pallas_call(
        _col_block_kernel,
        grid=(g, groups),
        in_specs=[pl.BlockSpec(memory_space=pltpu.SMEM),
                  pl.BlockSpec((None, k, D_MODEL), lambda i, c: (i, 0, c))],
        out_specs=pl.BlockSpec((None, per, k, MXU_COLS), lambda i, c: (i, c, 0, 0)),
        out_shape=jax.ShapeDtypeStruct((g, groups * per, k, MXU_COLS), BF16),
        compiler_params=pltpu.CompilerParams(vmem_limit_bytes=VMEM_LIMIT_BYTES),
        name="col_blocks",
    )(jnp.asarray(scales, F32), w)


def kernel(x, mem, norm_g, w_in, b_in, lru_conv_w, lru_conv_b, lru_w_a, lru_b_a, lru_w_x, lru_b_x,
           lru_lambda, m_conv_w, m_conv_b, m_w_q, m_w_k, m_w_v, m_w_o, m_b_o, m_w_if, m_b_if,
           m_norm_g, m_skip, mem_norm_g, w_mem_kv, w_branch, w_out, final_norm_g):
    B, S, D = x.shape
    assert D == D_MODEL and S % TIME_TILE == 0 and norm_g.shape[0] == 1
    tr = TIME_TILE
    H = N_HEADS
    row = lambda v: v.reshape(1, -1).astype(F32)

    kv = pl.pallas_call(
        _kv_kernel,
        grid=(2,),
        in_specs=[pl.BlockSpec((B, N_MEM, D), lambda c: (0, 0, 0)),
                  pl.BlockSpec((1, D), lambda c: (0, 0)),
                  pl.BlockSpec((None, D, D), lambda c: (0, 0, c))],
        out_specs=pl.BlockSpec((B, H, N_MEM, HEAD_DIM), lambda c: (0, c, 0, 0)),
        out_shape=jax.ShapeDtypeStruct((B, 2 * H, N_MEM, HEAD_DIM), BF16),
        compiler_params=pltpu.CompilerParams(vmem_limit_bytes=VMEM_LIMIT_BYTES),
        name="mem_kv",
    )(mem, row(mem_norm_g[0]), w_mem_kv)

    in_scale = [1.0, 0.5, 1.0, 0.5, 1.0, 0.5, 0.5, 0.5, 0.5]
    w_in_b = _col_blocks(w_in, in_scale)[0]
    w_br = _col_blocks(w_branch[0], [0.5] * N_BRANCH)
    w_out_b = _col_blocks(w_out, [1.0])[0]

    pad_lanes = lambda a: jnp.pad(a, ((0, 0), (0, LANES - H)))
    w_if = jnp.concatenate([pad_lanes(m_w_if[0][:, :H]), pad_lanes(m_w_if[0][:, H:])], axis=1).astype(BF16)
    b_if2 = m_b_if[0].reshape(1, 2 * H).astype(F32)
    b_if = jnp.pad(jnp.concatenate([pad_lanes(b_if2[:, :H]), pad_lanes(b_if2[:, H:])], axis=1),
                   ((0, 0), (0, D - 2 * LANES)))
    vecs = jnp.concatenate(
        [row(norm_g[0]), row(0.5 * lru_conv_b[0]), row(0.5 * lru_b_a[0]), row(0.5 * lru_b_x[0]),
         row(lru_lambda[0]), row(0.5 * m_conv_b[0]), row(0.5 * m_b_o[0]), row(m_norm_g[0]), row(m_skip[0]),
         row(final_norm_g), b_if, 0.5 * lru_conv_w[0], 0.5 * m_conv_w[0],
         (b_in[0].reshape(IN_GROUPS, D) * jnp.asarray(in_scale, F32)[:, None])], axis=0).astype(F32)
    head_w = (jnp.stack([lru_w_a[0], lru_w_x[0], m_w_q[0], m_w_k[0], m_w_v[0], m_w_o[0]])
              * jnp.asarray([1.0, 1.0, 1.0, 1.0, 1.0, 0.5], F32)[:, None, None, None]).astype(BF16)
    weights = [vecs, w_in_b, head_w, w_if, kv, w_br, w_out_b]

    R = B * tr
    tok = pl.BlockSpec((B, tr, D), lambda i: (0, i, 0))
    scratch = [pltpu.VMEM((R, D), BF16),
               pltpu.VMEM((B, SUBLANES, D), F32),
               pltpu.VMEM((B, SUBLANES, D), F32),
               pltpu.VMEM((B, SUBLANES, D), F32),
               pltpu.VMEM((R, D), BF16),
               pltpu.VMEM((R, D), BF16),
               pltpu.VMEM((R, D), BF16),
               pltpu.VMEM((R, D), F32),
               pltpu.VMEM((R, D), F32),
               pltpu.VMEM((R, 2 * LANES), F32),
               pltpu.VMEM((B, H, HEAD_DIM, HEAD_DIM + LANES), F32),
               pltpu.VMEM((B, H, SUBLANES, LANES), F32),
               pltpu.VMEM((N_BRANCH, R, D), BF16),
               pltpu.VMEM((R, D), F32),
               pltpu.VMEM((R, D), BF16)]
    return pl.pallas_call(
        functools.partial(_layer_kernel, nb=B, tr=tr),
        grid=(S // tr,),
        in_specs=[tok] + [_resident(w) for w in weights],
        out_specs=tok,
        out_shape=jax.ShapeDtypeStruct((B, S, D), F32),
        scratch_shapes=scratch,
        compiler_params=pltpu.CompilerParams(
            dimension_semantics=("arbitrary",),
            vmem_limit_bytes=VMEM_LIMIT_BYTES),
        name="hybrid_layer",
    )(x, *weights)
```

```python
import functools

import jax
import jax.numpy as jnp
from jax import lax
from jax.experimental import pallas as pl
from jax.experimental.pallas import tpu as pltpu

D_MODEL = 1024
N_HEADS = 4
HEAD_DIM = D_MODEL // N_HEADS
N_MEM = 256
N_BRANCH = 3
EPS = 1e-6
CONV_WIDTH = 4
LRU_C = 8.0
LANES = 128
SUBLANES = 8
MXU_COLS = 256
TIME_TILE = 128
ROW_CHUNK = 32
VMEM_LIMIT_BYTES = 60 * 1024 * 1024

LRU_X, LRU_Z, M_X, M_Z, XA_Q, XA_Z, GATE = 0, 4, 8, 12, 16, 20, 24
IN_GROUPS = 9
(V_NORM_G, V_LRU_CONV_B, V_LRU_B_A, V_LRU_B_X, V_LRU_LAMBDA, V_M_CONV_B, V_M_B_O, V_M_NORM_G, V_M_SKIP,
 V_FINAL_G, V_B_IF) = range(11)
V_LRU_CONV_W = 11
V_M_CONV_W = V_LRU_CONV_W + CONV_WIDTH
V_B_IN = V_M_CONV_W + CONV_WIDTH
N_VEC_ROWS = V_B_IN + IN_GROUPS
HW_LRU_A, HW_LRU_X, HW_M_Q, HW_M_K, HW_M_V, HW_M_O = range(6)

BF16 = jnp.bfloat16
F32 = jnp.float32


def _dot(a, b):
    return jnp.dot(a, b, preferred_element_type=F32)


def _dot_nt(a, b):
    return lax.dot_general(a, b, (((1,), (1,)), ((), ())), preferred_element_type=F32)


def _tanh1(v):
    return jnp.tanh(v) + 1.0


def _log_sigmoid(x):
    return jnp.minimum(x, 0.0) - jnp.log1p(jnp.exp(-jnp.abs(x)))


def _rms_norm(x, g):
    ms = jnp.mean(x * x, axis=-1, keepdims=True)
    return x * lax.rsqrt(ms + EPS) * g


def _blk(j):
    return slice(j * HEAD_DIM, (j + 1) * HEAD_DIM)


def _prefix_rows(x, op, identity):
    n = x.shape[0]
    row = lax.broadcasted_iota(jnp.int32, x.shape, 0)
    k = 1
    while k < n:
        x = op(x, jnp.where(row >= k, pltpu.roll(x, k, 0), identity))
        k *= 2
    return x


def _lane_tile(x, n):
    reps = n // LANES
    return x if reps == 1 else jnp.concatenate([x] * reps, axis=-1)


def _lane_bcast(x, lane):
    return jnp.broadcast_to(x[:, lane:lane + 1], x.shape)


def _causal_conv(xb, tail, cw, cb):
    t, c = xb.shape
    groups = t // SUBLANES
    full = jnp.concatenate([tail, xb], axis=0).reshape(groups + 1, SUBLANES, c)
    sub = lax.broadcasted_iota(jnp.int32, (groups, SUBLANES, c), 1)
    acc = cb + cw[CONV_WIDTH - 1:CONV_WIDTH, :] * xb
    for k in range(1, CONV_WIDTH):
        r = pltpu.roll(full, k, 1)
        shifted = jnp.where(sub < k, r[:-1], r[1:]).reshape(t, c)
        acc = acc + cw[CONV_WIDTH - 1 - k:CONV_WIDTH - k, :] * shifted
    return acc


def _lru_scan(a, b, h0):
    t, c = a.shape
    row = lax.broadcasted_iota(jnp.int32, (SUBLANES, c), 0)
    out = []
    for j in range(t // SUBLANES):
        aj = a[j * SUBLANES:(j + 1) * SUBLANES, :]
        bj = b[j * SUBLANES:(j + 1) * SUBLANES, :]
        for k in (1, 2, 4):
            valid = row >= k
            a_sh = pltpu.roll(aj, k, 0)
            b_sh = pltpu.roll(bj, k, 0)
            bj = bj + aj * jnp.where(valid, b_sh, 0.0)
            aj = aj * jnp.where(valid, a_sh, 1.0)
        hj = bj + aj * h0
        out.append(hj)
        h0 = jnp.broadcast_to(hj[SUBLANES - 1:SUBLANES, :], (SUBLANES, c))
    return jnp.concatenate(out, axis=0), h0


def _zip_stages(gens):
    gens = list(gens)
    while gens:
        alive = []
        for g in gens:
            try:
                next(g)
                alive.append(g)
            except StopIteration:
                pass
        gens = alive
        if gens:
            yield


def _interleave(gens, skew=1):
    pending = list(gens)
    active = []
    rnd = 0
    while pending or active:
        while pending and (len(gens) - len(pending)) * skew <= rnd:
            active.append(pending.pop(0))
        alive = []
        for g in active:
            try:
                next(g)
                alive.append(g)
            except StopIteration:
                pass
        active = alive
        rnd += 1


def _kv_kernel(mem_ref, g_ref, w_ref, o_ref):
    nb = mem_ref.shape[0]
    mn = _rms_norm(mem_ref[...].reshape(nb * N_MEM, D_MODEL), g_ref[...]).astype(BF16)
    w = w_ref[...].astype(BF16)
    for j in range(N_HEADS):
        o_ref[:, j] = _dot(mn, w[:, _blk(j)]).astype(BF16).reshape(nb, N_MEM, HEAD_DIM)


def _col_block_kernel(scale_ref, w_ref, o_ref):
    step = pl.program_id(0) * pl.num_programs(1) + pl.program_id(1)
    w = w_ref[...] * scale_ref[step]
    for j in range(o_ref.shape[0]):
        o_ref[j] = w[:, j * MXU_COLS:(j + 1) * MXU_COLS].astype(BF16)


def _layer_kernel(x_ref, vec_ref, win_ref, hw_ref, wif_ref, kv_ref, wbr_ref, wout_ref,
                  o_ref,
                  hb_s, ltail_s, mtail_s, hcar_s, q_s, k_s, v_s, og_s, xc_s, g_s, st_s, m_s, y_s, acc_s, mg_s,
                  *, nb, tr):
    L = tr
    R = nb * tr
    rows = [slice(b * tr, (b + 1) * tr) for b in range(nb)]

    @pl.when(pl.program_id(0) == 0)
    def _():
        ltail_s[...] = jnp.zeros(ltail_s.shape, F32)
        mtail_s[...] = jnp.zeros(mtail_s.shape, F32)
        hcar_s[...] = jnp.zeros(hcar_s.shape, F32)
        st_s[...] = jnp.zeros(st_s.shape, F32)
        m_s[...] = jnp.zeros(m_s.shape, F32)

    def vec(i, sl=slice(None)):
        return vec_ref[i:i + 1, sl]

    hb_s[...] = _rms_norm(x_ref[...].reshape(R, D_MODEL), vec(V_NORM_G)).astype(BF16)

    def proj(j):
        return _dot(hb_s[...], win_ref[j]) + vec(V_B_IN + j // N_HEADS, _blk(j % N_HEADS))

    def conv(x, tail_ref, cw_row, cb_row, sl):
        cw = vec_ref[cw_row:cw_row + CONV_WIDTH, sl]
        parts = []
        for b in range(nb):
            xb = x[rows[b], :]
            parts.append(_causal_conv(xb, tail_ref[b, :, sl], cw, vec(cb_row, sl)))
            tail_ref[b, :, sl] = xb[tr - SUBLANES:, :]
        return jnp.concatenate(parts, axis=0)


    def lru_block(g):
        sl = _blk(g)
        px = proj(LRU_X + g)
        yield
        u = conv(px, ltail_s, V_LRU_CONV_W, V_LRU_CONV_B, sl)
        ub = u.astype(BF16)
        yield
        r_pre = _dot(ub, hw_ref[HW_LRU_A, g]) + vec(V_LRU_B_A, sl)
        i_pre = _dot(ub, hw_ref[HW_LRU_X, g]) + vec(V_LRU_B_X, sl)
        yield
        zh = proj(LRU_Z + g)
        ls_c = (0.5 * LRU_C) * _log_sigmoid(vec(V_LRU_LAMBDA, sl))
        yield
        for b in range(nb):
            h0 = hcar_s[b, :, sl]
            for c in range(tr // ROW_CHUNK):
                rs = slice(b * tr + c * ROW_CHUNK, b * tr + (c + 1) * ROW_CHUNK)
                log_a = ls_c * _tanh1(r_pre[rs, :])
                a = jnp.exp(log_a)
                om = -jnp.tanh(log_a) * (1.0 + a * a)
                mult = jnp.where(om > 0.0, om * lax.rsqrt(om), 0.0)
                bb = mult * _tanh1(i_pre[rs, :]) * u[rs, :]
                h, h0 = _lru_scan(a, bb, h0)
                y_s[0, rs, sl] = (h * (zh[rs, :] * _tanh1(zh[rs, :]))).astype(BF16)
                yield
            hcar_s[b, :, sl] = h0

    def xattn_head(h):
        sl = _blk(h)
        q = proj(XA_Q + h).astype(BF16)
        yield
        scores = [_dot_nt(q[rows[b], :], kv_ref[b, h]) * (HEAD_DIM ** -0.5) for b in range(nb)]
        yield
        probs = []
        for s in scores:
            e = jnp.exp(s - jnp.max(s, axis=-1, keepdims=True))
            probs.append((e * (1.0 / jnp.sum(e, axis=-1, keepdims=True))).astype(BF16))
            yield
        o = jnp.concatenate([_dot(probs[b], kv_ref[b, N_HEADS + h]) for b in range(nb)], axis=0)
        yield
        zh = proj(XA_Z + h)
        yield
        y_s[2, :, sl] = (o * (zh * _tanh1(zh))).astype(BF16)

    def mlstm_front(h):
        sl = _blk(h)
        xm = proj(M_X + h)
        yield
        ch = conv(xm, mtail_s, V_M_CONV_W, V_M_CONV_B, sl)
        xc = ch * _tanh1(ch)
        xc_s[:, sl] = xc
        xcb = xc.astype(BF16)
        xmb = xm.astype(BF16)
        yield
        q_s[:, sl] = _dot(xcb, hw_ref[HW_M_Q, h]).astype(BF16)
        k_s[:, sl] = _dot(xcb, hw_ref[HW_M_K, h]).astype(BF16)
        yield
        v_s[:, sl] = _dot(xmb, hw_ref[HW_M_V, h]).astype(BF16)
        og_s[:, sl] = _tanh1(_dot(xmb, hw_ref[HW_M_O, h]) + vec(V_M_B_O, sl))
        fronts_done.append(h)

    fronts_done = []
    gate_terms = []

    def mlstm_gates():
        while len(fronts_done) < N_HEADS:
            yield
        g_s[...] = (_dot(q_s[...], wif_ref[0:D_MODEL, :])
                    + _dot(k_s[...], wif_ref[D_MODEL:2 * D_MODEL, :])
                    + _dot(v_s[...], wif_ref[2 * D_MODEL:3 * D_MODEL, :])
                    + vec(V_B_IF, slice(0, 2 * LANES)))
        yield
        for b in range(nb):
            bcum = _prefix_rows(_log_sigmoid(g_s[rows[b], LANES:2 * LANES]), jnp.add, 0.0)
            gq = g_s[rows[b], 0:LANES] - bcum
            cmax = _prefix_rows(gq, jnp.maximum, -jnp.inf)
            gate_terms.append((bcum, gq, cmax, gq.T))
            yield

    causal = (lax.broadcasted_iota(jnp.int32, (L, L), 0) >= lax.broadcasted_iota(jnp.int32, (L, L), 1))

    def mlstm_chunk(b, h, terms, out):
        sl = _blk(h)
        bcum, gq, cmax, gq_t = terms
        b_rep = _lane_bcast(bcum, h)
        gq_rep = _lane_bcast(gq, h)
        gq_row = gq_t[h:h + 1, :]
        m_prev = m_s[b, h, 0:1, :]
        mm = jnp.maximum(m_prev, _lane_bcast(cmax, h))
        mm_last = mm[L - 1:L, :]
        p = jnp.exp(jnp.where(causal, gq_row - _lane_tile(mm, L), -jnp.inf))
        sc = jnp.exp(m_prev - mm)
        qh = q_s[rows[b], sl]
        kh = (k_s[rows[b], sl] * (HEAD_DIM ** -0.5)).astype(BF16)
        vh = v_s[rows[b], sl]
        kt = kh.T
        st = st_s[b, h]
        stb = st.astype(BF16)
        yield
        qk = _dot(qh, jnp.concatenate([kt, stb[:, HEAD_DIM:]], axis=1))
        qc = _dot(qh, stb[:, :HEAD_DIM])
        yield
        s = qk[:, :L] * p
        den = jnp.sum(s, axis=-1, keepdims=True) + sc * qk[:, L:]
        inv = 0.5 / jnp.maximum(jnp.abs(den), jnp.exp(-(b_rep + mm)))
        wk = jnp.exp(gq_rep - mm_last)
        wkv = jnp.concatenate([_lane_tile(wk, HEAD_DIM) * vh.astype(F32), wk], axis=1).astype(BF16)
        yield
        num = _dot(s.astype(BF16), vh) + _lane_tile(sc, HEAD_DIM) * qc
        upd = _dot(kt, wkv)
        yield
        decay = sc[L - 1:L, :]
        st_s[b, h] = _lane_tile(decay, HEAD_DIM + LANES) * st + upd
        m_s[b, h] = jnp.broadcast_to(b_rep[L - 1:L, :] + mm_last, (SUBLANES, LANES))
        out[b] = num * _lane_tile(inv, HEAD_DIM)

    def mlstm_head(h, terms):
        sl = _blk(h)
        hparts = [None] * nb
        yield from _zip_stages([mlstm_chunk(b, h, terms[b], hparts) for b in range(nb)])
        hh = og_s[:, sl] * jnp.concatenate(hparts, axis=0)
        mu = jnp.mean(hh, axis=-1, keepdims=True)
        dl = hh - mu
        var = jnp.mean(dl * dl, axis=-1, keepdims=True)
        hn = dl * lax.rsqrt(var + EPS) * vec(V_M_NORM_G, sl)
        yield
        zh = proj(M_Z + h)
        yield
        y = (hn + vec(V_M_SKIP, sl) * xc_s[:, sl]) * (zh * _tanh1(zh))
        y_s[1, :, sl] = y.astype(BF16)

    def branch_out(n, j):
        gate2 = _tanh1(proj(GATE + N_HEADS * n + j))
        yield
        return gate2 * _dot(y_s[n], wbr_ref[n, j])

    def branch_pair(j):
        first = yield from branch_out(0, j)
        yield
        second = yield from branch_out(2, j)
        acc_s[:, _blk(j)] = first + second

    def branch_last(j):
        last = yield from branch_out(1, j)
        mg_s[:, _blk(j)] = (acc_s[:, _blk(j)] + last).astype(BF16)

    heads = range(N_HEADS)
    _interleave([f(i) for i in heads for f in (lru_block, mlstm_front, xattn_head)] + [mlstm_gates()])
    _interleave([f(i) for i in heads for f in (lambda i: mlstm_head(i, gate_terms), branch_pair)])
    _interleave([branch_last(j) for j in heads])
    ssq = None
    for j in range(N_HEADS):
        sl = _blk(j)
        xo = x_ref[:, :, sl].reshape(R, HEAD_DIM) + _dot(mg_s[...], wout_ref[j])
        o_ref[:, :, sl] = xo.reshape(nb, tr, HEAD_DIM)
        part = jnp.sum(xo * xo, axis=-1, keepdims=True)
        ssq = part if ssq is None else ssq + part
    scale = lax.rsqrt(ssq * (1.0 / D_MODEL) + EPS)
    o_ref[...] = (o_ref[...].reshape(R, D_MODEL) * scale * vec(V_FINAL_G)).reshape(nb, tr, D_MODEL)


def _resident(a):
    nd = a.ndim
    return pl.BlockSpec(a.shape, lambda i: (0,) * nd, pipeline_mode=pl.Buffered(1))


def _col_blocks(w, scales):
    g, k, n = w.shape
    groups = n // D_MODEL
    per = D_MODEL // MXU_COLS
    return pl.pallas_call(
        _col_block_kernel,
        grid=(g, groups),
        in_specs=[pl.BlockSpec(memory_space=pltpu.SMEM),
                  pl.BlockSpec((None, k, D_MODEL), lambda i, c: (i, 0, c))],
        out_specs=pl.BlockSpec((None, per, k, MXU_COLS), lambda i, c: (i, c, 0, 0)),
        out_shape=jax.ShapeDtypeStruct((g, groups * per, k, MXU_COLS), BF16),
        compiler_params=pltpu.CompilerParams(vmem_limit_bytes=VMEM_LIMIT_BYTES),
        name="col_blocks",
    )(jnp.asarray(scales, F32), w)


def kernel(x, mem, norm_g, w_in, b_in, lru_conv_w, lru_conv_b, lru_w_a, lru_b_a, lru_w_x, lru_b_x,
           lru_lambda, m_conv_w, m_conv_b, m_w_q, m_w_k, m_w_v, m_w_o, m_b_o, m_w_if, m_b_if,
           m_norm_g, m_skip, mem_norm_g, w_mem_kv, w_branch, w_out, final_norm_g):
    B, S, D = x.shape
    assert D == D_MODEL and S % TIME_TILE == 0 and norm_g.shape[0] == 1
    tr = TIME_TILE
    H = N_HEADS
    row = lambda v: v.reshape(1, -1).astype(F32)

    kv = pl.pallas_call(
        _kv_kernel,
        grid=(2,),
        in_specs=[pl.BlockSpec((B, N_MEM, D), lambda c: (0, 0, 0)),
                  pl.BlockSpec((1, D), lambda c: (0, 0)),
                  pl.BlockSpec((None, D, D), lambda c: (0, 0, c))],
        out_specs=pl.BlockSpec((B, H, N_MEM, HEAD_DIM), lambda c: (0, c, 0, 0)),
        out_shape=jax.ShapeDtypeStruct((B, 2 * H, N_MEM, HEAD_DIM), BF16),
        compiler_params=pltpu.CompilerParams(vmem_limit_bytes=VMEM_LIMIT_BYTES),
        name="mem_kv",
    )(mem, row(mem_norm_g[0]), w_mem_kv)

    in_scale = [1.0, 0.5, 1.0, 0.5, 1.0, 0.5, 0.5, 0.5, 0.5]
    w_in_b = _col_blocks(w_in, in_scale)[0]
    w_br = _col_blocks(w_branch[0], [0.5] * N_BRANCH)
    w_out_b = _col_blocks(w_out, [1.0])[0]

    pad_lanes = lambda a: jnp.pad(a, ((0, 0), (0, LANES - H)))
    w_if = jnp.concatenate([pad_lanes(m_w_if[0][:, :H]), pad_lanes(m_w_if[0][:, H:])], axis=1).astype(BF16)
    b_if2 = m_b_if[0].reshape(1, 2 * H).astype(F32)
    b_if = jnp.pad(jnp.concatenate([pad_lanes(b_if2[:, :H]), pad_lanes(b_if2[:, H:])], axis=1),
                   ((0, 0), (0, D - 2 * LANES)))
    vecs = jnp.concatenate(
        [row(norm_g[0]), row(0.5 * lru_conv_b[0]), row(0.5 * lru_b_a[0]), row(0.5 * lru_b_x[0]),
         row(lru_lambda[0]), row(0.5 * m_conv_b[0]), row(0.5 * m_b_o[0]), row(m_norm_g[0]), row(m_skip[0]),
         row(final_norm_g), b_if, 0.5 * lru_conv_w[0], 0.5 * m_conv_w[0],
         (b_in[0].reshape(IN_GROUPS, D) * jnp.asarray(in_scale, F32)[:, None])], axis=0).astype(F32)
    head_w = (jnp.stack([lru_w_a[0], lru_w_x[0], m_w_q[0], m_w_k[0], m_w_v[0], m_w_o[0]])
              * jnp.asarray([1.0, 1.0, 1.0, 1.0, 1.0, 0.5], F32)[:, None, None, None]).astype(BF16)
    weights = [vecs, w_in_b, head_w, w_if, kv, w_br, w_out_b]

    R = B * tr
    tok = pl.BlockSpec((B, tr, D), lambda i: (0, i, 0))
    scratch = [pltpu.VMEM((R, D), BF16),
               pltpu.VMEM((B, SUBLANES, D), F32),
               pltpu.VMEM((B, SUBLANES, D), F32),
               pltpu.VMEM((B, SUBLANES, D), F32),
               pltpu.VMEM((R, D), BF16),
               pltpu.VMEM((R, D), BF16),
               pltpu.VMEM((R, D), BF16),
               pltpu.VMEM((R, D), F32),
               pltpu.VMEM((R, D), F32),
               pltpu.VMEM((R, 2 * LANES), F32),
               pltpu.VMEM((B, H, HEAD_DIM, HEAD_DIM + LANES), F32),
               pltpu.VMEM((B, H, SUBLANES, LANES), F32),
               pltpu.VMEM((N_BRANCH, R, D), BF16),
               pltpu.VMEM((R, D), F32),
               pltpu.VMEM((R, D), BF16)]
    return pl.pallas_call(
        functools.partial(_layer_kernel, nb=B, tr=tr),
        grid=(S // tr,),
        in_specs=[tok] + [_resident(w) for w in weights],
        out_specs=tok,
        out_shape=jax.ShapeDtypeStruct((B, S, D), F32),
        scratch_shapes=scratch,
        compiler_params=pltpu.CompilerParams(
            dimension_semantics=("arbitrary",),
            vmem_limit_bytes=VMEM_LIMIT_BYTES),
        name="hybrid_layer",
    )(x, *weights)
```

```python
import functools

import jax
import jax.numpy as jnp
from jax import lax
from jax.experimental import pallas as pl
from jax.experimental.pallas import tpu as pltpu

D_MODEL = 1024
N_HEADS = 4
HEAD_DIM = D_MODEL // N_HEADS
N_MEM = 256
N_BRANCH = 3
EPS = 1e-6
CONV_WIDTH = 4
LRU_C = 8.0
LANES = 128
SUBLANES = 8
MXU_COLS = 256
TIME_TILE = 128
ROW_CHUNK = 32
VMEM_LIMIT_BYTES = 60 * 1024 * 1024

LRU_X, LRU_Z, M_X, M_Z, XA_Q, XA_Z, GATE = 0, 4, 8, 12, 16, 20, 24
IN_GROUPS = 9
IN_SCALE = (1.0, 0.5, 1.0, 0.5, 1.0, 0.5, 0.5, 0.5, 0.5)
BRANCH_SCALE = 0.5
STAGE_COLS = 2 * MXU_COLS
(V_NORM_G, V_LRU_CONV_B, V_LRU_B_A, V_LRU_B_X, V_LRU_LAMBDA, V_M_CONV_B, V_M_B_O, V_M_NORM_G, V_M_SKIP,
 V_FINAL_G, V_B_IF) = range(11)
V_LRU_CONV_W = 11
V_M_CONV_W = V_LRU_CONV_W + CONV_WIDTH
V_B_IN = V_M_CONV_W + CONV_WIDTH
N_VEC_ROWS = V_B_IN + IN_GROUPS
HW_LRU_A, HW_LRU_X, HW_M_Q, HW_M_K, HW_M_V, HW_M_O = range(6)

BF16 = jnp.bfloat16
F32 = jnp.float32


def _dot(a, b):
    return jnp.dot(a, b, preferred_element_type=F32)


def _dot_nt(a, b):
    return lax.dot_general(a, b, (((1,), (1,)), ((), ())), preferred_element_type=F32)


def _tanh1(v):
    return jnp.tanh(v) + 1.0


def _log_sigmoid(x):
    return jnp.minimum(x, 0.0) - jnp.log1p(jnp.exp(-jnp.abs(x)))


def _rms_norm(x, g):
    ms = jnp.mean(x * x, axis=-1, keepdims=True)
    return x * lax.rsqrt(ms + EPS) * g


def _blk(j):
    return slice(j * HEAD_DIM, (j + 1) * HEAD_DIM)


def _prefix_rows(x, op, identity):
    n = x.shape[0]
    row = lax.broadcasted_iota(jnp.int32, x.shape, 0)
    k = 1
    while k < n:
        x = op(x, jnp.where(row >= k, pltpu.roll(x, k, 0), identity))
        k *= 2
    return x


def _lane_tile(x, n):
    reps = n // LANES
    return x if reps == 1 else jnp.concatenate([x] * reps, axis=-1)


def _lane_bcast(x, lane):
    return jnp.broadcast_to(x[:, lane:lane + 1], x.shape)


def _causal_conv(xb, tail, cw, cb):
    t, c = xb.shape
    groups = t // SUBLANES
    full = jnp.concatenate([tail, xb], axis=0).reshape(groups + 1, SUBLANES, c)
    sub = lax.broadcasted_iota(jnp.int32, (groups, SUBLANES, c), 1)
    acc = cb + cw[CONV_WIDTH - 1:CONV_WIDTH, :] * xb
    for k in range(1, CONV_WIDTH):
        r = pltpu.roll(full, k, 1)
        shifted = jnp.where(sub < k, r[:-1], r[1:]).reshape(t, c)
        acc = acc + cw[CONV_WIDTH - 1 - k:CONV_WIDTH - k, :] * shifted
    return acc


def _lru_scan(a, b, h0):
    t, c = a.shape
    row = lax.broadcasted_iota(jnp.int32, (SUBLANES, c), 0)
    out = []
    for j in range(t // SUBLANES):
        aj = a[j * SUBLANES:(j + 1) * SUBLANES, :]
        bj = b[j * SUBLANES:(j + 1) * SUBLANES, :]
        for k in (1, 2, 4):
            valid = row >= k
            a_sh = pltpu.roll(aj, k, 0)
            b_sh = pltpu.roll(bj, k, 0)
            bj = bj + aj * jnp.where(valid, b_sh, 0.0)
            aj = aj * jnp.where(valid, a_sh, 1.0)
        hj = bj + aj * h0
        out.append(hj)
        h0 = jnp.broadcast_to(hj[SUBLANES - 1:SUBLANES, :], (SUBLANES, c))
    return jnp.concatenate(out, axis=0), h0


def _zip_stages(gens):
    gens = list(gens)
    while gens:
        alive = []
        for g in gens:
            try:
                next(g)
                alive.append(g)
            except StopIteration:
                pass
        gens = alive
        if gens:
            yield


def _interleave(gens, skew=1):
    pending = list(gens)
    active = []
    rnd = 0
    while pending or active:
        while pending and (len(gens) - len(pending)) * skew <= rnd:
            active.append(pending.pop(0))
        alive = []
        for g in active:
            try:
                next(g)
                alive.append(g)
            except StopIteration:
                pass
        active = alive
        rnd += 1


def _kv_kernel(mem_ref, g_ref, w_ref, o_ref):
    nb = mem_ref.shape[0]
    mn = _rms_norm(mem_ref[...].reshape(nb * N_MEM, D_MODEL), g_ref[...]).astype(BF16)
    w = w_ref[...].astype(BF16)
    for j in range(N_HEADS):
        o_ref[:, j] = _dot(mn, w[:, _blk(j)]).astype(BF16).reshape(nb, N_MEM, HEAD_DIM)


def _load_weights(win_hbm, wbr_hbm, wout_hbm, win_s, wbr_s, wout_s, stage_s, sem):
    per = STAGE_COLS // MXU_COLS
    chunks = []
    for c in range(win_hbm.shape[-1] // STAGE_COLS):
        chunks.append((win_hbm.at[0, :, pl.ds(c * STAGE_COLS, STAGE_COLS)], win_s, c * per,
                       IN_SCALE[c * STAGE_COLS // D_MODEL]))
    for n in range(N_BRANCH):
        for c in range(D_MODEL // STAGE_COLS):
            chunks.append((wbr_hbm.at[0, n, :, pl.ds(c * STAGE_COLS, STAGE_COLS)], wbr_s.at[n], c * per,
                           BRANCH_SCALE))
    for c in range(D_MODEL // STAGE_COLS):
        chunks.append((wout_hbm.at[0, :, pl.ds(c * STAGE_COLS, STAGE_COLS)], wout_s, c * per, 1.0))

    def copy(i):
        return pltpu.make_async_copy(chunks[i][0], stage_s.at[i % 2], sem.at[i % 2])

    copy(0).start()
    for i, (_, dest, blk0, scale) in enumerate(chunks):
        if i + 1 < len(chunks):
            copy(i + 1).start()
        copy(i).wait()
        for j in range(per):
            w = stage_s[i % 2, :, j * MXU_COLS:(j + 1) * MXU_COLS]
            dest[blk0 + j] = (w if scale == 1.0 else w * scale).astype(BF16)


def _layer_kernel(x_ref, vec_ref, hw_ref, wif_ref, kv_ref, win_hbm, wbr_hbm, wout_hbm,
                  o_ref,
                  win_s, wbr_s, wout_s, stage_s, dma_sem,
                  hb_s, ltail_s, mtail_s, hcar_s, q_s, k_s, v_s, og_s, xc_s, g_s, st_s, m_s, y_s, acc_s, mg_s,
                  *, nb, tr):
    L = tr
    R = nb * tr
    rows = [slice(b * tr, (b + 1) * tr) for b in range(nb)]

    @pl.when(pl.program_id(0) == 0)
    def _():
        ltail_s[...] = jnp.zeros(ltail_s.shape, F32)
        mtail_s[...] = jnp.zeros(mtail_s.shape, F32)
        hcar_s[...] = jnp.zeros(hcar_s.shape, F32)
        st_s[...] = jnp.zeros(st_s.shape, F32)
        m_s[...] = jnp.zeros(m_s.shape, F32)
        _load_weights(win_hbm, wbr_hbm, wout_hbm, win_s, wbr_s, wout_s, stage_s, dma_sem)

    def vec(i, sl=slice(None)):
        return vec_ref[i:i + 1, sl]

    hb_s[...] = _rms_norm(x_ref[...].reshape(R, D_MODEL), vec(V_NORM_G)).astype(BF16)

    def proj(j):
        return _dot(hb_s[...], win_s[j]) + vec(V_B_IN + j // N_HEADS, _blk(j % N_HEADS))

    def conv(x, tail_ref, cw_row, cb_row, sl):
        cw = vec_ref[cw_row:cw_row + CONV_WIDTH, sl]
        parts = []
        for b in range(nb):
            xb = x[rows[b], :]
            parts.append(_causal_conv(xb, tail_ref[b, :, sl], cw, vec(cb_row, sl)))
            tail_ref[b, :, sl] = xb[tr - SUBLANES:, :]
        return jnp.concatenate(parts, axis=0)


    def lru_block(g):
        sl = _blk(g)
        px = proj(LRU_X + g)
        yield
        u = conv(px, ltail_s, V_LRU_CONV_W, V_LRU_CONV_B, sl)
        ub = u.astype(BF16)
        yield
        r_pre = _dot(ub, hw_ref[HW_LRU_A, g]) + vec(V_LRU_B_A, sl)
        i_pre = _dot(ub, hw_ref[HW_LRU_X, g]) + vec(V_LRU_B_X, sl)
        yield
        zh = proj(LRU_Z + g)
        ls_c = (0.5 * LRU_C) * _log_sigmoid(vec(V_LRU_LAMBDA, sl))
        yield
        for b in range(nb):
            h0 = hcar_s[b, :, sl]
            for c in range(tr // ROW_CHUNK):
                rs = slice(b * tr + c * ROW_CHUNK, b * tr + (c + 1) * ROW_CHUNK)
                log_a = ls_c * _tanh1(r_pre[rs, :])
                a = jnp.exp(log_a)
                om = -jnp.tanh(log_a) * (1.0 + a * a)
                mult = jnp.where(om > 0.0, om * lax.rsqrt(om), 0.0)
                bb = mult * _tanh1(i_pre[rs, :]) * u[rs, :]
                h, h0 = _lru_scan(a, bb, h0)
                y_s[0, rs, sl] = (h * (zh[rs, :] * _tanh1(zh[rs, :]))).astype(BF16)
                yield
            hcar_s[b, :, sl] = h0

    def xattn_head(h):
        sl = _blk(h)
        q = proj(XA_Q + h).astype(BF16)
        yield
        scores = [_dot_nt(q[rows[b], :], kv_ref[b, h]) * (HEAD_DIM ** -0.5) for b in range(nb)]
        yield
        probs = []
        for s in scores:
            e = jnp.exp(s - jnp.max(s, axis=-1, keepdims=True))
            probs.append((e * (1.0 / jnp.sum(e, axis=-1, keepdims=True))).astype(BF16))
            yield
        o = jnp.concatenate([_dot(probs[b], kv_ref[b, N_HEADS + h]) for b in range(nb)], axis=0)
        yield
        zh = proj(XA_Z + h)
        yield
        y_s[2, :, sl] = (o * (zh * _tanh1(zh))).astype(BF16)

    def mlstm_front(h):
        sl = _blk(h)
        xm = proj(M_X + h)
        yield
        ch = conv(xm, mtail_s, V_M_CONV_W, V_M_CONV_B, sl)
        xc = ch * _tanh1(ch)
        xc_s[:, sl] = xc
        xcb = xc.astype(BF16)
        xmb = xm.astype(BF16)
        yield
        q_s[:, sl] = _dot(xcb, hw_ref[HW_M_Q, h]).astype(BF16)
        k_s[:, sl] = _dot(xcb, hw_ref[HW_M_K, h]).astype(BF16)
        yield
        v_s[:, sl] = _dot(xmb, hw_ref[HW_M_V, h]).astype(BF16)
        og_s[:, sl] = _tanh1(_dot(xmb, hw_ref[HW_M_O, h]) + vec(V_M_B_O, sl))
        fronts_done.append(h)

    fronts_done = []
    gate_terms = []

    def mlstm_gates():
        while len(fronts_done) < N_HEADS:
            yield
        g_s[...] = (_dot(q_s[...], wif_ref[0:D_MODEL, :])
                    + _dot(k_s[...], wif_ref[D_MODEL:2 * D_MODEL, :])
                    + _dot(v_s[...], wif_ref[2 * D_MODEL:3 * D_MODEL, :])
                    + vec(V_B_IF, slice(0, 2 * LANES)))
        yield
        for b in range(nb):
            bcum = _prefix_rows(_log_sigmoid(g_s[rows[b], LANES:2 * LANES]), jnp.add, 0.0)
            gq = g_s[rows[b], 0:LANES] - bcum
            cmax = _prefix_rows(gq, jnp.maximum, -jnp.inf)
            gate_terms.append((bcum, gq, cmax, gq.T))
            yield

    causal = (lax.broadcasted_iota(jnp.int32, (L, L), 0) >= lax.broadcasted_iota(jnp.int32, (L, L), 1))

    def mlstm_chunk(b, h, terms, out):
        sl = _blk(h)
        bcum, gq, cmax, gq_t = terms
        b_rep = _lane_bcast(bcum, h)
        gq_rep = _lane_bcast(gq, h)
        gq_row = gq_t[h:h + 1, :]
        m_prev = m_s[b, h, 0:1, :]
        mm = jnp.maximum(m_prev, _lane_bcast(cmax, h))
        mm_last = mm[L - 1:L, :]
        p = jnp.exp(jnp.where(causal, gq_row - _lane_tile(mm, L), -jnp.inf))
        sc = jnp.exp(m_prev - mm)
        qh = q_s[rows[b], sl]
        kh = (k_s[rows[b], sl] * (HEAD_DIM ** -0.5)).astype(BF16)
        vh = v_s[rows[b], sl]
        kt = kh.T
        st = st_s[b, h]
        stb = st.astype(BF16)
        yield
        qk = _dot(qh, jnp.concatenate([kt, stb[:, HEAD_DIM:]], axis=1))
        qc = _dot(qh, stb[:, :HEAD_DIM])
        yield
        s = qk[:, :L] * p
        den = jnp.sum(s, axis=-1, keepdims=True) + sc * qk[:, L:]
        inv = 0.5 / jnp.maximum(jnp.abs(den), jnp.exp(-(b_rep + mm)))
        wk = jnp.exp(gq_rep - mm_last)
        wkv = jnp.concatenate([_lane_tile(wk, HEAD_DIM) * vh.astype(F32), wk], axis=1).astype(BF16)
        yield
        num = _dot(s.astype(BF16), vh) + _lane_tile(sc, HEAD_DIM) * qc
        upd = _dot(kt, wkv)
        yield
        decay = sc[L - 1:L, :]
        st_s[b, h] = _lane_tile(decay, HEAD_DIM + LANES) * st + upd
        m_s[b, h] = jnp.broadcast_to(b_rep[L - 1:L, :] + mm_last, (SUBLANES, LANES))
        out[b] = num * _lane_tile(inv, HEAD_DIM)

    def mlstm_head(h, terms):
        sl = _blk(h)
        hparts = [None] * nb
        yield from _zip_stages([mlstm_chunk(b, h, terms[b], hparts) for b in range(nb)])
        hh = og_s[:, sl] * jnp.concatenate(hparts, axis=0)
        mu = jnp.mean(hh, axis=-1, keepdims=True)
        dl = hh - mu
        var = jnp.mean(dl * dl, axis=-1, keepdims=True)
        hn = dl * lax.rsqrt(var + EPS) * vec(V_M_NORM_G, sl)
        yield
        zh = proj(M_Z + h)
        yield
        y = (hn + vec(V_M_SKIP, sl) * xc_s[:, sl]) * (zh * _tanh1(zh))
        y_s[1, :, sl] = y.astype(BF16)

    def branch_out(n, j):
        gate2 = _tanh1(proj(GATE + N_HEADS * n + j))
        yield
        return gate2 * _dot(y_s[n], wbr_s[n, j])

    def branch_pair(j):
        first = yield from branch_out(0, j)
        yield
        second = yield from branch_out(2, j)
        acc_s[:, _blk(j)] = first + second

    def branch_last(j):
        last = yield from branch_out(1, j)
        mg_s[:, _blk(j)] = (acc_s[:, _blk(j)] + last).astype(BF16)

    heads = range(N_HEADS)
    _interleave([f(i) for i in heads for f in (lru_block, mlstm_front, xattn_head)] + [mlstm_gates()])
    _interleave([f(i) for i in heads for f in (lambda i: mlstm_head(i, gate_terms), branch_pair)])
    _interleave([branch_last(j) for j in heads])
    ssq = None
    for j in range(N_HEADS):
        sl = _blk(j)
        xo = x_ref[:, :, sl].reshape(R, HEAD_DIM) + _dot(mg_s[...], wout_s[j])
        o_ref[:, :, sl] = xo.reshape(nb, tr, HEAD_DIM)
        part = jnp.sum(xo * xo, axis=-1, keepdims=True)
        ssq = part if ssq is None else ssq + part
    scale = lax.rsqrt(ssq * (1.0 / D_MODEL) + EPS)
    o_ref[...] = (o_ref[...].reshape(R, D_MODEL) * scale * vec(V_FINAL_G)).reshape(nb, tr, D_MODEL)


def _resident(a):
    nd = a.ndim
    return pl.BlockSpec(a.shape, lambda i: (0,) * nd, pipeline_mode=pl.Buffered(1))


def kernel(x, mem, norm_g, w_in, b_in, lru_conv_w, lru_conv_b, lru_w_a, lru_b_a, lru_w_x, lru_b_x,
           lru_lambda, m_conv_w, m_conv_b, m_w_q, m_w_k, m_w_v, m_w_o, m_b_o, m_w_if, m_b_if,
           m_norm_g, m_skip, mem_norm_g, w_mem_kv, w_branch, w_out, final_norm_g):
    B, S, D = x.shape
    assert D == D_MODEL and S % TIME_TILE == 0 and norm_g.shape[0] == 1
    tr = TIME_TILE
    H = N_HEADS
    row = lambda v: v.reshape(1, -1).astype(F32)

    kv = pl.pallas_call(
        _kv_kernel,
        grid=(2,),
        in_specs=[pl.BlockSpec((B, N_MEM, D), lambda c: (0, 0, 0)),
                  pl.BlockSpec((1, D), lambda c: (0, 0)),
                  pl.BlockSpec((None, D, D), lambda c: (0, 0, c))],
        out_specs=pl.BlockSpec((B, H, N_MEM, HEAD_DIM), lambda c: (0, c, 0, 0)),
        out_shape=jax.ShapeDtypeStruct((B, 2 * H, N_MEM, HEAD_DIM), BF16),
        compiler_params=pltpu.CompilerParams(vmem_limit_bytes=VMEM_LIMIT_BYTES),
        name="mem_kv",
    )(mem, row(mem_norm_g[0]), w_mem_kv)

    pad_lanes = lambda a: jnp.pad(a, ((0, 0), (0, LANES - H)))
    w_if = jnp.concatenate([pad_lanes(m_w_if[0][:, :H]), pad_lanes(m_w_if[0][:, H:])], axis=1).astype(BF16)
    b_if2 = m_b_if[0].reshape(1, 2 * H).astype(F32)
    b_if = jnp.pad(jnp.concatenate([pad_lanes(b_if2[:, :H]), pad_lanes(b_if2[:, H:])], axis=1),
                   ((0, 0), (0, D - 2 * LANES)))
    vecs = jnp.concatenate(
        [row(norm_g[0]), row(0.5 * lru_conv_b[0]), row(0.5 * lru_b_a[0]), row(0.5 * lru_b_x[0]),
         row(lru_lambda[0]), row(0.5 * m_conv_b[0]), row(0.5 * m_b_o[0]), row(m_norm_g[0]), row(m_skip[0]),
         row(final_norm_g), b_if, 0.5 * lru_conv_w[0], 0.5 * m_conv_w[0],
         (b_in[0].reshape(IN_GROUPS, D) * jnp.asarray(IN_SCALE, F32)[:, None])], axis=0).astype(F32)
    head_w = (jnp.stack([lru_w_a[0], lru_w_x[0], m_w_q[0], m_w_k[0], m_w_v[0], m_w_o[0]])
              * jnp.asarray([1.0, 1.0, 1.0, 1.0, 1.0, 0.5], F32)[:, None, None, None]).astype(BF16)
    weights = [vecs, head_w, w_if, kv]
    hbm_weights = [w_in, w_branch, w_out]

    R = B * tr
    tok = pl.BlockSpec((B, tr, D), lambda i: (0, i, 0))
    nblk = D // MXU_COLS
    scratch = [pltpu.VMEM((IN_GROUPS * nblk, D, MXU_COLS), BF16),
               pltpu.VMEM((N_BRANCH, nblk, D, MXU_COLS), BF16),
               pltpu.VMEM((nblk, D, MXU_COLS), BF16),
               pltpu.VMEM((2, D, STAGE_COLS), F32),
               pltpu.SemaphoreType.DMA((2,)),
               pltpu.VMEM((R, D), BF16),
               pltpu.VMEM((B, SUBLANES, D), F32),
               pltpu.VMEM((B, SUBLANES, D), F32),
               pltpu.VMEM((B, SUBLANES, D), F32),
               pltpu.VMEM((R, D), BF16),
               pltpu.VMEM((R, D), BF16),
               pltpu.VMEM((R, D), BF16),
               pltpu.VMEM((R, D), F32),
               pltpu.VMEM((R, D), F32),
               pltpu.VMEM((R, 2 * LANES), F32),
               pltpu.VMEM((B, H, HEAD_DIM, HEAD_DIM + LANES), F32),
               pltpu.VMEM((B, H, SUBLANES, LANES), F32),
               pltpu.VMEM((N_BRANCH, R, D), BF16),
               pltpu.VMEM((R, D), F32),
               pltpu.VMEM((R, D), BF16)]
    return pl.pallas_call(
        functools.partial(_layer_kernel, nb=B, tr=tr),
        grid=(S // tr,),
        in_specs=([tok] + [_resident(w) for w in weights]
                  + [pl.BlockSpec(memory_space=pl.ANY)] * len(hbm_weights)),
        out_specs=tok,
        out_shape=jax.ShapeDtypeStruct((B, S, D), F32),
        scratch_shapes=scratch,
        compiler_params=pltpu.CompilerParams(
            dimension_semantics=("arbitrary",),
            vmem_limit_bytes=VMEM_LIMIT_BYTES),
        name="hybrid_layer",
    )(x, *weights, *hbm_weights)
```

```python
import functools

import jax
import jax.numpy as jnp
from jax import lax
from jax.experimental import pallas as pl
from jax.experimental.pallas import tpu as pltpu

D_MODEL = 1024
N_HEADS = 4
HEAD_DIM = D_MODEL // N_HEADS
N_MEM = 256
N_BRANCH = 3
EPS = 1e-6
CONV_WIDTH = 4
LRU_C = 8.0
LANES = 128
SUBLANES = 8
MXU_COLS = 256
TIME_TILE = 128
ROW_CHUNK = 32
VMEM_LIMIT_BYTES = 60 * 1024 * 1024

LRU_X, LRU_Z, M_X, M_Z, XA_Q, XA_Z, GATE = 0, 4, 8, 12, 16, 20, 24
IN_GROUPS = 9
IN_SCALE = (1.0, 0.5, 1.0, 0.5, 1.0, 0.5, 0.5, 0.5, 0.5)
BRANCH_SCALE = 0.5
STAGE_COLS = 2 * MXU_COLS
(V_NORM_G, V_LRU_CONV_B, V_LRU_B_A, V_LRU_B_X, V_LRU_LAMBDA, V_M_CONV_B, V_M_B_O, V_M_NORM_G, V_M_SKIP,
 V_FINAL_G, V_B_IF) = range(11)
V_LRU_CONV_W = 11
V_M_CONV_W = V_LRU_CONV_W + CONV_WIDTH
V_B_IN = V_M_CONV_W + CONV_WIDTH
N_VEC_ROWS = V_B_IN + IN_GROUPS
HW_LRU_A, HW_LRU_X, HW_M_Q, HW_M_K, HW_M_V, HW_M_O = range(6)
HEAD_W_SCALE = (1.0, 1.0, 1.0, 1.0, 1.0, 0.5)
ROW_INPUTS = ((V_NORM_G, 1.0), (V_LRU_CONV_B, 0.5), (V_LRU_B_A, 0.5), (V_LRU_B_X, 0.5), (V_LRU_LAMBDA, 1.0),
              (V_M_CONV_B, 0.5), (V_M_B_O, 0.5), (V_M_NORM_G, 1.0), (V_M_SKIP, 1.0), (V_FINAL_G, 1.0),
              (V_B_IF, 1.0))
CONV_SCALE = 0.5

BF16 = jnp.bfloat16
F32 = jnp.float32


def _dot(a, b):
    return jnp.dot(a, b, preferred_element_type=F32)


def _dot_nt(a, b):
    return lax.dot_general(a, b, (((1,), (1,)), ((), ())), preferred_element_type=F32)


def _tanh1(v):
    return jnp.tanh(v) + 1.0


def _log_sigmoid(x):
    return jnp.minimum(x, 0.0) - jnp.log1p(jnp.exp(-jnp.abs(x)))


def _rms_norm(x, g):
    ms = jnp.mean(x * x, axis=-1, keepdims=True)
    return x * lax.rsqrt(ms + EPS) * g


def _blk(j):
    return slice(j * HEAD_DIM, (j + 1) * HEAD_DIM)


def _prefix_rows(x, op, identity):
    n = x.shape[0]
    row = lax.broadcasted_iota(jnp.int32, x.shape, 0)
    k = 1
    while k < n:
        x = op(x, jnp.where(row >= k, pltpu.roll(x, k, 0), identity))
        k *= 2
    return x


def _lane_tile(x, n):
    reps = n // LANES
    return x if reps == 1 else jnp.concatenate([x] * reps, axis=-1)


def _lane_bcast(x, lane):
    return jnp.broadcast_to(x[:, lane:lane + 1], x.shape)


def _causal_conv(xb, tail, cw, cb):
    t, c = xb.shape
    groups = t // SUBLANES
    full = jnp.concatenate([tail, xb], axis=0).reshape(groups + 1, SUBLANES, c)
    sub = lax.broadcasted_iota(jnp.int32, (groups, SUBLANES, c), 1)
    acc = cb + cw[CONV_WIDTH - 1:CONV_WIDTH, :] * xb
    for k in range(1, CONV_WIDTH):
        r = pltpu.roll(full, k, 1)
        shifted = jnp.where(sub < k, r[:-1], r[1:]).reshape(t, c)
        acc = acc + cw[CONV_WIDTH - 1 - k:CONV_WIDTH - k, :] * shifted
    return acc


def _lru_scan(a, b, h0):
    t, c = a.shape
    row = lax.broadcasted_iota(jnp.int32, (SUBLANES, c), 0)
    out = []
    for j in range(t // SUBLANES):
        aj = a[j * SUBLANES:(j + 1) * SUBLANES, :]
        bj = b[j * SUBLANES:(j + 1) * SUBLANES, :]
        for k in (1, 2, 4):
            valid = row >= k
            a_sh = pltpu.roll(aj, k, 0)
            b_sh = pltpu.roll(bj, k, 0)
            bj = bj + aj * jnp.where(valid, b_sh, 0.0)
            aj = aj * jnp.where(valid, a_sh, 1.0)
        hj = bj + aj * h0
        out.append(hj)
        h0 = jnp.broadcast_to(hj[SUBLANES - 1:SUBLANES, :], (SUBLANES, c))
    return jnp.concatenate(out, axis=0), h0


def _zip_stages(gens):
    gens = list(gens)
    while gens:
        alive = []
        for g in gens:
            try:
                next(g)
                alive.append(g)
            except StopIteration:
                pass
        gens = alive
        if gens:
            yield


def _interleave(gens, skew=1):
    pending = list(gens)
    active = []
    rnd = 0
    while pending or active:
        while pending and (len(gens) - len(pending)) * skew <= rnd:
            active.append(pending.pop(0))
        alive = []
        for g in active:
            try:
                next(g)
                alive.append(g)
            except StopIteration:
                pass
        active = alive
        rnd += 1


def _scaled(v, scale):
    return v if scale == 1.0 else v * scale


def _prep_kernel(mem_ref, g_ref, w_ref,
                 r0, r1, r2, r3, r4, r5, r6, r7, r8, r9, r10, lcw_ref, mcw_ref, bin_ref,
                 h0, h1, h2, h3, h4, h5,
                 kv_ref, vec_ref, hw_ref):
    nb = mem_ref.shape[0]
    mn = _rms_norm(mem_ref[...].reshape(nb * N_MEM, D_MODEL), g_ref[...]).astype(BF16)
    w = w_ref[...].astype(BF16)
    for j in range(N_HEADS):
        kv_ref[:, j] = _dot(mn, w[:, _blk(j)]).astype(BF16).reshape(nb, N_MEM, HEAD_DIM)

    @pl.when(pl.program_id(0) == 0)
    def _():
        for (row, scale), ref in zip(ROW_INPUTS, (r0, r1, r2, r3, r4, r5, r6, r7, r8, r9, r10)):
            vec_ref[row:row + 1, :] = _scaled(ref[...], scale)
        vec_ref[V_LRU_CONV_W:V_LRU_CONV_W + CONV_WIDTH, :] = lcw_ref[0] * CONV_SCALE
        vec_ref[V_M_CONV_W:V_M_CONV_W + CONV_WIDTH, :] = mcw_ref[0] * CONV_SCALE
        for g in range(IN_GROUPS):
            vec_ref[V_B_IN + g:V_B_IN + g + 1, :] = _scaled(bin_ref[g:g + 1, :], IN_SCALE[g])
        for i, ref in enumerate((h0, h1, h2, h3, h4, h5)):
            hw_ref[i] = _scaled(ref[0], HEAD_W_SCALE[i]).astype(BF16)


def _load_weights(win_hbm, wbr_hbm, wout_hbm, win_s, wbr_s, wout_s, stage_s, sem):
    per = STAGE_COLS // MXU_COLS
    chunks = []
    for c in range(win_hbm.shape[-1] // STAGE_COLS):
        chunks.append((win_hbm.at[0, :, pl.ds(c * STAGE_COLS, STAGE_COLS)], win_s, c * per,
                       IN_SCALE[c * STAGE_COLS // D_MODEL]))
    for n in range(N_BRANCH):
        for c in range(D_MODEL // STAGE_COLS):
            chunks.append((wbr_hbm.at[0, n, :, pl.ds(c * STAGE_COLS, STAGE_COLS)], wbr_s.at[n], c * per,
                           BRANCH_SCALE))
    for c in range(D_MODEL // STAGE_COLS):
        chunks.append((wout_hbm.at[0, :, pl.ds(c * STAGE_COLS, STAGE_COLS)], wout_s, c * per, 1.0))

    def copy(i):
        return pltpu.make_async_copy(chunks[i][0], stage_s.at[i % 2], sem.at[i % 2])

    copy(0).start()
    for i, (_, dest, blk0, scale) in enumerate(chunks):
        if i + 1 < len(chunks):
            copy(i + 1).start()
        copy(i).wait()
        for j in range(per):
            w = stage_s[i % 2, :, j * MXU_COLS:(j + 1) * MXU_COLS]
            dest[blk0 + j] = (w if scale == 1.0 else w * scale).astype(BF16)


def _layer_kernel(x_ref, vec_ref, hw_ref, wif_ref, kv_ref, win_hbm, wbr_hbm, wout_hbm,
                  o_ref,
                  win_s, wbr_s, wout_s, stage_s, dma_sem,
                  hb_s, ltail_s, mtail_s, hcar_s, q_s, k_s, v_s, og_s, xc_s, g_s, st_s, m_s, y_s, acc_s, mg_s,
                  *, nb, tr):
    L = tr
    R = nb * tr
    rows = [slice(b * tr, (b + 1) * tr) for b in range(nb)]

    @pl.when(pl.program_id(0) == 0)
    def _():
        ltail_s[...] = jnp.zeros(ltail_s.shape, F32)
        mtail_s[...] = jnp.zeros(mtail_s.shape, F32)
        hcar_s[...] = jnp.zeros(hcar_s.shape, F32)
        st_s[...] = jnp.zeros(st_s.shape, F32)
        m_s[...] = jnp.zeros(m_s.shape, F32)
        _load_weights(win_hbm, wbr_hbm, wout_hbm, win_s, wbr_s, wout_s, stage_s, dma_sem)

    def vec(i, sl=slice(None)):
        return vec_ref[i:i + 1, sl]

    hb_s[...] = _rms_norm(x_ref[...].reshape(R, D_MODEL), vec(V_NORM_G)).astype(BF16)

    def proj(j):
        return _dot(hb_s[...], win_s[j]) + vec(V_B_IN + j // N_HEADS, _blk(j % N_HEADS))

    def conv(x, tail_ref, cw_row, cb_row, sl):
        cw = vec_ref[cw_row:cw_row + CONV_WIDTH, sl]
        parts = []
        for b in range(nb):
            xb = x[rows[b], :]
            parts.append(_causal_conv(xb, tail_ref[b, :, sl], cw, vec(cb_row, sl)))
            tail_ref[b, :, sl] = xb[tr - SUBLANES:, :]
        return jnp.concatenate(parts, axis=0)


    def lru_block(g):
        sl = _blk(g)
        px = proj(LRU_X + g)
        yield
        u = conv(px, ltail_s, V_LRU_CONV_W, V_LRU_CONV_B, sl)
        ub = u.astype(BF16)
        yield
        r_pre = _dot(ub, hw_ref[HW_LRU_A, g]) + vec(V_LRU_B_A, sl)
        i_pre = _dot(ub, hw_ref[HW_LRU_X, g]) + vec(V_LRU_B_X, sl)
        yield
        zh = proj(LRU_Z + g)
        ls_c = (0.5 * LRU_C) * _log_sigmoid(vec(V_LRU_LAMBDA, sl))
        yield
        for b in range(nb):
            h0 = hcar_s[b, :, sl]
            for c in range(tr // ROW_CHUNK):
                rs = slice(b * tr + c * ROW_CHUNK, b * tr + (c + 1) * ROW_CHUNK)
                log_a = ls_c * _tanh1(r_pre[rs, :])
                a = jnp.exp(log_a)
                om = -jnp.tanh(log_a) * (1.0 + a * a)
                mult = jnp.where(om > 0.0, om * lax.rsqrt(om), 0.0)
                bb = mult * _tanh1(i_pre[rs, :]) * u[rs, :]
                h, h0 = _lru_scan(a, bb, h0)
                y_s[0, rs, sl] = (h * (zh[rs, :] * _tanh1(zh[rs, :]))).astype(BF16)
                yield
            hcar_s[b, :, sl] = h0

    def xattn_head(h):
        sl = _blk(h)
        q = proj(XA_Q + h).astype(BF16)
        yield
        scores = [_dot_nt(q[rows[b], :], kv_ref[b, h]) * (HEAD_DIM ** -0.5) for b in range(nb)]
        yield
        probs = []
        for s in scores:
            e = jnp.exp(s - jnp.max(s, axis=-1, keepdims=True))
            probs.append((e * (1.0 / jnp.sum(e, axis=-1, keepdims=True))).astype(BF16))
            yield
        o = jnp.concatenate([_dot(probs[b], kv_ref[b, N_HEADS + h]) for b in range(nb)], axis=0)
        yield
        zh = proj(XA_Z + h)
        yield
        y_s[2, :, sl] = (o * (zh * _tanh1(zh))).astype(BF16)

    def mlstm_front(h):
        sl = _blk(h)
        xm = proj(M_X + h)
        yield
        ch = conv(xm, mtail_s, V_M_CONV_W, V_M_CONV_B, sl)
        xc = ch * _tanh1(ch)
        xc_s[:, sl] = xc
        xcb = xc.astype(BF16)
        xmb = xm.astype(BF16)
        yield
        q_s[:, sl] = _dot(xcb, hw_ref[HW_M_Q, h]).astype(BF16)
        k_s[:, sl] = _dot(xcb, hw_ref[HW_M_K, h]).astype(BF16)
        yield
        v_s[:, sl] = _dot(xmb, hw_ref[HW_M_V, h]).astype(BF16)
        og_s[:, sl] = _tanh1(_dot(xmb, hw_ref[HW_M_O, h]) + vec(V_M_B_O, sl))
        fronts_done.append(h)

    fronts_done = []
    gate_terms = []

    def mlstm_gates():
        while len(fronts_done) < N_HEADS:
            yield
        g_s[...] = (_dot(q_s[...], wif_ref[0:D_MODEL, :])
                    + _dot(k_s[...], wif_ref[D_MODEL:2 * D_MODEL, :])
                    + _dot(v_s[...], wif_ref[2 * D_MODEL:3 * D_MODEL, :])
                    + vec(V_B_IF, slice(0, 2 * LANES)))
        yield
        for b in range(nb):
            bcum = _prefix_rows(_log_sigmoid(g_s[rows[b], LANES:2 * LANES]), jnp.add, 0.0)
            gq = g_s[rows[b], 0:LANES] - bcum
            cmax = _prefix_rows(gq, jnp.maximum, -jnp.inf)
            gate_terms.append((bcum, gq, cmax, gq.T))
            yield

    causal = (lax.broadcasted_iota(jnp.int32, (L, L), 0) >= lax.broadcasted_iota(jnp.int32, (L, L), 1))

    def mlstm_chunk(b, h, terms, out):
        sl = _blk(h)
        bcum, gq, cmax, gq_t = terms
        b_rep = _lane_bcast(bcum, h)
        gq_rep = _lane_bcast(gq, h)
        gq_row = gq_t[h:h + 1, :]
        m_prev = m_s[b, h, 0:1, :]
        mm = jnp.maximum(m_prev, _lane_bcast(cmax, h))
        mm_last = mm[L - 1:L, :]
        p = jnp.exp(jnp.where(causal, gq_row - _lane_tile(mm, L), -jnp.inf))
        sc = jnp.exp(m_prev - mm)
        qh = q_s[rows[b], sl]
        kh = (k_s[rows[b], sl] * (HEAD_DIM ** -0.5)).astype(BF16)
        vh = v_s[rows[b], sl]
        kt = kh.T
        st = st_s[b, h]
        stb = st.astype(BF16)
        yield
        qk = _dot(qh, jnp.concatenate([kt, stb[:, HEAD_DIM:]], axis=1))
        qc = _dot(qh, stb[:, :HEAD_DIM])
        yield
        s = qk[:, :L] * p
        den = jnp.sum(s, axis=-1, keepdims=True) + sc * qk[:, L:]
        inv = 0.5 / jnp.maximum(jnp.abs(den), jnp.exp(-(b_rep + mm)))
        wk = jnp.exp(gq_rep - mm_last)
        wkv = jnp.concatenate([_lane_tile(wk, HEAD_DIM) * vh.astype(F32), wk], axis=1).astype(BF16)
        yield
        num = _dot(s.astype(BF16), vh) + _lane_tile(sc, HEAD_DIM) * qc
        upd = _dot(kt, wkv)
        yield
        decay = sc[L - 1:L, :]
        st_s[b, h] = _lane_tile(decay, HEAD_DIM + LANES) * st + upd
        m_s[b, h] = jnp.broadcast_to(b_rep[L - 1:L, :] + mm_last, (SUBLANES, LANES))
        out[b] = num * _lane_tile(inv, HEAD_DIM)

    def mlstm_head(h, terms):
        sl = _blk(h)
        hparts = [None] * nb
        yield from _zip_stages([mlstm_chunk(b, h, terms[b], hparts) for b in range(nb)])
        hh = og_s[:, sl] * jnp.concatenate(hparts, axis=0)
        mu = jnp.mean(hh, axis=-1, keepdims=True)
        dl = hh - mu
        var = jnp.mean(dl * dl, axis=-1, keepdims=True)
        hn = dl * lax.rsqrt(var + EPS) * vec(V_M_NORM_G, sl)
        yield
        zh = proj(M_Z + h)
        yield
        y = (hn + vec(V_M_SKIP, sl) * xc_s[:, sl]) * (zh * _tanh1(zh))
        y_s[1, :, sl] = y.astype(BF16)

    def branch_out(n, j):
        gate2 = _tanh1(proj(GATE + N_HEADS * n + j))
        yield
        return gate2 * _dot(y_s[n], wbr_s[n, j])

    def branch_pair(j):
        first = yield from branch_out(0, j)
        yield
        second = yield from branch_out(2, j)
        acc_s[:, _blk(j)] = first + second

    def branch_last(j):
        last = yield from branch_out(1, j)
        mg_s[:, _blk(j)] = (acc_s[:, _blk(j)] + last).astype(BF16)

    heads = range(N_HEADS)
    _interleave([f(i) for i in heads for f in (lru_block, mlstm_front, xattn_head)] + [mlstm_gates()])
    _interleave([f(i) for i in heads for f in (lambda i: mlstm_head(i, gate_terms), branch_pair)])
    _interleave([branch_last(j) for j in heads])
    ssq = None
    for j in range(N_HEADS):
        sl = _blk(j)
        xo = x_ref[:, :, sl].reshape(R, HEAD_DIM) + _dot(mg_s[...], wout_s[j])
        o_ref[:, :, sl] = xo.reshape(nb, tr, HEAD_DIM)
        part = jnp.sum(xo * xo, axis=-1, keepdims=True)
        ssq = part if ssq is None else ssq + part
    scale = lax.rsqrt(ssq * (1.0 / D_MODEL) + EPS)
    o_ref[...] = (o_ref[...].reshape(R, D_MODEL) * scale * vec(V_FINAL_G)).reshape(nb, tr, D_MODEL)


def _resident(a):
    nd = a.ndim
    return pl.BlockSpec(a.shape, lambda i: (0,) * nd, pipeline_mode=pl.Buffered(1))


def kernel(x, mem, norm_g, w_in, b_in, lru_conv_w, lru_conv_b, lru_w_a, lru_b_a, lru_w_x, lru_b_x,
           lru_lambda, m_conv_w, m_conv_b, m_w_q, m_w_k, m_w_v, m_w_o, m_b_o, m_w_if, m_b_if,
           m_norm_g, m_skip, mem_norm_g, w_mem_kv, w_branch, w_out, final_norm_g):
    B, S, D = x.shape
    assert D == D_MODEL and S % TIME_TILE == 0 and norm_g.shape[0] == 1
    tr = TIME_TILE
    H = N_HEADS
    pad_lanes = lambda a: jnp.pad(a, ((0, 0), (0, LANES - H)))
    w_if = jnp.concatenate([pad_lanes(m_w_if[0][:, :H]), pad_lanes(m_w_if[0][:, H:])], axis=1).astype(BF16)
    b_if2 = m_b_if[0].reshape(1, 2 * H).astype(F32)
    b_if = jnp.pad(jnp.concatenate([pad_lanes(b_if2[:, :H]), pad_lanes(b_if2[:, H:])], axis=1),
                   ((0, 0), (0, D - 2 * LANES)))

    rows_in = [norm_g, lru_conv_b, lru_b_a, lru_b_x, lru_lambda, m_conv_b, m_b_o, m_norm_g, m_skip,
               final_norm_g.reshape(1, D), b_if]
    prep_in = rows_in + [lru_conv_w, m_conv_w, b_in.reshape(IN_GROUPS, D),
                         lru_w_a, lru_w_x, m_w_q, m_w_k, m_w_v, m_w_o]
    whole = lambda a: pl.BlockSpec(a.shape, lambda c: (0,) * a.ndim)
    kv, vecs, head_w = pl.pallas_call(
        _prep_kernel,
        grid=(2,),
        in_specs=[whole(mem), whole(mem_norm_g), pl.BlockSpec((None, D, D), lambda c: (0, 0, c))]
        + [whole(a) for a in prep_in],
        out_specs=[pl.BlockSpec((B, H, N_MEM, HEAD_DIM), lambda c: (0, c, 0, 0)),
                   pl.BlockSpec((N_VEC_ROWS, D), lambda c: (0, 0)),
                   pl.BlockSpec((len(HEAD_W_SCALE), H, HEAD_DIM, HEAD_DIM), lambda c: (0, 0, 0, 0))],
        out_shape=[jax.ShapeDtypeStruct((B, 2 * H, N_MEM, HEAD_DIM), BF16),
                   jax.ShapeDtypeStruct((N_VEC_ROWS, D), F32),
                   jax.ShapeDtypeStruct((len(HEAD_W_SCALE), H, HEAD_DIM, HEAD_DIM), BF16)],
        compiler_params=pltpu.CompilerParams(dimension_semantics=("arbitrary",),
                                             vmem_limit_bytes=VMEM_LIMIT_BYTES),
        name="prep",
    )(mem, mem_norm_g, w_mem_kv, *prep_in)
    weights = [vecs, head_w, w_if, kv]
    hbm_weights = [w_in, w_branch, w_out]

    R = B * tr
    tok = pl.BlockSpec((B, tr, D), lambda i: (0, i, 0))
    nblk = D // MXU_COLS
    scratch = [pltpu.VMEM((IN_GROUPS * nblk, D, MXU_COLS), BF16),
               pltpu.VMEM((N_BRANCH, nblk, D, MXU_COLS), BF16),
               pltpu.VMEM((nblk, D, MXU_COLS), BF16),
               pltpu.VMEM((2, D, STAGE_COLS), F32),
               pltpu.SemaphoreType.DMA((2,)),
               pltpu.VMEM((R, D), BF16),
               pltpu.VMEM((B, SUBLANES, D), F32),
               pltpu.VMEM((B, SUBLANES, D), F32),
               pltpu.VMEM((B, SUBLANES, D), F32),
               pltpu.VMEM((R, D), BF16),
               pltpu.VMEM((R, D), BF16),
               pltpu.VMEM((R, D), BF16),
               pltpu.VMEM((R, D), F32),
               pltpu.VMEM((R, D), F32),
               pltpu.VMEM((R, 2 * LANES), F32),
               pltpu.VMEM((B, H, HEAD_DIM, HEAD_DIM + LANES), F32),
               pltpu.VMEM((B, H, SUBLANES, LANES), F32),
               pltpu.VMEM((N_BRANCH, R, D), BF16),
               pltpu.VMEM((R, D), F32),
               pltpu.VMEM((R, D), BF16)]
    return pl.pallas_call(
        functools.partial(_layer_kernel, nb=B, tr=tr),
        grid=(S // tr,),
        in_specs=([tok] + [_resident(w) for w in weights]
                  + [pl.BlockSpec(memory_space=pl.ANY)] * len(hbm_weights)),
        out_specs=tok,
        out_shape=jax.ShapeDtypeStruct((B, S, D), F32),
        scratch_shapes=scratch,
        compiler_params=pltpu.CompilerParams(
            dimension_semantics=("arbitrary",),
            vmem_limit_bytes=VMEM_LIMIT_BYTES),
        name="hybrid_layer",
    )(x, *weights, *hbm_weights)
```

```python
import functools

import jax
import jax.numpy as jnp
from jax import lax
from jax.experimental import pallas as pl
from jax.experimental.pallas import tpu as pltpu

D_MODEL = 1024
N_HEADS = 4
HEAD_DIM = D_MODEL // N_HEADS
N_MEM = 256
N_BRANCH = 3
EPS = 1e-6
CONV_WIDTH = 4
LRU_C = 8.0
LANES = 128
SUBLANES = 8
MXU_COLS = 256
TIME_TILE = 128
ROW_CHUNK = 32
VMEM_LIMIT_BYTES = 60 * 1024 * 1024

LRU_X, LRU_Z, M_X, M_Z, XA_Q, XA_Z, GATE = 0, 4, 8, 12, 16, 20, 24
IN_GROUPS = 9
IN_SCALE = (1.0, 0.5, 1.0, 0.5, 1.0, 0.5, 0.5, 0.5, 0.5)
BRANCH_SCALE = 0.5
STAGE_COLS = 2 * MXU_COLS
(V_NORM_G, V_LRU_CONV_B, V_LRU_B_A, V_LRU_B_X, V_LRU_LAMBDA, V_M_CONV_B, V_M_B_O, V_M_NORM_G, V_M_SKIP,
 V_FINAL_G, V_B_IF) = range(11)
V_LRU_CONV_W = 11
V_M_CONV_W = V_LRU_CONV_W + CONV_WIDTH
V_B_IN = V_M_CONV_W + CONV_WIDTH
N_VEC_ROWS = V_B_IN + IN_GROUPS
HW_LRU_A, HW_LRU_X, HW_M_Q, HW_M_K, HW_M_V, HW_M_O = range(6)
HEAD_W_SCALE = (1.0, 1.0, 1.0, 1.0, 1.0, 0.5)
ROW_INPUTS = ((V_NORM_G, 1.0), (V_LRU_CONV_B, 0.5), (V_LRU_B_A, 0.5), (V_LRU_B_X, 0.5), (V_LRU_LAMBDA, 1.0),
              (V_M_CONV_B, 0.5), (V_M_B_O, 0.5), (V_M_NORM_G, 1.0), (V_M_SKIP, 1.0), (V_FINAL_G, 1.0),
              (V_B_IF, 1.0))
CONV_SCALE = 0.5

BF16 = jnp.bfloat16
F32 = jnp.float32


def _dot(a, b):
    return jnp.dot(a, b, preferred_element_type=F32)


def _dot_nt(a, b):
    return lax.dot_general(a, b, (((1,), (1,)), ((), ())), preferred_element_type=F32)


def _tanh1(v):
    return jnp.tanh(v) + 1.0


def _log_sigmoid(x):
    return jnp.minimum(x, 0.0) - jnp.log1p(jnp.exp(-jnp.abs(x)))


def _rms_norm(x, g):
    ms = jnp.mean(x * x, axis=-1, keepdims=True)
    return x * lax.rsqrt(ms + EPS) * g


def _blk(j):
    return slice(j * HEAD_DIM, (j + 1) * HEAD_DIM)


def _prefix_rows(x, op, identity):
    n = x.shape[0]
    row = lax.broadcasted_iota(jnp.int32, x.shape, 0)
    k = 1
    while k < n:
        x = op(x, jnp.where(row >= k, pltpu.roll(x, k, 0), identity))
        k *= 2
    return x


def _lane_tile(x, n):
    reps = n // LANES
    return x if reps == 1 else jnp.concatenate([x] * reps, axis=-1)


def _lane_bcast(x, lane):
    return jnp.broadcast_to(x[:, lane:lane + 1], x.shape)


def _shift_rows(x3, k):
    sub = lax.broadcasted_iota(jnp.int32, (x3.shape[0] - 1,) + x3.shape[1:], 1)
    r = pltpu.roll(x3, k, 1)
    return jnp.where(sub < k, r[:-1], r[1:])


def _causal_conv(xb, tail, cw, cb):
    assert CONV_WIDTH == 4
    t, c = xb.shape
    groups = t // SUBLANES
    full = jnp.concatenate([tail, xb], axis=0).reshape(groups + 1, SUBLANES, c)
    w0, w1, w2, w3 = (cw[j:j + 1, :] for j in range(CONV_WIDTH))
    prev = _shift_rows(full, 1)
    tail3 = full[:1]
    tail_prev = pltpu.roll(tail3, 1, 1)
    older = jnp.concatenate([w1 * tail3 + w0 * tail_prev, w1 * full[1:] + w0 * prev], axis=0)
    acc = cb + w3 * full[1:] + w2 * prev + _shift_rows(older, 2)
    return acc.reshape(t, c)


def _lru_scan(a, b, h0):
    t, c = a.shape
    row = lax.broadcasted_iota(jnp.int32, (SUBLANES, c), 0)
    out = []
    for j in range(t // SUBLANES):
        aj = a[j * SUBLANES:(j + 1) * SUBLANES, :]
        bj = b[j * SUBLANES:(j + 1) * SUBLANES, :]
        for k in (1, 2, 4):
            valid = row >= k
            a_sh = pltpu.roll(aj, k, 0)
            b_sh = pltpu.roll(bj, k, 0)
            bj = bj + aj * jnp.where(valid, b_sh, 0.0)
            aj = aj * jnp.where(valid, a_sh, 1.0)
        hj = bj + aj * h0
        out.append(hj)
        h0 = jnp.broadcast_to(hj[SUBLANES - 1:SUBLANES, :], (SUBLANES, c))
    return jnp.concatenate(out, axis=0), h0


def _zip_stages(gens):
    gens = list(gens)
    while gens:
        alive = []
        for g in gens:
            try:
                next(g)
                alive.append(g)
            except StopIteration:
                pass
        gens = alive
        if gens:
            yield


def _interleave(gens, skew=1):
    pending = list(gens)
    active = []
    rnd = 0
    while pending or active:
        while pending and (len(gens) - len(pending)) * skew <= rnd:
            active.append(pending.pop(0))
        alive = []
        for g in active:
            try:
                next(g)
                alive.append(g)
            except StopIteration:
                pass
        active = alive
        rnd += 1


def _scaled(v, scale):
    return v if scale == 1.0 else v * scale


def _prep_kernel(mem_ref, g_ref, w_ref,
                 r0, r1, r2, r3, r4, r5, r6, r7, r8, r9, r10, lcw_ref, mcw_ref, bin_ref,
                 h0, h1, h2, h3, h4, h5,
                 kv_ref, vec_ref, hw_ref):
    nb = mem_ref.shape[0]
    mn = _rms_norm(mem_ref[...].reshape(nb * N_MEM, D_MODEL), g_ref[...]).astype(BF16)
    w = w_ref[...].astype(BF16)
    for j in range(N_HEADS):
        kv_ref[:, j] = _dot(mn, w[:, _blk(j)]).astype(BF16).reshape(nb, N_MEM, HEAD_DIM)

    @pl.when(pl.program_id(0) == 0)
    def _():
        for (row, scale), ref in zip(ROW_INPUTS, (r0, r1, r2, r3, r4, r5, r6, r7, r8, r9, r10)):
            vec_ref[row:row + 1, :] = _scaled(ref[...], scale)
        vec_ref[V_LRU_CONV_W:V_LRU_CONV_W + CONV_WIDTH, :] = lcw_ref[0] * CONV_SCALE
        vec_ref[V_M_CONV_W:V_M_CONV_W + CONV_WIDTH, :] = mcw_ref[0] * CONV_SCALE
        for g in range(IN_GROUPS):
            vec_ref[V_B_IN + g:V_B_IN + g + 1, :] = _scaled(bin_ref[g:g + 1, :], IN_SCALE[g])
        for i, ref in enumerate((h0, h1, h2, h3, h4, h5)):
            hw_ref[i] = _scaled(ref[0], HEAD_W_SCALE[i]).astype(BF16)


def _load_weights(win_hbm, wbr_hbm, wout_hbm, win_s, wbr_s, wout_s, stage_s, sem):
    per = STAGE_COLS // MXU_COLS
    chunks = []
    for c in range(win_hbm.shape[-1] // STAGE_COLS):
        chunks.append((win_hbm.at[0, :, pl.ds(c * STAGE_COLS, STAGE_COLS)], win_s, c * per,
                       IN_SCALE[c * STAGE_COLS // D_MODEL]))
    for n in range(N_BRANCH):
        for c in range(D_MODEL // STAGE_COLS):
            chunks.append((wbr_hbm.at[0, n, :, pl.ds(c * STAGE_COLS, STAGE_COLS)], wbr_s.at[n], c * per,
                           BRANCH_SCALE))
    for c in range(D_MODEL // STAGE_COLS):
        chunks.append((wout_hbm.at[0, :, pl.ds(c * STAGE_COLS, STAGE_COLS)], wout_s, c * per, 1.0))

    def copy(i):
        return pltpu.make_async_copy(chunks[i][0], stage_s.at[i % 2], sem.at[i % 2])

    copy(0).start()
    for i, (_, dest, blk0, scale) in enumerate(chunks):
        if i + 1 < len(chunks):
            copy(i + 1).start()
        copy(i).wait()
        for j in range(per):
            w = stage_s[i % 2, :, j * MXU_COLS:(j + 1) * MXU_COLS]
            dest[blk0 + j] = (w if scale == 1.0 else w * scale).astype(BF16)


def _layer_kernel(x_ref, vec_ref, hw_ref, wif_ref, kv_ref, win_hbm, wbr_hbm, wout_hbm,
                  o_ref,
                  win_s, wbr_s, wout_s, stage_s, dma_sem,
                  hb_s, ltail_s, mtail_s, hcar_s, q_s, k_s, v_s, og_s, xc_s, g_s, st_s, m_s, y_s, acc_s, mg_s,
                  *, nb, tr):
    L = tr
    R = nb * tr
    rows = [slice(b * tr, (b + 1) * tr) for b in range(nb)]

    @pl.when(pl.program_id(0) == 0)
    def _():
        ltail_s[...] = jnp.zeros(ltail_s.shape, F32)
        mtail_s[...] = jnp.zeros(mtail_s.shape, F32)
        hcar_s[...] = jnp.zeros(hcar_s.shape, F32)
        st_s[...] = jnp.zeros(st_s.shape, F32)
        m_s[...] = jnp.zeros(m_s.shape, F32)
        _load_weights(win_hbm, wbr_hbm, wout_hbm, win_s, wbr_s, wout_s, stage_s, dma_sem)

    def vec(i, sl=slice(None)):
        return vec_ref[i:i + 1, sl]

    hb_s[...] = _rms_norm(x_ref[...].reshape(R, D_MODEL), vec(V_NORM_G)).astype(BF16)

    def proj(j):
        return _dot(hb_s[...], win_s[j]) + vec(V_B_IN + j // N_HEADS, _blk(j % N_HEADS))

    def conv(x, tail_ref, cw_row, cb_row, sl):
        cw = vec_ref[cw_row:cw_row + CONV_WIDTH, sl]
        parts = []
        for b in range(nb):
            xb = x[rows[b], :]
            parts.append(_causal_conv(xb, tail_ref[b, :, sl], cw, vec(cb_row, sl)))
            tail_ref[b, :, sl] = xb[tr - SUBLANES:, :]
        return jnp.concatenate(parts, axis=0)


    def lru_block(g):
        sl = _blk(g)
        px = proj(LRU_X + g)
        yield
        u = conv(px, ltail_s, V_LRU_CONV_W, V_LRU_CONV_B, sl)
        ub = u.astype(BF16)
        yield
        r_pre = _dot(ub, hw_ref[HW_LRU_A, g]) + vec(V_LRU_B_A, sl)
        i_pre = _dot(ub, hw_ref[HW_LRU_X, g]) + vec(V_LRU_B_X, sl)
        yield
        zh = proj(LRU_Z + g)
        ls_c = (0.5 * LRU_C) * _log_sigmoid(vec(V_LRU_LAMBDA, sl))
        yield
        for b in range(nb):
            h0 = hcar_s[b, :, sl]
            for c in range(tr // ROW_CHUNK):
                rs = slice(b * tr + c * ROW_CHUNK, b * tr + (c + 1) * ROW_CHUNK)
                log_a = ls_c * _tanh1(r_pre[rs, :])
                a = jnp.exp(log_a)
                om = -jnp.tanh(log_a) * (1.0 + a * a)
                mult = jnp.where(om > 0.0, om * lax.rsqrt(om), 0.0)
                bb = mult * _tanh1(i_pre[rs, :]) * u[rs, :]
                h, h0 = _lru_scan(a, bb, h0)
                y_s[0, rs, sl] = (h * (zh[rs, :] * _tanh1(zh[rs, :]))).astype(BF16)
                yield
            hcar_s[b, :, sl] = h0

    def xattn_head(h):
        sl = _blk(h)
        q = proj(XA_Q + h).astype(BF16)
        yield
        scores = [_dot_nt(q[rows[b], :], kv_ref[b, h]) * (HEAD_DIM ** -0.5) for b in range(nb)]
        yield
        probs = []
        for s in scores:
            e = jnp.exp(s - jnp.max(s, axis=-1, keepdims=True))
            probs.append((e * (1.0 / jnp.sum(e, axis=-1, keepdims=True))).astype(BF16))
            yield
        o = jnp.concatenate([_dot(probs[b], kv_ref[b, N_HEADS + h]) for b in range(nb)], axis=0)
        yield
        zh = proj(XA_Z + h)
        yield
        y_s[2, :, sl] = (o * (zh * _tanh1(zh))).astype(BF16)

    def mlstm_front(h):
        sl = _blk(h)
        xm = proj(M_X + h)
        yield
        ch = conv(xm, mtail_s, V_M_CONV_W, V_M_CONV_B, sl)
        xc = ch * _tanh1(ch)
        xc_s[:, sl] = xc
        xcb = xc.astype(BF16)
        xmb = xm.astype(BF16)
        yield
        q_s[:, sl] = _dot(xcb, hw_ref[HW_M_Q, h]).astype(BF16)
        k_s[:, sl] = _dot(xcb, hw_ref[HW_M_K, h]).astype(BF16)
        yield
        v_s[:, sl] = _dot(xmb, hw_ref[HW_M_V, h]).astype(BF16)
        og_s[:, sl] = _tanh1(_dot(xmb, hw_ref[HW_M_O, h]) + vec(V_M_B_O, sl))
        fronts_done.append(h)

    fronts_done = []
    gate_terms = []

    def mlstm_gates():
        while len(fronts_done) < N_HEADS:
            yield
        g_s[...] = (_dot(q_s[...], wif_ref[0:D_MODEL, :])
                    + _dot(k_s[...], wif_ref[D_MODEL:2 * D_MODEL, :])
                    + _dot(v_s[...], wif_ref[2 * D_MODEL:3 * D_MODEL, :])
                    + vec(V_B_IF, slice(0, 2 * LANES)))
        yield
        for b in range(nb):
            bcum = _prefix_rows(_log_sigmoid(g_s[rows[b], LANES:2 * LANES]), jnp.add, 0.0)
            gq = g_s[rows[b], 0:LANES] - bcum
            cmax = _prefix_rows(gq, jnp.maximum, -jnp.inf)
            gate_terms.append((bcum, gq, cmax, gq.T))
            yield

    causal = (lax.broadcasted_iota(jnp.int32, (L, L), 0) >= lax.broadcasted_iota(jnp.int32, (L, L), 1))

    def mlstm_chunk(b, h, terms, out):
        sl = _blk(h)
        bcum, gq, cmax, gq_t = terms
        b_rep = _lane_bcast(bcum, h)
        gq_rep = _lane_bcast(gq, h)
        gq_row = gq_t[h:h + 1, :]
        m_prev = m_s[b, h, 0:1, :]
        mm = jnp.maximum(m_prev, _lane_bcast(cmax, h))
        mm_last = mm[L - 1:L, :]
        p = jnp.exp(jnp.where(causal, gq_row - _lane_tile(mm, L), -jnp.inf))
        sc = jnp.exp(m_prev - mm)
        qh = q_s[rows[b], sl]
        kh = (k_s[rows[b], sl] * (HEAD_DIM ** -0.5)).astype(BF16)
        vh = v_s[rows[b], sl]
        kt = kh.T
        st = st_s[b, h]
        stb = st.astype(BF16)
        yield
        qk = _dot(qh, jnp.concatenate([kt, stb[:, HEAD_DIM:]], axis=1))
        qc = _dot(qh, stb[:, :HEAD_DIM])
        yield
        s = qk[:, :L] * p
        den = jnp.sum(s, axis=-1, keepdims=True) + sc * qk[:, L:]
        inv = 0.5 / jnp.maximum(jnp.abs(den), jnp.exp(-(b_rep + mm)))
        wk = jnp.exp(gq_rep - mm_last)
        wkv = jnp.concatenate([_lane_tile(wk, HEAD_DIM) * vh.astype(F32), wk], axis=1).astype(BF16)
        yield
        num = _dot(s.astype(BF16), vh) + _lane_tile(sc, HEAD_DIM) * qc
        upd = _dot(kt, wkv)
        yield
        decay = sc[L - 1:L, :]
        st_s[b, h] = _lane_tile(decay, HEAD_DIM + LANES) * st + upd
        m_s[b, h] = jnp.broadcast_to(b_rep[L - 1:L, :] + mm_last, (SUBLANES, LANES))
        out[b] = num * _lane_tile(inv, HEAD_DIM)

    def mlstm_head(h, terms):
        sl = _blk(h)
        hparts = [None] * nb
        yield from _zip_stages([mlstm_chunk(b, h, terms[b], hparts) for b in range(nb)])
        hh = og_s[:, sl] * jnp.concatenate(hparts, axis=0)
        mu = jnp.mean(hh, axis=-1, keepdims=True)
        dl = hh - mu
        var = jnp.mean(dl * dl, axis=-1, keepdims=True)
        hn = dl * lax.rsqrt(var + EPS) * vec(V_M_NORM_G, sl)
        yield
        zh = proj(M_Z + h)
        yield
        y = (hn + vec(V_M_SKIP, sl) * xc_s[:, sl]) * (zh * _tanh1(zh))
        y_s[1, :, sl] = y.astype(BF16)

    def branch_out(n, j):
        gate2 = _tanh1(proj(GATE + N_HEADS * n + j))
        yield
        return gate2 * _dot(y_s[n], wbr_s[n, j])

    def branch_pair(j):
        first = yield from branch_out(0, j)
        yield
        second = yield from branch_out(2, j)
        acc_s[:, _blk(j)] = first + second

    def branch_last(j):
        last = yield from branch_out(1, j)
        mg_s[:, _blk(j)] = (acc_s[:, _blk(j)] + last).astype(BF16)

    heads = range(N_HEADS)
    _interleave([f(i) for i in heads for f in (lru_block, mlstm_front, xattn_head)] + [mlstm_gates()])
    _interleave([f(i) for i in heads for f in (lambda i: mlstm_head(i, gate_terms), branch_pair)])
    _interleave([branch_last(j) for j in heads])
    ssq = None
    for j in range(N_HEADS):
        sl = _blk(j)
        xo = x_ref[:, :, sl].reshape(R, HEAD_DIM) + _dot(mg_s[...], wout_s[j])
        o_ref[:, :, sl] = xo.reshape(nb, tr, HEAD_DIM)
        part = jnp.sum(xo * xo, axis=-1, keepdims=True)
        ssq = part if ssq is None else ssq + part
    scale = lax.rsqrt(ssq * (1.0 / D_MODEL) + EPS)
    o_ref[...] = (o_ref[...].reshape(R, D_MODEL) * scale * vec(V_FINAL_G)).reshape(nb, tr, D_MODEL)


def _resident(a):
    nd = a.ndim
    return pl.BlockSpec(a.shape, lambda i: (0,) * nd, pipeline_mode=pl.Buffered(1))


def kernel(x, mem, norm_g, w_in, b_in, lru_conv_w, lru_conv_b, lru_w_a, lru_b_a, lru_w_x, lru_b_x,
           lru_lambda, m_conv_w, m_conv_b, m_w_q, m_w_k, m_w_v, m_w_o, m_b_o, m_w_if, m_b_if,
           m_norm_g, m_skip, mem_norm_g, w_mem_kv, w_branch, w_out, final_norm_g):
    B, S, D = x.shape
    assert D == D_MODEL and S % TIME_TILE == 0 and norm_g.shape[0] == 1
    tr = TIME_TILE
    H = N_HEADS
    pad_lanes = lambda a: jnp.pad(a, ((0, 0), (0, LANES - H)))
    w_if = jnp.concatenate([pad_lanes(m_w_if[0][:, :H]), pad_lanes(m_w_if[0][:, H:])], axis=1).astype(BF16)
    b_if2 = m_b_if[0].reshape(1, 2 * H).astype(F32)
    b_if = jnp.pad(jnp.concatenate([pad_lanes(b_if2[:, :H]), pad_lanes(b_if2[:, H:])], axis=1),
                   ((0, 0), (0, D - 2 * LANES)))

    rows_in = [norm_g, lru_conv_b, lru_b_a, lru_b_x, lru_lambda, m_conv_b, m_b_o, m_norm_g, m_skip,
               final_norm_g.reshape(1, D), b_if]
    prep_in = rows_in + [lru_conv_w, m_conv_w, b_in.reshape(IN_GROUPS, D),
                         lru_w_a, lru_w_x, m_w_q, m_w_k, m_w_v, m_w_o]
    whole = lambda a: pl.BlockSpec(a.shape, lambda c: (0,) * a.ndim)
    kv, vecs, head_w = pl.pallas_call(
        _prep_kernel,
        grid=(2,),
        in_specs=[whole(mem), whole(mem_norm_g), pl.BlockSpec((None, D, D), lambda c: (0, 0, c))]
        + [whole(a) for a in prep_in],
        out_specs=[pl.BlockSpec((B, H, N_MEM, HEAD_DIM), lambda c: (0, c, 0, 0)),
                   pl.BlockSpec((N_VEC_ROWS, D), lambda c: (0, 0)),
                   pl.BlockSpec((len(HEAD_W_SCALE), H, HEAD_DIM, HEAD_DIM), lambda c: (0, 0, 0, 0))],
        out_shape=[jax.ShapeDtypeStruct((B, 2 * H, N_MEM, HEAD_DIM), BF16),
                   jax.ShapeDtypeStruct((N_VEC_ROWS, D), F32),
                   jax.ShapeDtypeStruct((len(HEAD_W_SCALE), H, HEAD_DIM, HEAD_DIM), BF16)],
        compiler_params=pltpu.CompilerParams(dimension_semantics=("arbitrary",),
                                             vmem_limit_bytes=VMEM_LIMIT_BYTES),
        name="prep",
    )(mem, mem_norm_g, w_mem_kv, *prep_in)
    weights = [vecs, head_w, w_if, kv]
    hbm_weights = [w_in, w_branch, w_out]

    R = B * tr
    tok = pl.BlockSpec((B, tr, D), lambda i: (0, i, 0))
    nblk = D // MXU_COLS
    scratch = [pltpu.VMEM((IN_GROUPS * nblk, D, MXU_COLS), BF16),
               pltpu.VMEM((N_BRANCH, nblk, D, MXU_COLS), BF16),
               pltpu.VMEM((nblk, D, MXU_COLS), BF16),
               pltpu.VMEM((2, D, STAGE_COLS), F32),
               pltpu.SemaphoreType.DMA((2,)),
               pltpu.VMEM((R, D), BF16),
               pltpu.VMEM((B, SUBLANES, D), F32),
               pltpu.VMEM((B, SUBLANES, D), F32),
               pltpu.VMEM((B, SUBLANES, D), F32),
               pltpu.VMEM((R, D), BF16),
               pltpu.VMEM((R, D), BF16),
               pltpu.VMEM((R, D), BF16),
               pltpu.VMEM((R, D), F32),
               pltpu.VMEM((R, D), F32),
               pltpu.VMEM((R, 2 * LANES), F32),
               pltpu.VMEM((B, H, HEAD_DIM, HEAD_DIM + LANES), F32),
               pltpu.VMEM((B, H, SUBLANES, LANES), F32),
               pltpu.VMEM((N_BRANCH, R, D), BF16),
               pltpu.VMEM((R, D), F32),
               pltpu.VMEM((R, D), BF16)]
    return pl.pallas_call(
        functools.partial(_layer_kernel, nb=B, tr=tr),
        grid=(S // tr,),
        in_specs=([tok] + [_resident(w) for w in weights]
                  + [pl.BlockSpec(memory_space=pl.ANY)] * len(hbm_weights)),
        out_specs=tok,
        out_shape=jax.ShapeDtypeStruct((B, S, D), F32),
        scratch_shapes=scratch,
        compiler_params=pltpu.CompilerParams(
            dimension_semantics=("arbitrary",),
            vmem_limit_bytes=VMEM_LIMIT_BYTES),
        name="hybrid_layer",
    )(x, *weights, *hbm_weights)
```

```python
import functools

import jax
import jax.numpy as jnp
from jax import lax
from jax.experimental import pallas as pl
from jax.experimental.pallas import tpu as pltpu

D_MODEL = 1024
N_HEADS = 4
HEAD_DIM = D_MODEL // N_HEADS
N_MEM = 256
N_BRANCH = 3
EPS = 1e-6
CONV_WIDTH = 4
LRU_C = 8.0
LANES = 128
SUBLANES = 8
MXU_COLS = 256
TIME_TILE = 128
ROW_CHUNK = 32
VMEM_LIMIT_BYTES = 60 * 1024 * 1024

LRU_X, LRU_Z, M_X, M_Z, XA_Q, XA_Z, GATE = 0, 4, 8, 12, 16, 20, 24
IN_GROUPS = 9
IN_SCALE = (1.0, 0.5, 1.0, 0.5, 1.0, 0.5, 0.5, 0.5, 0.5)
BRANCH_SCALE = 0.5
STAGE_COLS = 2 * MXU_COLS
(V_NORM_G, V_LRU_CONV_B, V_LRU_B_A, V_LRU_B_X, V_LRU_LAMBDA, V_M_CONV_B, V_M_B_O, V_M_NORM_G, V_M_SKIP,
 V_FINAL_G, V_B_IF) = range(11)
V_LRU_CONV_W = 11
V_M_CONV_W = V_LRU_CONV_W + CONV_WIDTH
V_B_IN = V_M_CONV_W + CONV_WIDTH
N_VEC_ROWS = V_B_IN + IN_GROUPS
HW_LRU_A, HW_LRU_X, HW_M_Q, HW_M_K, HW_M_V, HW_M_O = range(6)
HEAD_W_SCALE = (1.0, 1.0, 1.0, 1.0, 1.0, 0.5)
ROW_INPUTS = ((V_NORM_G, 1.0), (V_LRU_CONV_B, 0.5), (V_LRU_B_A, 0.5), (V_LRU_B_X, 0.5), (V_LRU_LAMBDA, 1.0),
              (V_M_CONV_B, 0.5), (V_M_B_O, 0.5), (V_M_NORM_G, 1.0), (V_M_SKIP, 1.0), (V_FINAL_G, 1.0),
              (V_B_IF, 1.0))
CONV_SCALE = 0.5

BF16 = jnp.bfloat16
F32 = jnp.float32


def _dot(a, b):
    return jnp.dot(a, b, preferred_element_type=F32)


def _dot_nt(a, b):
    return lax.dot_general(a, b, (((1,), (1,)), ((), ())), preferred_element_type=F32)


def _tanh1(v):
    return jnp.tanh(v) + 1.0


def _log_sigmoid(x):
    return jnp.minimum(x, 0.0) - jnp.log1p(jnp.exp(-jnp.abs(x)))


def _rms_norm(x, g):
    ms = jnp.mean(x * x, axis=-1, keepdims=True)
    return x * lax.rsqrt(ms + EPS) * g


def _blk(j):
    return slice(j * HEAD_DIM, (j + 1) * HEAD_DIM)


def _prefix_rows(x, op, identity):
    n = x.shape[0]
    row = lax.broadcasted_iota(jnp.int32, x.shape, 0)
    k = 1
    while k < n:
        x = op(x, jnp.where(row >= k, pltpu.roll(x, k, 0), identity))
        k *= 2
    return x


def _lane_tile(x, n):
    reps = n // LANES
    return x if reps == 1 else jnp.concatenate([x] * reps, axis=-1)


def _lane_bcast(x, lane):
    return jnp.broadcast_to(x[:, lane:lane + 1], x.shape)


def _shift_rows(x3, k):
    sub = lax.broadcasted_iota(jnp.int32, (x3.shape[0] - 1,) + x3.shape[1:], 1)
    r = pltpu.roll(x3, k, 1)
    return jnp.where(sub < k, r[:-1], r[1:])


def _causal_conv(xb, tail, cw, cb):
    assert CONV_WIDTH == 4
    t, c = xb.shape
    groups = t // SUBLANES
    full = jnp.concatenate([tail, xb], axis=0).reshape(groups + 1, SUBLANES, c)
    w0, w1, w2, w3 = (cw[j:j + 1, :] for j in range(CONV_WIDTH))
    prev = _shift_rows(full, 1)
    tail3 = full[:1]
    tail_prev = pltpu.roll(tail3, 1, 1)
    older = jnp.concatenate([w1 * tail3 + w0 * tail_prev, w1 * full[1:] + w0 * prev], axis=0)
    acc = cb + w3 * full[1:] + w2 * prev + _shift_rows(older, 2)
    return acc.reshape(t, c)


def _lru_scan(a, b, h0):
    t, c = a.shape
    row = lax.broadcasted_iota(jnp.int32, (SUBLANES, c), 0)
    out = []
    for j in range(t // SUBLANES):
        aj = a[j * SUBLANES:(j + 1) * SUBLANES, :]
        bj = b[j * SUBLANES:(j + 1) * SUBLANES, :]
        for k in (1, 2, 4):
            valid = row >= k
            a_sh = pltpu.roll(aj, k, 0)
            b_sh = pltpu.roll(bj, k, 0)
            bj = bj + aj * jnp.where(valid, b_sh, 0.0)
            aj = aj * jnp.where(valid, a_sh, 1.0)
        hj = bj + aj * h0
        out.append(hj)
        h0 = jnp.broadcast_to(hj[SUBLANES - 1:SUBLANES, :], (SUBLANES, c))
    return jnp.concatenate(out, axis=0), h0


def _zip_stages(gens):
    gens = list(gens)
    while gens:
        alive = []
        for g in gens:
            try:
                next(g)
                alive.append(g)
            except StopIteration:
                pass
        gens = alive
        if gens:
            yield


def _interleave(gens, skew=1):
    pending = list(gens)
    active = []
    rnd = 0
    while pending or active:
        while pending and (len(gens) - len(pending)) * skew <= rnd:
            active.append(pending.pop(0))
        alive = []
        for g in active:
            try:
                next(g)
                alive.append(g)
            except StopIteration:
                pass
        active = alive
        rnd += 1


def _scaled(v, scale):
    return v if scale == 1.0 else v * scale


def _prep_kernel(mem_ref, g_ref, w_ref,
                 r0, r1, r2, r3, r4, r5, r6, r7, r8, r9, r10, lcw_ref, mcw_ref, bin_ref,
                 h0, h1, h2, h3, h4, h5,
                 kv_ref, vec_ref, hw_ref):
    nb = mem_ref.shape[0]
    mn = _rms_norm(mem_ref[...].reshape(nb * N_MEM, D_MODEL), g_ref[...]).astype(BF16)
    w = w_ref[...].astype(BF16)
    for j in range(N_HEADS):
        kv_ref[:, j] = _dot(mn, w[:, _blk(j)]).astype(BF16).reshape(nb, N_MEM, HEAD_DIM)

    @pl.when(pl.program_id(0) == 0)
    def _():
        for (row, scale), ref in zip(ROW_INPUTS, (r0, r1, r2, r3, r4, r5, r6, r7, r8, r9, r10)):
            vec_ref[row:row + 1, :] = _scaled(ref[...], scale)
        vec_ref[V_LRU_CONV_W:V_LRU_CONV_W + CONV_WIDTH, :] = lcw_ref[0] * CONV_SCALE
        vec_ref[V_M_CONV_W:V_M_CONV_W + CONV_WIDTH, :] = mcw_ref[0] * CONV_SCALE
        for g in range(IN_GROUPS):
            vec_ref[V_B_IN + g:V_B_IN + g + 1, :] = _scaled(bin_ref[g:g + 1, :], IN_SCALE[g])
        for i, ref in enumerate((h0, h1, h2, h3, h4, h5)):
            hw_ref[i] = _scaled(ref[0], HEAD_W_SCALE[i]).astype(BF16)


def _load_weights(win_hbm, wbr_hbm, wout_hbm, win_s, wbr_s, wout_s, stage_s, sem):
    per = STAGE_COLS // MXU_COLS
    chunks = []
    for c in range(win_hbm.shape[-1] // STAGE_COLS):
        chunks.append((win_hbm.at[0, :, pl.ds(c * STAGE_COLS, STAGE_COLS)], win_s, c * per,
                       IN_SCALE[c * STAGE_COLS // D_MODEL]))
    for n in range(N_BRANCH):
        for c in range(D_MODEL // STAGE_COLS):
            chunks.append((wbr_hbm.at[0, n, :, pl.ds(c * STAGE_COLS, STAGE_COLS)], wbr_s.at[n], c * per,
                           BRANCH_SCALE))
    for c in range(D_MODEL // STAGE_COLS):
        chunks.append((wout_hbm.at[0, :, pl.ds(c * STAGE_COLS, STAGE_COLS)], wout_s, c * per, 1.0))

    def copy(i):
        return pltpu.make_async_copy(chunks[i][0], stage_s.at[i % 2], sem.at[i % 2])

    copy(0).start()
    for i, (_, dest, blk0, scale) in enumerate(chunks):
        if i + 1 < len(chunks):
            copy(i + 1).start()
        copy(i).wait()
        for j in range(per):
            w = stage_s[i % 2, :, j * MXU_COLS:(j + 1) * MXU_COLS]
            dest[blk0 + j] = (w if scale == 1.0 else w * scale).astype(BF16)


def _layer_kernel(x_ref, vec_ref, hw_ref, wif_ref, kv_ref, win_hbm, wbr_hbm, wout_hbm,
                  o_ref,
                  win_s, wbr_s, wout_s, stage_s, dma_sem,
                  hb_s, ltail_s, mtail_s, hcar_s, q_s, k_s, v_s, og_s, xc_s, g_s, st_s, m_s, y_s, acc_s, mg_s,
                  *, nb, tr):
    L = tr
    R = nb * tr
    rows = [slice(b * tr, (b + 1) * tr) for b in range(nb)]

    @pl.when(pl.program_id(0) == 0)
    def _():
        ltail_s[...] = jnp.zeros(ltail_s.shape, F32)
        mtail_s[...] = jnp.zeros(mtail_s.shape, F32)
        hcar_s[...] = jnp.zeros(hcar_s.shape, F32)
        st_s[...] = jnp.zeros(st_s.shape, F32)
        m_s[...] = jnp.zeros(m_s.shape, F32)
        _load_weights(win_hbm, wbr_hbm, wout_hbm, win_s, wbr_s, wout_s, stage_s, dma_sem)

    def vec(i, sl=slice(None)):
        return vec_ref[i:i + 1, sl]

    hb_s[...] = _rms_norm(x_ref[...].reshape(R, D_MODEL), vec(V_NORM_G)).astype(BF16)

    def proj(j):
        return _dot(hb_s[...], win_s[j]) + vec(V_B_IN + j // N_HEADS, _blk(j % N_HEADS))

    def conv(x, tail_ref, cw_row, cb_row, sl):
        cw = vec_ref[cw_row:cw_row + CONV_WIDTH, sl]
        parts = []
        for b in range(nb):
            xb = x[rows[b], :]
            parts.append(_causal_conv(xb, tail_ref[b, :, sl], cw, vec(cb_row, sl)))
            tail_ref[b, :, sl] = xb[tr - SUBLANES:, :]
        return jnp.concatenate(parts, axis=0)


    def lru_block(g):
        sl = _blk(g)
        px = proj(LRU_X + g)
        yield
        u = conv(px, ltail_s, V_LRU_CONV_W, V_LRU_CONV_B, sl)
        ub = u.astype(BF16)
        yield
        r_pre = _dot(ub, hw_ref[HW_LRU_A, g]) + vec(V_LRU_B_A, sl)
        i_pre = _dot(ub, hw_ref[HW_LRU_X, g]) + vec(V_LRU_B_X, sl)
        yield
        zh = proj(LRU_Z + g)
        ls_c = (0.5 * LRU_C) * _log_sigmoid(vec(V_LRU_LAMBDA, sl))
        yield
        for b in range(nb):
            h0 = hcar_s[b, :, sl]
            for c in range(tr // ROW_CHUNK):
                rs = slice(b * tr + c * ROW_CHUNK, b * tr + (c + 1) * ROW_CHUNK)
                log_a = ls_c * _tanh1(r_pre[rs, :])
                a = jnp.exp(log_a)
                om = -jnp.tanh(log_a) * (1.0 + a * a)
                mult = jnp.where(om > 0.0, om * lax.rsqrt(om), 0.0)
                bb = mult * _tanh1(i_pre[rs, :]) * u[rs, :]
                h, h0 = _lru_scan(a, bb, h0)
                y_s[0, rs, sl] = (h * (zh[rs, :] * _tanh1(zh[rs, :]))).astype(BF16)
                yield
            hcar_s[b, :, sl] = h0

    def xattn_head(h):
        sl = _blk(h)
        q = proj(XA_Q + h).astype(BF16)
        yield
        scores = [_dot_nt(q[rows[b], :], kv_ref[b, h]) * (HEAD_DIM ** -0.5) for b in range(nb)]
        yield
        probs = []
        for s in scores:
            e = jnp.exp(s - jnp.max(s, axis=-1, keepdims=True))
            probs.append((e * (1.0 / jnp.sum(e, axis=-1, keepdims=True))).astype(BF16))
            yield
        o = jnp.concatenate([_dot(probs[b], kv_ref[b, N_HEADS + h]) for b in range(nb)], axis=0)
        yield
        zh = proj(XA_Z + h)
        yield
        y_s[2, :, sl] = (o * (zh * _tanh1(zh))).astype(BF16)

    def mlstm_front(h):
        sl = _blk(h)
        xm = proj(M_X + h)
        yield
        ch = conv(xm, mtail_s, V_M_CONV_W, V_M_CONV_B, sl)
        xc = ch * _tanh1(ch)
        xc_s[:, sl] = xc
        xcb = xc.astype(BF16)
        xmb = xm.astype(BF16)
        yield
        q_s[:, sl] = _dot(xcb, hw_ref[HW_M_Q, h]).astype(BF16)
        k_s[:, sl] = _dot(xcb, hw_ref[HW_M_K, h]).astype(BF16)
        yield
        v_s[:, sl] = _dot(xmb, hw_ref[HW_M_V, h]).astype(BF16)
        og_s[:, sl] = _tanh1(_dot(xmb, hw_ref[HW_M_O, h]) + vec(V_M_B_O, sl))
        fronts_done.append(h)

    fronts_done = []
    gate_terms = []

    def mlstm_gates():
        while len(fronts_done) < N_HEADS:
            yield
        g_s[...] = (_dot(q_s[...], wif_ref[0:D_MODEL, :])
                    + _dot(k_s[...], wif_ref[D_MODEL:2 * D_MODEL, :])
                    + _dot(v_s[...], wif_ref[2 * D_MODEL:3 * D_MODEL, :])
                    + vec(V_B_IF, slice(0, 2 * LANES)))
        yield
        for b in range(nb):
            bcum = _prefix_rows(_log_sigmoid(g_s[rows[b], LANES:2 * LANES]), jnp.add, 0.0)
            gq = g_s[rows[b], 0:LANES] - bcum
            cmax = _prefix_rows(gq, jnp.maximum, -jnp.inf)
            gate_terms.append((bcum, gq, cmax, gq.T))
            yield

    causal = (lax.broadcasted_iota(jnp.int32, (L, L), 0) >= lax.broadcasted_iota(jnp.int32, (L, L), 1))

    def mlstm_chunk(b, h, terms, out):
        sl = _blk(h)
        bcum, gq, cmax, gq_t = terms
        b_rep = _lane_bcast(bcum, h)
        gq_rep = _lane_bcast(gq, h)
        gq_row = gq_t[h:h + 1, :]
        m_prev = m_s[b, h, 0:1, :]
        mm = jnp.maximum(m_prev, _lane_bcast(cmax, h))
        mm_last = mm[L - 1:L, :]
        p = jnp.exp(jnp.where(causal, gq_row - _lane_tile(mm, L), -jnp.inf))
        sc = jnp.exp(m_prev - mm)
        qh = q_s[rows[b], sl]
        kh = (k_s[rows[b], sl] * (HEAD_DIM ** -0.5)).astype(BF16)
        vh = v_s[rows[b], sl]
        kt = kh.T
        st = st_s[b, h]
        stb = st.astype(BF16)
        yield
        qk = _dot(qh, jnp.concatenate([kt, stb[:, HEAD_DIM:]], axis=1))
        qc = _dot(qh, stb[:, :HEAD_DIM])
        yield
        s = qk[:, :L] * p
        den = jnp.sum(s, axis=-1, keepdims=True) + sc * qk[:, L:]
        inv = 0.5 / jnp.maximum(jnp.abs(den), jnp.exp(-(b_rep + mm)))
        wk = jnp.exp(gq_rep - mm_last)
        wkv = (_lane_tile(wk, HEAD_DIM) * vh.astype(F32)).astype(BF16)
        wk_row = jnp.exp(gq_row - _lane_tile(mm_last, L))
        n_inc = jnp.sum(kt.astype(F32) * wk_row, axis=-1, keepdims=True)
        yield
        num = _dot(s.astype(BF16), vh) + _lane_tile(sc, HEAD_DIM) * qc
        upd = _dot(kt, wkv)
        yield
        decay = sc[L - 1:L, :]
        st_s[b, h, :, :HEAD_DIM] = _lane_tile(decay, HEAD_DIM) * st[:, :HEAD_DIM] + upd
        st_s[b, h, :, HEAD_DIM:] = decay * st[:, HEAD_DIM:] + n_inc
        m_s[b, h] = jnp.broadcast_to(b_rep[L - 1:L, :] + mm_last, (SUBLANES, LANES))
        out[b] = num * _lane_tile(inv, HEAD_DIM)

    def mlstm_head(h, terms):
        sl = _blk(h)
        hparts = [None] * nb
        yield from _zip_stages([mlstm_chunk(b, h, terms[b], hparts) for b in range(nb)])
        hh = og_s[:, sl] * jnp.concatenate(hparts, axis=0)
        mu = jnp.mean(hh, axis=-1, keepdims=True)
        dl = hh - mu
        var = jnp.mean(dl * dl, axis=-1, keepdims=True)
        hn = dl * lax.rsqrt(var + EPS) * vec(V_M_NORM_G, sl)
        yield
        zh = proj(M_Z + h)
        yield
        y = (hn + vec(V_M_SKIP, sl) * xc_s[:, sl]) * (zh * _tanh1(zh))
        y_s[1, :, sl] = y.astype(BF16)

    def branch_out(n, j):
        gate2 = _tanh1(proj(GATE + N_HEADS * n + j))
        yield
        return gate2 * _dot(y_s[n], wbr_s[n, j])

    def branch_pair(j):
        first = yield from branch_out(0, j)
        yield
        second = yield from branch_out(2, j)
        acc_s[:, _blk(j)] = first + second

    def branch_last(j):
        last = yield from branch_out(1, j)
        mg_s[:, _blk(j)] = (acc_s[:, _blk(j)] + last).astype(BF16)

    heads = range(N_HEADS)
    _interleave([f(i) for i in heads for f in (lru_block, mlstm_front, xattn_head)] + [mlstm_gates()])
    _interleave([f(i) for i in heads for f in (lambda i: mlstm_head(i, gate_terms), branch_pair)])
    _interleave([branch_last(j) for j in heads])
    ssq = None
    for j in range(N_HEADS):
        sl = _blk(j)
        xo = x_ref[:, :, sl].reshape(R, HEAD_DIM) + _dot(mg_s[...], wout_s[j])
        o_ref[:, :, sl] = xo.reshape(nb, tr, HEAD_DIM)
        part = jnp.sum(xo * xo, axis=-1, keepdims=True)
        ssq = part if ssq is None else ssq + part
    scale = lax.rsqrt(ssq * (1.0 / D_MODEL) + EPS)
    o_ref[...] = (o_ref[...].reshape(R, D_MODEL) * scale * vec(V_FINAL_G)).reshape(nb, tr, D_MODEL)


def _resident(a):
    nd = a.ndim
    return pl.BlockSpec(a.shape, lambda i: (0,) * nd, pipeline_mode=pl.Buffered(1))


def kernel(x, mem, norm_g, w_in, b_in, lru_conv_w, lru_conv_b, lru_w_a, lru_b_a, lru_w_x, lru_b_x,
           lru_lambda, m_conv_w, m_conv_b, m_w_q, m_w_k, m_w_v, m_w_o, m_b_o, m_w_if, m_b_if,
           m_norm_g, m_skip, mem_norm_g, w_mem_kv, w_branch, w_out, final_norm_g):
    B, S, D = x.shape
    assert D == D_MODEL and S % TIME_TILE == 0 and norm_g.shape[0] == 1
    tr = TIME_TILE
    H = N_HEADS
    pad_lanes = lambda a: jnp.pad(a, ((0, 0), (0, LANES - H)))
    w_if = jnp.concatenate([pad_lanes(m_w_if[0][:, :H]), pad_lanes(m_w_if[0][:, H:])], axis=1).astype(BF16)
    b_if2 = m_b_if[0].reshape(1, 2 * H).astype(F32)
    b_if = jnp.pad(jnp.concatenate([pad_lanes(b_if2[:, :H]), pad_lanes(b_if2[:, H:])], axis=1),
                   ((0, 0), (0, D - 2 * LANES)))

    rows_in = [norm_g, lru_conv_b, lru_b_a, lru_b_x, lru_lambda, m_conv_b, m_b_o, m_norm_g, m_skip,
               final_norm_g.reshape(1, D), b_if]
    prep_in = rows_in + [lru_conv_w, m_conv_w, b_in.reshape(IN_GROUPS, D),
                         lru_w_a, lru_w_x, m_w_q, m_w_k, m_w_v, m_w_o]
    whole = lambda a: pl.BlockSpec(a.shape, lambda c: (0,) * a.ndim)
    kv, vecs, head_w = pl.pallas_call(
        _prep_kernel,
        grid=(2,),
        in_specs=[whole(mem), whole(mem_norm_g), pl.BlockSpec((None, D, D), lambda c: (0, 0, c))]
        + [whole(a) for a in prep_in],
        out_specs=[pl.BlockSpec((B, H, N_MEM, HEAD_DIM), lambda c: (0, c, 0, 0)),
                   pl.BlockSpec((N_VEC_ROWS, D), lambda c: (0, 0)),
                   pl.BlockSpec((len(HEAD_W_SCALE), H, HEAD_DIM, HEAD_DIM), lambda c: (0, 0, 0, 0))],
        out_shape=[jax.ShapeDtypeStruct((B, 2 * H, N_MEM, HEAD_DIM), BF16),
                   jax.ShapeDtypeStruct((N_VEC_ROWS, D), F32),
                   jax.ShapeDtypeStruct((len(HEAD_W_SCALE), H, HEAD_DIM, HEAD_DIM), BF16)],
        compiler_params=pltpu.CompilerParams(dimension_semantics=("arbitrary",),
                                             vmem_limit_bytes=VMEM_LIMIT_BYTES),
        name="prep",
    )(mem, mem_norm_g, w_mem_kv, *prep_in)
    weights = [vecs, head_w, w_if, kv]
    hbm_weights = [w_in, w_branch, w_out]

    R = B * tr
    tok = pl.BlockSpec((B, tr, D), lambda i: (0, i, 0))
    nblk = D // MXU_COLS
    scratch = [pltpu.VMEM((IN_GROUPS * nblk, D, MXU_COLS), BF16),
               pltpu.VMEM((N_BRANCH, nblk, D, MXU_COLS), BF16),
               pltpu.VMEM((nblk, D, MXU_COLS), BF16),
               pltpu.VMEM((2, D, STAGE_COLS), F32),
               pltpu.SemaphoreType.DMA((2,)),
               pltpu.VMEM((R, D), BF16),
               pltpu.VMEM((B, SUBLANES, D), F32),
               pltpu.VMEM((B, SUBLANES, D), F32),
               pltpu.VMEM((B, SUBLANES, D), F32),
               pltpu.VMEM((R, D), BF16),
               pltpu.VMEM((R, D), BF16),
               pltpu.VMEM((R, D), BF16),
               pltpu.VMEM((R, D), F32),
               pltpu.VMEM((R, D), F32),
               pltpu.VMEM((R, 2 * LANES), F32),
               pltpu.VMEM((B, H, HEAD_DIM, HEAD_DIM + LANES), F32),
               pltpu.VMEM((B, H, SUBLANES, LANES), F32),
               pltpu.VMEM((N_BRANCH, R, D), BF16),
               pltpu.VMEM((R, D), F32),
               pltpu.VMEM((R, D), BF16)]
    return pl.pallas_call(
        functools.partial(_layer_kernel, nb=B, tr=tr),
        grid=(S // tr,),
        in_specs=([tok] + [_resident(w) for w in weights]
                  + [pl.BlockSpec(memory_space=pl.ANY)] * len(hbm_weights)),
        out_specs=tok,
        out_shape=jax.ShapeDtypeStruct((B, S, D), F32),
        scratch_shapes=scratch,
        compiler_params=pltpu.CompilerParams(
            dimension_semantics=("arbitrary",),
            vmem_limit_bytes=VMEM_LIMIT_BYTES),
        name="hybrid_layer",
    )(x, *weights, *hbm_weights)
```

```python
import functools

import jax
import jax.numpy as jnp
from jax import lax
from jax.experimental import pallas as pl
from jax.experimental.pallas import tpu as pltpu

D_MODEL = 1024
N_HEADS = 4
HEAD_DIM = D_MODEL // N_HEADS
N_MEM = 256
N_BRANCH = 3
EPS = 1e-6
CONV_WIDTH = 4
LRU_C = 8.0
LANES = 128
SUBLANES = 8
MXU_COLS = 256
TIME_TILE = 128
ROW_CHUNK = 32
VMEM_LIMIT_BYTES = 60 * 1024 * 1024

LRU_X, LRU_Z, M_X, M_Z, XA_Q, XA_Z, GATE = 0, 4, 8, 12, 16, 20, 24
IN_GROUPS = 9
IN_SCALE = (1.0, 0.5, 1.0, 0.5, 1.0, 0.5, 0.5, 0.5, 0.5)
BRANCH_SCALE = 0.5
STAGE_COLS = 2 * MXU_COLS
(V_NORM_G, V_LRU_CONV_B, V_LRU_B_A, V_LRU_B_X, V_LRU_LAMBDA, V_M_CONV_B, V_M_B_O, V_M_NORM_G, V_M_SKIP,
 V_FINAL_G, V_B_IF) = range(11)
V_LRU_CONV_W = 11
V_M_CONV_W = V_LRU_CONV_W + CONV_WIDTH
V_B_IN = V_M_CONV_W + CONV_WIDTH
N_VEC_ROWS = V_B_IN + IN_GROUPS
HW_LRU_A, HW_LRU_X, HW_M_Q, HW_M_K, HW_M_V, HW_M_O = range(6)
HEAD_W_SCALE = (1.0, 1.0, 1.0, 1.0, 1.0, 0.5)
ROW_INPUTS = ((V_NORM_G, 1.0), (V_LRU_CONV_B, 0.5), (V_LRU_B_A, 0.5), (V_LRU_B_X, 0.5), (V_LRU_LAMBDA, 1.0),
              (V_M_CONV_B, 0.5), (V_M_B_O, 0.5), (V_M_NORM_G, 1.0), (V_M_SKIP, 1.0), (V_FINAL_G, 1.0),
              (V_B_IF, 1.0))
CONV_SCALE = 0.5

BF16 = jnp.bfloat16
F32 = jnp.float32


def _dot(a, b):
    return jnp.dot(a, b, preferred_element_type=F32)


def _dot_nt(a, b):
    return lax.dot_general(a, b, (((1,), (1,)), ((), ())), preferred_element_type=F32)


def _tanh1(v):
    return jnp.tanh(v) + 1.0


def _log_sigmoid(x):
    return jnp.minimum(x, 0.0) - jnp.log1p(jnp.exp(-jnp.abs(x)))


def _rms_norm(x, g):
    ms = jnp.mean(x * x, axis=-1, keepdims=True)
    return x * lax.rsqrt(ms + EPS) * g


def _blk(j):
    return slice(j * HEAD_DIM, (j + 1) * HEAD_DIM)


def _prefix_rows(x, op, identity):
    n = x.shape[0]
    row = lax.broadcasted_iota(jnp.int32, x.shape, 0)
    k = 1
    while k < n:
        x = op(x, jnp.where(row >= k, pltpu.roll(x, k, 0), identity))
        k *= 2
    return x


def _lane_tile(x, n):
    reps = n // LANES
    return x if reps == 1 else jnp.concatenate([x] * reps, axis=-1)


def _lane_bcast(x, lane):
    return jnp.broadcast_to(x[:, lane:lane + 1], x.shape)


def _shift_rows(x3, k):
    sub = lax.broadcasted_iota(jnp.int32, (x3.shape[0] - 1,) + x3.shape[1:], 1)
    r = pltpu.roll(x3, k, 1)
    return jnp.where(sub < k, r[:-1], r[1:])


def _causal_conv(xb, tail, cw, cb):
    assert CONV_WIDTH == 4
    t, c = xb.shape
    groups = t // SUBLANES
    full = jnp.concatenate([tail, xb], axis=0).reshape(groups + 1, SUBLANES, c)
    w0, w1, w2, w3 = (cw[j:j + 1, :] for j in range(CONV_WIDTH))
    prev = _shift_rows(full, 1)
    tail3 = full[:1]
    tail_prev = pltpu.roll(tail3, 1, 1)
    older = jnp.concatenate([w1 * tail3 + w0 * tail_prev, w1 * full[1:] + w0 * prev], axis=0)
    acc = cb + w3 * full[1:] + w2 * prev + _shift_rows(older, 2)
    return acc.reshape(t, c)


def _lru_scan(a, b, h0):
    t, c = a.shape
    row = lax.broadcasted_iota(jnp.int32, (SUBLANES, c), 0)
    out = []
    for j in range(t // SUBLANES):
        aj = a[j * SUBLANES:(j + 1) * SUBLANES, :]
        bj = b[j * SUBLANES:(j + 1) * SUBLANES, :]
        for k in (1, 2, 4):
            valid = row >= k
            a_sh = pltpu.roll(aj, k, 0)
            b_sh = pltpu.roll(bj, k, 0)
            bj = bj + aj * jnp.where(valid, b_sh, 0.0)
            aj = aj * jnp.where(valid, a_sh, 1.0)
        hj = bj + aj * h0
        out.append(hj)
        h0 = jnp.broadcast_to(hj[SUBLANES - 1:SUBLANES, :], (SUBLANES, c))
    return jnp.concatenate(out, axis=0), h0


def _zip_stages(gens):
    gens = list(gens)
    while gens:
        alive = []
        for g in gens:
            try:
                next(g)
                alive.append(g)
            except StopIteration:
                pass
        gens = alive
        if gens:
            yield


def _interleave(gens, skew=1):
    pending = list(gens)
    active = []
    rnd = 0
    while pending or active:
        while pending and (len(gens) - len(pending)) * skew <= rnd:
            active.append(pending.pop(0))
        alive = []
        for g in active:
            try:
                next(g)
                alive.append(g)
            except StopIteration:
                pass
        active = alive
        rnd += 1


def _scaled(v, scale):
    return v if scale == 1.0 else v * scale


def _prep_kernel(mem_ref, g_ref, w_ref,
                 r0, r1, r2, r3, r4, r5, r6, r7, r8, r9, r10, lcw_ref, mcw_ref, bin_ref,
                 h0, h1, h2, h3, h4, h5,
                 kv_ref, vec_ref, hw_ref):
    nb = mem_ref.shape[0]
    mn = _rms_norm(mem_ref[...].reshape(nb * N_MEM, D_MODEL), g_ref[...]).astype(BF16)
    w = w_ref[...].astype(BF16)
    for j in range(N_HEADS):
        kv_ref[:, j] = _dot(mn, w[:, _blk(j)]).astype(BF16).reshape(nb, N_MEM, HEAD_DIM)

    @pl.when(pl.program_id(0) == 0)
    def _():
        for (row, scale), ref in zip(ROW_INPUTS, (r0, r1, r2, r3, r4, r5, r6, r7, r8, r9, r10)):
            vec_ref[row:row + 1, :] = _scaled(ref[...], scale)
        vec_ref[V_LRU_CONV_W:V_LRU_CONV_W + CONV_WIDTH, :] = lcw_ref[0] * CONV_SCALE
        vec_ref[V_M_CONV_W:V_M_CONV_W + CONV_WIDTH, :] = mcw_ref[0] * CONV_SCALE
        for g in range(IN_GROUPS):
            vec_ref[V_B_IN + g:V_B_IN + g + 1, :] = _scaled(bin_ref[g:g + 1, :], IN_SCALE[g])
        for i, ref in enumerate((h0, h1, h2, h3, h4, h5)):
            hw_ref[i] = _scaled(ref[0], HEAD_W_SCALE[i]).astype(BF16)


def _load_weights(win_hbm, wbr_hbm, wout_hbm, win_s, wbr_s, wout_s, stage_s, sem):
    per = STAGE_COLS // MXU_COLS
    chunks = []
    for c in range(win_hbm.shape[-1] // STAGE_COLS):
        chunks.append((win_hbm.at[0, :, pl.ds(c * STAGE_COLS, STAGE_COLS)], win_s, c * per,
                       IN_SCALE[c * STAGE_COLS // D_MODEL]))
    for n in range(N_BRANCH):
        for c in range(D_MODEL // STAGE_COLS):
            chunks.append((wbr_hbm.at[0, n, :, pl.ds(c * STAGE_COLS, STAGE_COLS)], wbr_s.at[n], c * per,
                           BRANCH_SCALE))
    for c in range(D_MODEL // STAGE_COLS):
        chunks.append((wout_hbm.at[0, :, pl.ds(c * STAGE_COLS, STAGE_COLS)], wout_s, c * per, 1.0))

    def copy(i):
        return pltpu.make_async_copy(chunks[i][0], stage_s.at[i % 2], sem.at[i % 2])

    copy(0).start()
    for i, (_, dest, blk0, scale) in enumerate(chunks):
        if i + 1 < len(chunks):
            copy(i + 1).start()
        copy(i).wait()
        for j in range(per):
            w = stage_s[i % 2, :, j * MXU_COLS:(j + 1) * MXU_COLS]
            dest[blk0 + j] = (w if scale == 1.0 else w * scale).astype(BF16)


def _layer_kernel(x_ref, vec_ref, hw_ref, wif_ref, kv_ref, win_hbm, wbr_hbm, wout_hbm,
                  o_ref,
                  win_s, wbr_s, wout_s, stage_s, dma_sem,
                  hb_s, ltail_s, mtail_s, hcar_s, q_s, k_s, v_s, og_s, xc_s, g_s, st_s, m_s, y_s, acc_s, mg_s,
                  *, nb, tr):
    L = tr
    R = nb * tr
    rows = [slice(b * tr, (b + 1) * tr) for b in range(nb)]

    @pl.when(pl.program_id(0) == 0)
    def _():
        ltail_s[...] = jnp.zeros(ltail_s.shape, F32)
        mtail_s[...] = jnp.zeros(mtail_s.shape, F32)
        hcar_s[...] = jnp.zeros(hcar_s.shape, F32)
        st_s[...] = jnp.zeros(st_s.shape, F32)
        m_s[...] = jnp.zeros(m_s.shape, F32)
        _load_weights(win_hbm, wbr_hbm, wout_hbm, win_s, wbr_s, wout_s, stage_s, dma_sem)

    def vec(i, sl=slice(None)):
        return vec_ref[i:i + 1, sl]

    hb_s[...] = _rms_norm(x_ref[...].reshape(R, D_MODEL), vec(V_NORM_G)).astype(BF16)

    def proj(j):
        return _dot(hb_s[...], win_s[j]) + vec(V_B_IN + j // N_HEADS, _blk(j % N_HEADS))

    def conv(x, tail_ref, cw_row, cb_row, sl):
        cw = vec_ref[cw_row:cw_row + CONV_WIDTH, sl]
        parts = []
        for b in range(nb):
            xb = x[rows[b], :]
            parts.append(_causal_conv(xb, tail_ref[b, :, sl], cw, vec(cb_row, sl)))
            tail_ref[b, :, sl] = xb[tr - SUBLANES:, :]
        return jnp.concatenate(parts, axis=0)


    def lru_block(g):
        sl = _blk(g)
        px = proj(LRU_X + g)
        yield
        u = conv(px, ltail_s, V_LRU_CONV_W, V_LRU_CONV_B, sl)
        ub = u.astype(BF16)
        yield
        r_pre = _dot(ub, hw_ref[HW_LRU_A, g]) + vec(V_LRU_B_A, sl)
        i_pre = _dot(ub, hw_ref[HW_LRU_X, g]) + vec(V_LRU_B_X, sl)
        yield
        zh = proj(LRU_Z + g)
        ls_c = (0.5 * LRU_C) * _log_sigmoid(vec(V_LRU_LAMBDA, sl))
        yield
        for b in range(nb):
            h0 = hcar_s[b, :, sl]
            for c in range(tr // ROW_CHUNK):
                rs = slice(b * tr + c * ROW_CHUNK, b * tr + (c + 1) * ROW_CHUNK)
                log_a = ls_c * _tanh1(r_pre[rs, :])
                a = jnp.exp(log_a)
                om = -jnp.tanh(log_a) * (1.0 + a * a)
                mult = jnp.where(om > 0.0, om * lax.rsqrt(om), 0.0)
                bb = mult * _tanh1(i_pre[rs, :]) * u[rs, :]
                h, h0 = _lru_scan(a, bb, h0)
                y_s[0, rs, sl] = (h * (zh[rs, :] * _tanh1(zh[rs, :]))).astype(BF16)
                yield
            hcar_s[b, :, sl] = h0

    def xattn_head(h):
        sl = _blk(h)
        q = proj(XA_Q + h).astype(BF16)
        yield
        scores = [_dot_nt(q[rows[b], :], kv_ref[b, h]) * (HEAD_DIM ** -0.5) for b in range(nb)]
        yield
        probs = []
        for s in scores:
            e = jnp.exp(s - jnp.max(s, axis=-1, keepdims=True))
            probs.append((e * (1.0 / jnp.sum(e, axis=-1, keepdims=True))).astype(BF16))
            yield
        o = jnp.concatenate([_dot(probs[b], kv_ref[b, N_HEADS + h]) for b in range(nb)], axis=0)
        yield
        zh = proj(XA_Z + h)
        yield
        y_s[2, :, sl] = (o * (zh * _tanh1(zh))).astype(BF16)

    def mlstm_front(h):
        sl = _blk(h)
        xm = proj(M_X + h)
        yield
        ch = conv(xm, mtail_s, V_M_CONV_W, V_M_CONV_B, sl)
        xc = ch * _tanh1(ch)
        xc_s[:, sl] = xc
        xcb = xc.astype(BF16)
        xmb = xm.astype(BF16)
        yield
        q_s[:, sl] = _dot(xcb, hw_ref[HW_M_Q, h]).astype(BF16)
        k_s[:, sl] = _dot(xcb, hw_ref[HW_M_K, h]).astype(BF16)
        yield
        v_s[:, sl] = _dot(xmb, hw_ref[HW_M_V, h]).astype(BF16)
        og_s[:, sl] = _tanh1(_dot(xmb, hw_ref[HW_M_O, h]) + vec(V_M_B_O, sl))
        fronts_done.append(h)

    fronts_done = []
    gate_terms = []

    def mlstm_gates():
        while len(fronts_done) < N_HEADS:
            yield
        g_s[...] = (_dot(q_s[...], wif_ref[0:D_MODEL, :])
                    + _dot(k_s[...], wif_ref[D_MODEL:2 * D_MODEL, :])
                    + _dot(v_s[...], wif_ref[2 * D_MODEL:3 * D_MODEL, :])
                    + vec(V_B_IF, slice(0, 2 * LANES)))
        yield
        for b in range(nb):
            bcum = _prefix_rows(_log_sigmoid(g_s[rows[b], LANES:2 * LANES]), jnp.add, 0.0)
            gq = g_s[rows[b], 0:LANES] - bcum
            cmax = _prefix_rows(gq, jnp.maximum, -jnp.inf)
            gate_terms.append((bcum, gq, cmax, gq.T))
            yield

    causal = (lax.broadcasted_iota(jnp.int32, (L, L), 0) >= lax.broadcasted_iota(jnp.int32, (L, L), 1))

    def mlstm_chunk(b, h, terms, out):
        sl = _blk(h)
        bcum, gq, cmax, gq_t = terms
        b_rep = _lane_bcast(bcum, h)
        gq_rep = _lane_bcast(gq, h)
        gq_row = gq_t[h:h + 1, :]
        m_prev = m_s[b, h, 0:1, :]
        mm = jnp.maximum(m_prev, _lane_bcast(cmax, h))
        mm_last = mm[L - 1:L, :]
        p = jnp.exp(jnp.where(causal, gq_row - _lane_tile(mm, L), -jnp.inf))
        sc = jnp.exp(m_prev - mm)
        qh = q_s[rows[b], sl]
        kh = (k_s[rows[b], sl] * (HEAD_DIM ** -0.5)).astype(BF16)
        vh = v_s[rows[b], sl]
        kt = kh.T
        st = st_s[b, h]
        stb = st.astype(BF16)
        yield
        qk = _dot(qh, jnp.concatenate([kt, stb[:, HEAD_DIM:]], axis=1))
        qc = _dot(qh, stb[:, :HEAD_DIM])
        yield
        s = qk[:, :L] * p
        den = jnp.sum(s, axis=-1, keepdims=True) + sc * qk[:, L:]
        inv = 0.5 / jnp.maximum(jnp.abs(den), jnp.exp(-(b_rep + mm)))
        wk = jnp.exp(gq_rep - mm_last)
        wkv = jnp.concatenate([_lane_tile(wk, HEAD_DIM) * vh.astype(F32), wk], axis=1).astype(BF16)
        yield
        num = _dot(s.astype(BF16), vh) + _lane_tile(sc, HEAD_DIM) * qc
        upd = _dot(kt, wkv)
        yield
        decay = sc[L - 1:L, :]
        st_s[b, h] = _lane_tile(decay, HEAD_DIM + LANES) * st + upd
        m_s[b, h] = jnp.broadcast_to(b_rep[L - 1:L, :] + mm_last, (SUBLANES, LANES))
        out[b] = num * _lane_tile(inv, HEAD_DIM)

    def mlstm_head(h, terms):
        sl = _blk(h)
        hparts = [None] * nb
        yield from _zip_stages([mlstm_chunk(b, h, terms[b], hparts) for b in range(nb)])
        hh = og_s[:, sl] * jnp.concatenate(hparts, axis=0)
        mu = jnp.mean(hh, axis=-1, keepdims=True)
        dl = hh - mu
        var = jnp.mean(dl * dl, axis=-1, keepdims=True)
        hn = dl * lax.rsqrt(var + EPS) * vec(V_M_NORM_G, sl)
        yield
        zh = proj(M_Z + h)
        yield
        y = (hn + vec(V_M_SKIP, sl) * xc_s[:, sl]) * (zh * _tanh1(zh))
        y_s[1, :, sl] = y.astype(BF16)

    def branch_out(n, j):
        gate2 = _tanh1(proj(GATE + N_HEADS * n + j))
        yield
        return gate2 * _dot(y_s[n], wbr_s[n, j])

    def branch_pair(j):
        first = yield from branch_out(0, j)
        yield
        second = yield from branch_out(2, j)
        acc_s[:, _blk(j)] = first + second

    def branch_last(j):
        last = yield from branch_out(1, j)
        mg_s[:, _blk(j)] = (acc_s[:, _blk(j)] + last).astype(BF16)

    heads = range(N_HEADS)
    _interleave([f(i) for i in heads for f in (lru_block, mlstm_front, xattn_head)] + [mlstm_gates()])
    _interleave([f(i) for i in heads for f in (lambda i: mlstm_head(i, gate_terms), branch_pair)], skew=0)
    _interleave([branch_last(j) for j in heads])
    ssq = None
    for j in range(N_HEADS):
        sl = _blk(j)
        xo = x_ref[:, :, sl].reshape(R, HEAD_DIM) + _dot(mg_s[...], wout_s[j])
        o_ref[:, :, sl] = xo.reshape(nb, tr, HEAD_DIM)
        part = jnp.sum(xo * xo, axis=-1, keepdims=True)
        ssq = part if ssq is None else ssq + part
    scale = lax.rsqrt(ssq * (1.0 / D_MODEL) + EPS)
    o_ref[...] = (o_ref[...].reshape(R, D_MODEL) * scale * vec(V_FINAL_G)).reshape(nb, tr, D_MODEL)


def _resident(a):
    nd = a.ndim
    return pl.BlockSpec(a.shape, lambda i: (0,) * nd, pipeline_mode=pl.Buffered(1))


def kernel(x, mem, norm_g, w_in, b_in, lru_conv_w, lru_conv_b, lru_w_a, lru_b_a, lru_w_x, lru_b_x,
           lru_lambda, m_conv_w, m_conv_b, m_w_q, m_w_k, m_w_v, m_w_o, m_b_o, m_w_if, m_b_if,
           m_norm_g, m_skip, mem_norm_g, w_mem_kv, w_branch, w_out, final_norm_g):
    B, S, D = x.shape
    assert D == D_MODEL and S % TIME_TILE == 0 and norm_g.shape[0] == 1
    tr = TIME_TILE
    H = N_HEADS
    pad_lanes = lambda a: jnp.pad(a, ((0, 0), (0, LANES - H)))
    w_if = jnp.concatenate([pad_lanes(m_w_if[0][:, :H]), pad_lanes(m_w_if[0][:, H:])], axis=1).astype(BF16)
    b_if2 = m_b_if[0].reshape(1, 2 * H).astype(F32)
    b_if = jnp.pad(jnp.concatenate([pad_lanes(b_if2[:, :H]), pad_lanes(b_if2[:, H:])], axis=1),
                   ((0, 0), (0, D - 2 * LANES)))

    rows_in = [norm_g, lru_conv_b, lru_b_a, lru_b_x, lru_lambda, m_conv_b, m_b_o, m_norm_g, m_skip,
               final_norm_g.reshape(1, D), b_if]
    prep_in = rows_in + [lru_conv_w, m_conv_w, b_in.reshape(IN_GROUPS, D),
                         lru_w_a, lru_w_x, m_w_q, m_w_k, m_w_v, m_w_o]
    whole = lambda a: pl.BlockSpec(a.shape, lambda c: (0,) * a.ndim)
    kv, vecs, head_w = pl.pallas_call(
        _prep_kernel,
        grid=(2,),
        in_specs=[whole(mem), whole(mem_norm_g), pl.BlockSpec((None, D, D), lambda c: (0, 0, c))]
        + [whole(a) for a in prep_in],
        out_specs=[pl.BlockSpec((B, H, N_MEM, HEAD_DIM), lambda c: (0, c, 0, 0)),
                   pl.BlockSpec((N_VEC_ROWS, D), lambda c: (0, 0)),
                   pl.BlockSpec((len(HEAD_W_SCALE), H, HEAD_DIM, HEAD_DIM), lambda c: (0, 0, 0, 0))],
        out_shape=[jax.ShapeDtypeStruct((B, 2 * H, N_MEM, HEAD_DIM), BF16),
                   jax.ShapeDtypeStruct((N_VEC_ROWS, D), F32),
                   jax.ShapeDtypeStruct((len(HEAD_W_SCALE), H, HEAD_DIM, HEAD_DIM), BF16)],
        compiler_params=pltpu.CompilerParams(dimension_semantics=("arbitrary",),
                                             vmem_limit_bytes=VMEM_LIMIT_BYTES),
        name="prep",
    )(mem, mem_norm_g, w_mem_kv, *prep_in)
    weights = [vecs, head_w, w_if, kv]
    hbm_weights = [w_in, w_branch, w_out]

    R = B * tr
    tok = pl.BlockSpec((B, tr, D), lambda i: (0, i, 0))
    nblk = D // MXU_COLS
    scratch = [pltpu.VMEM((IN_GROUPS * nblk, D, MXU_COLS), BF16),
               pltpu.VMEM((N_BRANCH, nblk, D, MXU_COLS), BF16),
               pltpu.VMEM((nblk, D, MXU_COLS), BF16),
               pltpu.VMEM((2, D, STAGE_COLS), F32),
               pltpu.SemaphoreType.DMA((2,)),
               pltpu.VMEM((R, D), BF16),
               pltpu.VMEM((B, SUBLANES, D), F32),
               pltpu.VMEM((B, SUBLANES, D), F32),
               pltpu.VMEM((B, SUBLANES, D), F32),
               pltpu.VMEM((R, D), BF16),
               pltpu.VMEM((R, D), BF16),
               pltpu.VMEM((R, D), BF16),
               pltpu.VMEM((R, D), F32),
               pltpu.VMEM((R, D), F32),
               pltpu.VMEM((R, 2 * LANES), F32),
               pltpu.VMEM((B, H, HEAD_DIM, HEAD_DIM + LANES), F32),
               pltpu.VMEM((B, H, SUBLANES, LANES), F32),
               pltpu.VMEM((N_BRANCH, R, D), BF16),
               pltpu.VMEM((R, D), F32),
               pltpu.VMEM((R, D), BF16)]
    return pl.pallas_call(
        functools.partial(_layer_kernel, nb=B, tr=tr),
        grid=(S // tr,),
        in_specs=([tok] + [_resident(w) for w in weights]
                  + [pl.BlockSpec(memory_space=pl.ANY)] * len(hbm_weights)),
        out_specs=tok,
        out_shape=jax.ShapeDtypeStruct((B, S, D), F32),
        scratch_shapes=scratch,
        compiler_params=pltpu.CompilerParams(
            dimension_semantics=("arbitrary",),
            vmem_limit_bytes=VMEM_LIMIT_BYTES),
        name="hybrid_layer",
    )(x, *weights, *hbm_weights)
```

```python
import functools

import jax
import jax.numpy as jnp
from jax import lax
from jax.experimental import pallas as pl
from jax.experimental.pallas import tpu as pltpu

D_MODEL = 1024
N_HEADS = 4
HEAD_DIM = D_MODEL // N_HEADS
N_MEM = 256
N_BRANCH = 3
EPS = 1e-6
CONV_WIDTH = 4
LRU_C = 8.0
LANES = 128
SUBLANES = 8
MXU_COLS = 256
TIME_TILE = 128
SCAN_PITCH = 24
VMEM_LIMIT_BYTES = 60 * 1024 * 1024

LRU_X, LRU_Z, M_X, M_Z, XA_Q, XA_Z, GATE = 0, 4, 8, 12, 16, 20, 24
IN_GROUPS = 9
IN_SCALE = (1.0, 0.5, 1.0, 0.5, 1.0, 0.5, 0.5, 0.5, 0.5)
BRANCH_SCALE = 0.5
STAGE_COLS = 2 * MXU_COLS
(V_NORM_G, V_LRU_CONV_B, V_LRU_B_A, V_LRU_B_X, V_LRU_LAMBDA, V_M_CONV_B, V_M_B_O, V_M_NORM_G, V_M_SKIP,
 V_FINAL_G, V_B_IF) = range(11)
V_LRU_CONV_W = 11
V_M_CONV_W = V_LRU_CONV_W + CONV_WIDTH
V_B_IN = V_M_CONV_W + CONV_WIDTH
N_VEC_ROWS = V_B_IN + IN_GROUPS
HW_LRU_A, HW_LRU_X, HW_M_Q, HW_M_K, HW_M_V, HW_M_O = range(6)
HEAD_W_SCALE = (1.0, 1.0, 1.0, 1.0, 1.0, 0.5)
ROW_INPUTS = ((V_NORM_G, 1.0), (V_LRU_CONV_B, 0.5), (V_LRU_B_A, 0.5), (V_LRU_B_X, 0.5), (V_LRU_LAMBDA, 1.0),
              (V_M_CONV_B, 0.5), (V_M_B_O, 0.5), (V_M_NORM_G, 1.0), (V_M_SKIP, 1.0), (V_FINAL_G, 1.0),
              (V_B_IF, 1.0))
CONV_SCALE = 0.5

BF16 = jnp.bfloat16
F32 = jnp.float32


def _dot(a, b):
    return jnp.dot(a, b, preferred_element_type=F32)


def _dot_nt(a, b):
    return lax.dot_general(a, b, (((1,), (1,)), ((), ())), preferred_element_type=F32)


def _tanh1(v):
    return jnp.tanh(v) + 1.0


def _log_sigmoid(x):
    return jnp.minimum(x, 0.0) - jnp.log1p(jnp.exp(-jnp.abs(x)))


def _rms_norm(x, g):
    ms = jnp.mean(x * x, axis=-1, keepdims=True)
    return x * lax.rsqrt(ms + EPS) * g


def _blk(j):
    return slice(j * HEAD_DIM, (j + 1) * HEAD_DIM)


def _prefix_rows(x, op, identity):
    n = x.shape[0]
    row = lax.broadcasted_iota(jnp.int32, x.shape, 0)
    k = 1
    while k < n:
        x = op(x, jnp.where(row >= k, pltpu.roll(x, k, 0), identity))
        k *= 2
    return x


def _lane_tile(x, n):
    reps = n // LANES
    return x if reps == 1 else jnp.concatenate([x] * reps, axis=-1)


def _lane_bcast(x, lane):
    return jnp.broadcast_to(x[:, lane:lane + 1], x.shape)


def _shift_rows(x3, k):
    sub = lax.broadcasted_iota(jnp.int32, (x3.shape[0] - 1,) + x3.shape[1:], 1)
    r = pltpu.roll(x3, k, 1)
    return jnp.where(sub < k, r[:-1], r[1:])


def _causal_conv(xb, tail, cw, cb):
    assert CONV_WIDTH == 4
    t, c = xb.shape
    groups = t // SUBLANES
    full = jnp.concatenate([tail, xb], axis=0).reshape(groups + 1, SUBLANES, c)
    w0, w1, w2, w3 = (cw[j:j + 1, :] for j in range(CONV_WIDTH))
    prev = _shift_rows(full, 1)
    tail3 = full[:1]
    tail_prev = pltpu.roll(tail3, 1, 1)
    older = jnp.concatenate([w1 * tail3 + w0 * tail_prev, w1 * full[1:] + w0 * prev], axis=0)
    acc = cb + w3 * full[1:] + w2 * prev + _shift_rows(older, 2)
    return acc.reshape(t, c)


def _zip_stages(gens):
    gens = list(gens)
    while gens:
        alive = []
        for g in gens:
            try:
                next(g)
                alive.append(g)
            except StopIteration:
                pass
        gens = alive
        if gens:
            yield


def _interleave(gens, skew=1):
    pending = list(gens)
    active = []
    rnd = 0
    while pending or active:
        while pending and (len(gens) - len(pending)) * skew <= rnd:
            active.append(pending.pop(0))
        alive = []
        for g in active:
            try:
                next(g)
                alive.append(g)
            except StopIteration:
                pass
        active = alive
        rnd += 1


def _scaled(v, scale):
    return v if scale == 1.0 else v * scale


def _prep_kernel(mem_ref, g_ref, w_ref,
                 r0, r1, r2, r3, r4, r5, r6, r7, r8, r9, r10, lcw_ref, mcw_ref, bin_ref,
                 h0, h1, h2, h3, h4, h5,
                 kv_ref, vec_ref, hw_ref):
    nb = mem_ref.shape[0]
    mn = _rms_norm(mem_ref[...].reshape(nb * N_MEM, D_MODEL), g_ref[...]).astype(BF16)
    w = w_ref[...].astype(BF16)
    for j in range(N_HEADS):
        kv_ref[:, j] = _dot(mn, w[:, _blk(j)]).astype(BF16).reshape(nb, N_MEM, HEAD_DIM)

    @pl.when(pl.program_id(0) == 0)
    def _():
        for (row, scale), ref in zip(ROW_INPUTS, (r0, r1, r2, r3, r4, r5, r6, r7, r8, r9, r10)):
            vec_ref[row:row + 1, :] = _scaled(ref[...], scale)
        vec_ref[V_LRU_CONV_W:V_LRU_CONV_W + CONV_WIDTH, :] = lcw_ref[0] * CONV_SCALE
        vec_ref[V_M_CONV_W:V_M_CONV_W + CONV_WIDTH, :] = mcw_ref[0] * CONV_SCALE
        for g in range(IN_GROUPS):
            vec_ref[V_B_IN + g:V_B_IN + g + 1, :] = _scaled(bin_ref[g:g + 1, :], IN_SCALE[g])
        for i, ref in enumerate((h0, h1, h2, h3, h4, h5)):
            hw_ref[i] = _scaled(ref[0], HEAD_W_SCALE[i]).astype(BF16)


def _load_weights(win_hbm, wbr_hbm, wout_hbm, win_s, wbr_s, wout_s, stage_s, sem):
    per = STAGE_COLS // MXU_COLS
    chunks = []
    for c in range(win_hbm.shape[-1] // STAGE_COLS):
        chunks.append((win_hbm.at[0, :, pl.ds(c * STAGE_COLS, STAGE_COLS)], win_s, c * per,
                       IN_SCALE[c * STAGE_COLS // D_MODEL]))
    for n in range(N_BRANCH):
        for c in range(D_MODEL // STAGE_COLS):
            chunks.append((wbr_hbm.at[0, n, :, pl.ds(c * STAGE_COLS, STAGE_COLS)], wbr_s.at[n], c * per,
                           BRANCH_SCALE))
    for c in range(D_MODEL // STAGE_COLS):
        chunks.append((wout_hbm.at[0, :, pl.ds(c * STAGE_COLS, STAGE_COLS)], wout_s, c * per, 1.0))

    def copy(i):
        return pltpu.make_async_copy(chunks[i][0], stage_s.at[i % 2], sem.at[i % 2])

    copy(0).start()
    for i, (_, dest, blk0, scale) in enumerate(chunks):
        if i + 1 < len(chunks):
            copy(i + 1).start()
        copy(i).wait()
        for j in range(per):
            w = stage_s[i % 2, :, j * MXU_COLS:(j + 1) * MXU_COLS]
            dest[blk0 + j] = (w if scale == 1.0 else w * scale).astype(BF16)


def _layer_kernel(x_ref, vec_ref, hw_ref, wif_ref, kv_ref, win_hbm, wbr_hbm, wout_hbm,
                  o_ref,
                  win_s, wbr_s, wout_s, stage_s, dma_sem,
                  hb_s, ltail_s, mtail_s, hcar_s, q_s, k_s, v_s, og_s, xc_s, g_s, st_s, m_s, y_s, acc_s, mg_s, sa_s, sb_s,
                  *, nb, tr):
    L = tr
    R = nb * tr
    rows = [slice(b * tr, (b + 1) * tr) for b in range(nb)]

    @pl.when(pl.program_id(0) == 0)
    def _():
        ltail_s[...] = jnp.zeros(ltail_s.shape, F32)
        mtail_s[...] = jnp.zeros(mtail_s.shape, F32)
        hcar_s[...] = jnp.zeros(hcar_s.shape, F32)
        st_s[...] = jnp.zeros(st_s.shape, F32)
        m_s[...] = jnp.zeros(m_s.shape, F32)
        _load_weights(win_hbm, wbr_hbm, wout_hbm, win_s, wbr_s, wout_s, stage_s, dma_sem)

    def vec(i, sl=slice(None)):
        return vec_ref[i:i + 1, sl]

    hb_s[...] = _rms_norm(x_ref[...].reshape(R, D_MODEL), vec(V_NORM_G)).astype(BF16)

    def proj(j):
        return _dot(hb_s[...], win_s[j]) + vec(V_B_IN + j // N_HEADS, _blk(j % N_HEADS))

    def conv(x, tail_ref, cw_row, cb_row, sl):
        cw = vec_ref[cw_row:cw_row + CONV_WIDTH, sl]
        parts = []
        for b in range(nb):
            xb = x[rows[b], :]
            parts.append(_causal_conv(xb, tail_ref[b, :, sl], cw, vec(cb_row, sl)))
            tail_ref[b, :, sl] = xb[tr - SUBLANES:, :]
        return jnp.concatenate(parts, axis=0)


    def lru_block(g):
        sl = _blk(g)
        px = proj(LRU_X + g)
        yield
        u = conv(px, ltail_s, V_LRU_CONV_W, V_LRU_CONV_B, sl)
        ub = u.astype(BF16)
        yield
        r_pre = _dot(ub, hw_ref[HW_LRU_A, g]) + vec(V_LRU_B_A, sl)
        i_pre = _dot(ub, hw_ref[HW_LRU_X, g]) + vec(V_LRU_B_X, sl)
        yield
        zh = proj(LRU_Z + g)
        ls_c = (0.5 * LRU_C) * _log_sigmoid(vec(V_LRU_LAMBDA, sl))
        yield
        sub_len = tr // SUBLANES
        tiles = HEAD_DIM // LANES
        for b in range(nb):
            slabs = [(b * N_HEADS + g) * tiles + c for c in range(tiles)]
            for s in range(SUBLANES):
                rs = slice(b * tr + s * sub_len, b * tr + (s + 1) * sub_len)
                log_a = ls_c * _tanh1(r_pre[rs, :])
                a = jnp.exp(log_a)
                om = -jnp.tanh(log_a) * (1.0 + a * a)
                mult = jnp.where(om > 0.0, om * lax.rsqrt(om), 0.0)
                bb = mult * _tanh1(i_pre[rs, :]) * u[rs, :]
                for c in range(tiles):
                    sa_s[slabs[c], pl.ds(s * SCAN_PITCH, sub_len), :] = a[:, c * LANES:(c + 1) * LANES]
                    sb_s[slabs[c], pl.ds(s * SCAN_PITCH, sub_len), :] = bb[:, c * LANES:(c + 1) * LANES]
            yield
            hl = [jnp.zeros((SUBLANES, LANES), F32) for _ in range(tiles)]
            ac = [jnp.ones((SUBLANES, LANES), F32) for _ in range(tiles)]
            for i in range(sub_len):
                for c in range(tiles):
                    rows_i = pl.ds(i, SUBLANES, stride=SCAN_PITCH)
                    av = sa_s[slabs[c], rows_i, :]
                    hl[c] = av * hl[c] + sb_s[slabs[c], rows_i, :]
                    ac[c] = ac[c] * av
                    sb_s[slabs[c], rows_i, :] = hl[c]
                    sa_s[slabs[c], rows_i, :] = ac[c]
            yield
            h_in = [hcar_s[b, 0:1, g * HEAD_DIM + c * LANES:g * HEAD_DIM + (c + 1) * LANES] for c in range(tiles)]
            for s in range(SUBLANES):
                rs = slice(b * tr + s * sub_len, b * tr + (s + 1) * sub_len)
                parts = []
                for c in range(tiles):
                    blk_rows = pl.ds(s * SCAN_PITCH, sub_len)
                    parts.append(sb_s[slabs[c], blk_rows, :] + sa_s[slabs[c], blk_rows, :] * h_in[c])
                    h_in[c] = ac[c][s:s + 1, :] * h_in[c] + hl[c][s:s + 1, :]
                h = jnp.concatenate(parts, axis=1)
                y_s[0, rs, sl] = (h * (zh[rs, :] * _tanh1(zh[rs, :]))).astype(BF16)
            for c in range(tiles):
                hcar_s[b, :, g * HEAD_DIM + c * LANES:g * HEAD_DIM + (c + 1) * LANES] = jnp.broadcast_to(
                    h_in[c], (SUBLANES, LANES))
            yield

    def xattn_head(h):
        sl = _blk(h)
        q = proj(XA_Q + h).astype(BF16)
        yield
        scores = [_dot_nt(q[rows[b], :], kv_ref[b, h]) * (HEAD_DIM ** -0.5) for b in range(nb)]
        yield
        probs = []
        for s in scores:
            e = jnp.exp(s - jnp.max(s, axis=-1, keepdims=True))
            probs.append((e * (1.0 / jnp.sum(e, axis=-1, keepdims=True))).astype(BF16))
            yield
        o = jnp.concatenate([_dot(probs[b], kv_ref[b, N_HEADS + h]) for b in range(nb)], axis=0)
        yield
        zh = proj(XA_Z + h)
        yield
        y_s[2, :, sl] = (o * (zh * _tanh1(zh))).astype(BF16)

    def mlstm_front(h):
        sl = _blk(h)
        xm = proj(M_X + h)
        yield
        ch = conv(xm, mtail_s, V_M_CONV_W, V_M_CONV_B, sl)
        xc = ch * _tanh1(ch)
        xc_s[:, sl] = xc
        xcb = xc.astype(BF16)
        xmb = xm.astype(BF16)
        yield
        q_s[:, sl] = _dot(xcb, hw_ref[HW_M_Q, h]).astype(BF16)
        k_s[:, sl] = _dot(xcb, hw_ref[HW_M_K, h]).astype(BF16)
        yield
        v_s[:, sl] = _dot(xmb, hw_ref[HW_M_V, h]).astype(BF16)
        og_s[:, sl] = _tanh1(_dot(xmb, hw_ref[HW_M_O, h]) + vec(V_M_B_O, sl))
        fronts_done.append(h)

    fronts_done = []
    gate_terms = []

    def mlstm_gates():
        while len(fronts_done) < N_HEADS:
            yield
        g_s[...] = (_dot(q_s[...], wif_ref[0:D_MODEL, :])
                    + _dot(k_s[...], wif_ref[D_MODEL:2 * D_MODEL, :])
                    + _dot(v_s[...], wif_ref[2 * D_MODEL:3 * D_MODEL, :])
                    + vec(V_B_IF, slice(0, 2 * LANES)))
        yield
        for b in range(nb):
            bcum = _prefix_rows(_log_sigmoid(g_s[rows[b], LANES:2 * LANES]), jnp.add, 0.0)
            gq = g_s[rows[b], 0:LANES] - bcum
            cmax = _prefix_rows(gq, jnp.maximum, -jnp.inf)
            gate_terms.append((bcum, gq, cmax, gq.T))
            yield

    causal = (lax.broadcasted_iota(jnp.int32, (L, L), 0) >= lax.broadcasted_iota(jnp.int32, (L, L), 1))

    def mlstm_chunk(b, h, terms, out):
        sl = _blk(h)
        bcum, gq, cmax, gq_t = terms
        b_rep = _lane_bcast(bcum, h)
        gq_rep = _lane_bcast(gq, h)
        gq_row = gq_t[h:h + 1, :]
        m_prev = m_s[b, h, 0:1, :]
        mm = jnp.maximum(m_prev, _lane_bcast(cmax, h))
        mm_last = mm[L - 1:L, :]
        p = jnp.exp(jnp.where(causal, gq_row - _lane_tile(mm, L), -jnp.inf))
        sc = jnp.exp(m_prev - mm)
        qh = q_s[rows[b], sl]
        kh = (k_s[rows[b], sl] * (HEAD_DIM ** -0.5)).astype(BF16)
        vh = v_s[rows[b], sl]
        kt = kh.T
        st = st_s[b, h]
        stb = st.astype(BF16)
        yield
        qk = _dot(qh, jnp.concatenate([kt, stb[:, HEAD_DIM:]], axis=1))
        qc = _dot(qh, stb[:, :HEAD_DIM])
        yield
        s = qk[:, :L] * p
        den = jnp.sum(s, axis=-1, keepdims=True) + sc * qk[:, L:]
        inv = 0.5 / jnp.maximum(jnp.abs(den), jnp.exp(-(b_rep + mm)))
        wk = jnp.exp(gq_rep - mm_last)
        wkv = jnp.concatenate([_lane_tile(wk, HEAD_DIM) * vh.astype(F32), wk], axis=1).astype(BF16)
        yield
        num = _dot(s.astype(BF16), vh) + _lane_tile(sc, HEAD_DIM) * qc
        upd = _dot(kt, wkv)
        yield
        decay = sc[L - 1:L, :]
        st_s[b, h] = _lane_tile(decay, HEAD_DIM + LANES) * st + upd
        m_s[b, h] = jnp.broadcast_to(b_rep[L - 1:L, :] + mm_last, (SUBLANES, LANES))
        out[b] = num * _lane_tile(inv, HEAD_DIM)

    def mlstm_head(h, terms):
        sl = _blk(h)
        hparts = [None] * nb
        yield from _zip_stages([mlstm_chunk(b, h, terms[b], hparts) for b in range(nb)])
        hh = og_s[:, sl] * jnp.concatenate(hparts, axis=0)
        mu = jnp.mean(hh, axis=-1, keepdims=True)
        dl = hh - mu
        var = jnp.mean(dl * dl, axis=-1, keepdims=True)
        hn = dl * lax.rsqrt(var + EPS) * vec(V_M_NORM_G, sl)
        yield
        zh = proj(M_Z + h)
        yield
        y = (hn + vec(V_M_SKIP, sl) * xc_s[:, sl]) * (zh * _tanh1(zh))
        y_s[1, :, sl] = y.astype(BF16)

    def branch_out(n, j):
        gate2 = _tanh1(proj(GATE + N_HEADS * n + j))
        yield
        return gate2 * _dot(y_s[n], wbr_s[n, j])

    def branch_pair(j):
        first = yield from branch_out(0, j)
        yield
        second = yield from branch_out(2, j)
        acc_s[:, _blk(j)] = first + second

    def branch_last(j):
        last = yield from branch_out(1, j)
        mg_s[:, _blk(j)] = (acc_s[:, _blk(j)] + last).astype(BF16)

    heads = range(N_HEADS)
    _interleave([f(i) for i in heads for f in (lru_block, mlstm_front, xattn_head)] + [mlstm_gates()])
    _interleave([f(i) for i in heads for f in (lambda i: mlstm_head(i, gate_terms), branch_pair)], skew=0)
    _interleave([branch_last(j) for j in heads])
    ssq = None
    for j in range(N_HEADS):
        sl = _blk(j)
        xo = x_ref[:, :, sl].reshape(R, HEAD_DIM) + _dot(mg_s[...], wout_s[j])
        o_ref[:, :, sl] = xo.reshape(nb, tr, HEAD_DIM)
        part = jnp.sum(xo * xo, axis=-1, keepdims=True)
        ssq = part if ssq is None else ssq + part
    scale = lax.rsqrt(ssq * (1.0 / D_MODEL) + EPS)
    o_ref[...] = (o_ref[...].reshape(R, D_MODEL) * scale * vec(V_FINAL_G)).reshape(nb, tr, D_MODEL)


def _resident(a):
    nd = a.ndim
    return pl.BlockSpec(a.shape, lambda i: (0,) * nd, pipeline_mode=pl.Buffered(1))


def kernel(x, mem, norm_g, w_in, b_in, lru_conv_w, lru_conv_b, lru_w_a, lru_b_a, lru_w_x, lru_b_x,
           lru_lambda, m_conv_w, m_conv_b, m_w_q, m_w_k, m_w_v, m_w_o, m_b_o, m_w_if, m_b_if,
           m_norm_g, m_skip, mem_norm_g, w_mem_kv, w_branch, w_out, final_norm_g):
    B, S, D = x.shape
    assert D == D_MODEL and S % TIME_TILE == 0 and norm_g.shape[0] == 1
    tr = TIME_TILE
    H = N_HEADS
    pad_lanes = lambda a: jnp.pad(a, ((0, 0), (0, LANES - H)))
    w_if = jnp.concatenate([pad_lanes(m_w_if[0][:, :H]), pad_lanes(m_w_if[0][:, H:])], axis=1).astype(BF16)
    b_if2 = m_b_if[0].reshape(1, 2 * H).astype(F32)
    b_if = jnp.pad(jnp.concatenate([pad_lanes(b_if2[:, :H]), pad_lanes(b_if2[:, H:])], axis=1),
                   ((0, 0), (0, D - 2 * LANES)))

    rows_in = [norm_g, lru_conv_b, lru_b_a, lru_b_x, lru_lambda, m_conv_b, m_b_o, m_norm_g, m_skip,
               final_norm_g.reshape(1, D), b_if]
    prep_in = rows_in + [lru_conv_w, m_conv_w, b_in.reshape(IN_GROUPS, D),
                         lru_w_a, lru_w_x, m_w_q, m_w_k, m_w_v, m_w_o]
    whole = lambda a: pl.BlockSpec(a.shape, lambda c: (0,) * a.ndim)
    kv, vecs, head_w = pl.pallas_call(
        _prep_kernel,
        grid=(2,),
        in_specs=[whole(mem), whole(mem_norm_g), pl.BlockSpec((None, D, D), lambda c: (0, 0, c))]
        + [whole(a) for a in prep_in],
        out_specs=[pl.BlockSpec((B, H, N_MEM, HEAD_DIM), lambda c: (0, c, 0, 0)),
                   pl.BlockSpec((N_VEC_ROWS, D), lambda c: (0, 0)),
                   pl.BlockSpec((len(HEAD_W_SCALE), H, HEAD_DIM, HEAD_DIM), lambda c: (0, 0, 0, 0))],
        out_shape=[jax.ShapeDtypeStruct((B, 2 * H, N_MEM, HEAD_DIM), BF16),
                   jax.ShapeDtypeStruct((N_VEC_ROWS, D), F32),
                   jax.ShapeDtypeStruct((len(HEAD_W_SCALE), H, HEAD_DIM, HEAD_DIM), BF16)],
        compiler_params=pltpu.CompilerParams(dimension_semantics=("arbitrary",),
                                             vmem_limit_bytes=VMEM_LIMIT_BYTES),
        name="prep",
    )(mem, mem_norm_g, w_mem_kv, *prep_in)
    weights = [vecs, head_w, w_if, kv]
    hbm_weights = [w_in, w_branch, w_out]

    R = B * tr
    tok = pl.BlockSpec((B, tr, D), lambda i: (0, i, 0))
    nblk = D // MXU_COLS
    scratch = [pltpu.VMEM((IN_GROUPS * nblk, D, MXU_COLS), BF16),
               pltpu.VMEM((N_BRANCH, nblk, D, MXU_COLS), BF16),
               pltpu.VMEM((nblk, D, MXU_COLS), BF16),
               pltpu.VMEM((2, D, STAGE_COLS), F32),
               pltpu.SemaphoreType.DMA((2,)),
               pltpu.VMEM((R, D), BF16),
               pltpu.VMEM((B, SUBLANES, D), F32),
               pltpu.VMEM((B, SUBLANES, D), F32),
               pltpu.VMEM((B, SUBLANES, D), F32),
               pltpu.VMEM((R, D), BF16),
               pltpu.VMEM((R, D), BF16),
               pltpu.VMEM((R, D), BF16),
               pltpu.VMEM((R, D), F32),
               pltpu.VMEM((R, D), F32),
               pltpu.VMEM((R, 2 * LANES), F32),
               pltpu.VMEM((B, H, HEAD_DIM, HEAD_DIM + LANES), F32),
               pltpu.VMEM((B, H, SUBLANES, LANES), F32),
               pltpu.VMEM((N_BRANCH, R, D), BF16),
               pltpu.VMEM((R, D), F32),
               pltpu.VMEM((R, D), BF16),
               pltpu.VMEM((B * D // LANES, SUBLANES * SCAN_PITCH, LANES), F32),
               pltpu.VMEM((B * D // LANES, SUBLANES * SCAN_PITCH, LANES), F32)]
    return pl.pallas_call(
        functools.partial(_layer_kernel, nb=B, tr=tr),
        grid=(S // tr,),
        in_specs=([tok] + [_resident(w) for w in weights]
                  + [pl.BlockSpec(memory_space=pl.ANY)] * len(hbm_weights)),
        out_specs=tok,
        out_shape=jax.ShapeDtypeStruct((B, S, D), F32),
        scratch_shapes=scratch,
        compiler_params=pltpu.CompilerParams(
            dimension_semantics=("arbitrary",),
            vmem_limit_bytes=VMEM_LIMIT_BYTES),
        name="hybrid_layer",
    )(x, *weights, *hbm_weights)
```

```python
import functools

import jax
import jax.numpy as jnp
from jax import lax
from jax.experimental import pallas as pl
from jax.experimental.pallas import tpu as pltpu

D_MODEL = 1024
N_HEADS = 4
HEAD_DIM = D_MODEL // N_HEADS
N_MEM = 256
N_BRANCH = 3
EPS = 1e-6
CONV_WIDTH = 4
LRU_C = 8.0
LANES = 128
SUBLANES = 8
MXU_COLS = 256
TIME_TILE = 256
ROW_CHUNK = 32
VMEM_LIMIT_BYTES = 60 * 1024 * 1024

LRU_X, LRU_Z, M_X, M_Z, XA_Q, XA_Z, GATE = 0, 4, 8, 12, 16, 20, 24
IN_GROUPS = 9
IN_SCALE = (1.0, 0.5, 1.0, 0.5, 1.0, 0.5, 0.5, 0.5, 0.5)
BRANCH_SCALE = 0.5
STAGE_COLS = 2 * MXU_COLS
(V_NORM_G, V_LRU_CONV_B, V_LRU_B_A, V_LRU_B_X, V_LRU_LAMBDA, V_M_CONV_B, V_M_B_O, V_M_NORM_G, V_M_SKIP,
 V_FINAL_G, V_B_IF) = range(11)
V_LRU_CONV_W = 11
V_M_CONV_W = V_LRU_CONV_W + CONV_WIDTH
V_B_IN = V_M_CONV_W + CONV_WIDTH
N_VEC_ROWS = V_B_IN + IN_GROUPS
HW_LRU_A, HW_LRU_X, HW_M_Q, HW_M_K, HW_M_V, HW_M_O = range(6)
HEAD_W_SCALE = (1.0, 1.0, 1.0, 1.0, 1.0, 0.5)
ROW_INPUTS = ((V_NORM_G, 1.0), (V_LRU_CONV_B, 0.5), (V_LRU_B_A, 0.5), (V_LRU_B_X, 0.5), (V_LRU_LAMBDA, 1.0),
              (V_M_CONV_B, 0.5), (V_M_B_O, 0.5), (V_M_NORM_G, 1.0), (V_M_SKIP, 1.0), (V_FINAL_G, 1.0),
              (V_B_IF, 1.0))
CONV_SCALE = 0.5

BF16 = jnp.bfloat16
F32 = jnp.float32


def _dot(a, b):
    return jnp.dot(a, b, preferred_element_type=F32)


def _dot_nt(a, b):
    return lax.dot_general(a, b, (((1,), (1,)), ((), ())), preferred_element_type=F32)


def _tanh1(v):
    return jnp.tanh(v) + 1.0


def _log_sigmoid(x):
    return jnp.minimum(x, 0.0) - jnp.log1p(jnp.exp(-jnp.abs(x)))


def _rms_norm(x, g):
    ms = jnp.mean(x * x, axis=-1, keepdims=True)
    return x * lax.rsqrt(ms + EPS) * g


def _blk(j):
    return slice(j * HEAD_DIM, (j + 1) * HEAD_DIM)


def _prefix_rows(x, op, identity):
    n = x.shape[0]
    row = lax.broadcasted_iota(jnp.int32, x.shape, 0)
    k = 1
    while k < n:
        x = op(x, jnp.where(row >= k, pltpu.roll(x, k, 0), identity))
        k *= 2
    return x


def _lane_tile(x, n):
    reps = n // LANES
    return x if reps == 1 else jnp.concatenate([x] * reps, axis=-1)


def _lane_bcast(x, lane):
    return jnp.broadcast_to(x[:, lane:lane + 1], x.shape)


def _shift_rows(x3, k):
    sub = lax.broadcasted_iota(jnp.int32, (x3.shape[0] - 1,) + x3.shape[1:], 1)
    r = pltpu.roll(x3, k, 1)
    return jnp.where(sub < k, r[:-1], r[1:])


def _causal_conv(xb, tail, cw, cb):
    assert CONV_WIDTH == 4
    t, c = xb.shape
    groups = t // SUBLANES
    full = jnp.concatenate([tail, xb], axis=0).reshape(groups + 1, SUBLANES, c)
    w0, w1, w2, w3 = (cw[j:j + 1, :] for j in range(CONV_WIDTH))
    prev = _shift_rows(full, 1)
    tail3 = full[:1]
    tail_prev = pltpu.roll(tail3, 1, 1)
    older = jnp.concatenate([w1 * tail3 + w0 * tail_prev, w1 * full[1:] + w0 * prev], axis=0)
    acc = cb + w3 * full[1:] + w2 * prev + _shift_rows(older, 2)
    return acc.reshape(t, c)


def _lru_scan(a, b, h0):
    t, c = a.shape
    row = lax.broadcasted_iota(jnp.int32, (SUBLANES, c), 0)
    out = []
    for j in range(t // SUBLANES):
        aj = a[j * SUBLANES:(j + 1) * SUBLANES, :]
        bj = b[j * SUBLANES:(j + 1) * SUBLANES, :]
        for k in (1, 2, 4):
            valid = row >= k
            a_sh = pltpu.roll(aj, k, 0)
            b_sh = pltpu.roll(bj, k, 0)
            bj = bj + aj * jnp.where(valid, b_sh, 0.0)
            aj = aj * jnp.where(valid, a_sh, 1.0)
        hj = bj + aj * h0
        out.append(hj)
        h0 = jnp.broadcast_to(hj[SUBLANES - 1:SUBLANES, :], (SUBLANES, c))
    return jnp.concatenate(out, axis=0), h0


def _zip_stages(gens):
    gens = list(gens)
    while gens:
        alive = []
        for g in gens:
            try:
                next(g)
                alive.append(g)
            except StopIteration:
                pass
        gens = alive
        if gens:
            yield


def _interleave(gens, skew=1):
    pending = list(gens)
    active = []
    rnd = 0
    while pending or active:
        while pending and (len(gens) - len(pending)) * skew <= rnd:
            active.append(pending.pop(0))
        alive = []
        for g in active:
            try:
                next(g)
                alive.append(g)
            except StopIteration:
                pass
        active = alive
        rnd += 1


def _scaled(v, scale):
    return v if scale == 1.0 else v * scale


def _prep_kernel(mem_ref, g_ref, w_ref,
                 r0, r1, r2, r3, r4, r5, r6, r7, r8, r9, r10, lcw_ref, mcw_ref, bin_ref,
                 h0, h1, h2, h3, h4, h5,
                 kv_ref, vec_ref, hw_ref):
    nb = mem_ref.shape[0]
    mn = _rms_norm(mem_ref[...].reshape(nb * N_MEM, D_MODEL), g_ref[...]).astype(BF16)
    w = w_ref[...].astype(BF16)
    for j in range(N_HEADS):
        kv_ref[:, j] = _dot(mn, w[:, _blk(j)]).astype(BF16).reshape(nb, N_MEM, HEAD_DIM)

    @pl.when(pl.program_id(0) == 0)
    def _():
        for (row, scale), ref in zip(ROW_INPUTS, (r0, r1, r2, r3, r4, r5, r6, r7, r8, r9, r10)):
            vec_ref[row:row + 1, :] = _scaled(ref[...], scale)
        vec_ref[V_LRU_CONV_W:V_LRU_CONV_W + CONV_WIDTH, :] = lcw_ref[0] * CONV_SCALE
        vec_ref[V_M_CONV_W:V_M_CONV_W + CONV_WIDTH, :] = mcw_ref[0] * CONV_SCALE
        for g in range(IN_GROUPS):
            vec_ref[V_B_IN + g:V_B_IN + g + 1, :] = _scaled(bin_ref[g:g + 1, :], IN_SCALE[g])
        for i, ref in enumerate((h0, h1, h2, h3, h4, h5)):
            hw_ref[i] = _scaled(ref[0], HEAD_W_SCALE[i]).astype(BF16)


def _load_weights(win_hbm, wbr_hbm, wout_hbm, win_s, wbr_s, wout_s, stage_s, sem):
    per = STAGE_COLS // MXU_COLS
    chunks = []
    for c in range(win_hbm.shape[-1] // STAGE_COLS):
        chunks.append((win_hbm.at[0, :, pl.ds(c * STAGE_COLS, STAGE_COLS)], win_s, c * per,
                       IN_SCALE[c * STAGE_COLS // D_MODEL]))
    for n in range(N_BRANCH):
        for c in range(D_MODEL // STAGE_COLS):
            chunks.append((wbr_hbm.at[0, n, :, pl.ds(c * STAGE_COLS, STAGE_COLS)], wbr_s.at[n], c * per,
                           BRANCH_SCALE))
    for c in range(D_MODEL // STAGE_COLS):
        chunks.append((wout_hbm.at[0, :, pl.ds(c * STAGE_COLS, STAGE_COLS)], wout_s, c * per, 1.0))

    def copy(i):
        return pltpu.make_async_copy(chunks[i][0], stage_s.at[i % 2], sem.at[i % 2])

    copy(0).start()
    for i, (_, dest, blk0, scale) in enumerate(chunks):
        if i + 1 < len(chunks):
            copy(i + 1).start()
        copy(i).wait()
        for j in range(per):
            w = stage_s[i % 2, :, j * MXU_COLS:(j + 1) * MXU_COLS]
            dest[blk0 + j] = (w if scale == 1.0 else w * scale).astype(BF16)


def _layer_kernel(x_ref, vec_ref, hw_ref, wif_ref, kv_ref, win_hbm, wbr_hbm, wout_hbm,
                  o_ref,
                  win_s, wbr_s, wout_s, stage_s, dma_sem,
                  hb_s, ltail_s, mtail_s, hcar_s, q_s, k_s, v_s, og_s, xc_s, g_s, st_s, m_s, y_s, acc_s, mg_s,
                  *, nb, tr):
    L = tr
    R = nb * tr
    rows = [slice(b * tr, (b + 1) * tr) for b in range(nb)]

    @pl.when(pl.program_id(1) == 0)
    def _():
        ltail_s[...] = jnp.zeros(ltail_s.shape, F32)
        mtail_s[...] = jnp.zeros(mtail_s.shape, F32)
        hcar_s[...] = jnp.zeros(hcar_s.shape, F32)
        st_s[...] = jnp.zeros(st_s.shape, F32)
        m_s[...] = jnp.zeros(m_s.shape, F32)

    @pl.when((pl.program_id(0) == 0) & (pl.program_id(1) == 0))
    def _():
        _load_weights(win_hbm, wbr_hbm, wout_hbm, win_s, wbr_s, wout_s, stage_s, dma_sem)

    def vec(i, sl=slice(None)):
        return vec_ref[i:i + 1, sl]

    hb_s[...] = _rms_norm(x_ref[...].reshape(R, D_MODEL), vec(V_NORM_G)).astype(BF16)

    def proj(j):
        return _dot(hb_s[...], win_s[j]) + vec(V_B_IN + j // N_HEADS, _blk(j % N_HEADS))

    def conv(x, tail_ref, cw_row, cb_row, sl):
        cw = vec_ref[cw_row:cw_row + CONV_WIDTH, sl]
        parts = []
        for b in range(nb):
            xb = x[rows[b], :]
            parts.append(_causal_conv(xb, tail_ref[b, :, sl], cw, vec(cb_row, sl)))
            tail_ref[b, :, sl] = xb[tr - SUBLANES:, :]
        return jnp.concatenate(parts, axis=0)


    def lru_block(g):
        sl = _blk(g)
        px = proj(LRU_X + g)
        yield
        u = conv(px, ltail_s, V_LRU_CONV_W, V_LRU_CONV_B, sl)
        ub = u.astype(BF16)
        yield
        r_pre = _dot(ub, hw_ref[HW_LRU_A, g]) + vec(V_LRU_B_A, sl)
        i_pre = _dot(ub, hw_ref[HW_LRU_X, g]) + vec(V_LRU_B_X, sl)
        yield
        zh = proj(LRU_Z + g)
        ls_c = (0.5 * LRU_C) * _log_sigmoid(vec(V_LRU_LAMBDA, sl))
        yield
        for b in range(nb):
            h0 = hcar_s[b, :, sl]
            for c in range(tr // ROW_CHUNK):
                rs = slice(b * tr + c * ROW_CHUNK, b * tr + (c + 1) * ROW_CHUNK)
                log_a = ls_c * _tanh1(r_pre[rs, :])
                a = jnp.exp(log_a)
                om = -jnp.tanh(log_a) * (1.0 + a * a)
                mult = jnp.where(om > 0.0, om * lax.rsqrt(om), 0.0)
                bb = mult * _tanh1(i_pre[rs, :]) * u[rs, :]
                h, h0 = _lru_scan(a, bb, h0)
                y_s[0, rs, sl] = (h * (zh[rs, :] * _tanh1(zh[rs, :]))).astype(BF16)
                yield
            hcar_s[b, :, sl] = h0

    def xattn_head(h):
        sl = _blk(h)
        q = proj(XA_Q + h).astype(BF16)
        yield
        scores = [_dot_nt(q[rows[b], :], kv_ref[b, h]) * (HEAD_DIM ** -0.5) for b in range(nb)]
        yield
        probs = []
        for s in scores:
            e = jnp.exp(s - jnp.max(s, axis=-1, keepdims=True))
            probs.append((e * (1.0 / jnp.sum(e, axis=-1, keepdims=True))).astype(BF16))
            yield
        o = jnp.concatenate([_dot(probs[b], kv_ref[b, N_HEADS + h]) for b in range(nb)], axis=0)
        yield
        zh = proj(XA_Z + h)
        yield
        y_s[2, :, sl] = (o * (zh * _tanh1(zh))).astype(BF16)

    def mlstm_front(h):
        sl = _blk(h)
        xm = proj(M_X + h)
        yield
        ch = conv(xm, mtail_s, V_M_CONV_W, V_M_CONV_B, sl)
        xc = ch * _tanh1(ch)
        xc_s[:, sl] = xc
        xcb = xc.astype(BF16)
        xmb = xm.astype(BF16)
        yield
        q_s[:, sl] = _dot(xcb, hw_ref[HW_M_Q, h]).astype(BF16)
        k_s[:, sl] = _dot(xcb, hw_ref[HW_M_K, h]).astype(BF16)
        yield
        v_s[:, sl] = _dot(xmb, hw_ref[HW_M_V, h]).astype(BF16)
        og_s[:, sl] = _tanh1(_dot(xmb, hw_ref[HW_M_O, h]) + vec(V_M_B_O, sl))
        fronts_done.append(h)

    fronts_done = []
    gate_terms = []

    def mlstm_gates():
        while len(fronts_done) < N_HEADS:
            yield
        g_s[...] = (_dot(q_s[...], wif_ref[0:D_MODEL, :])
                    + _dot(k_s[...], wif_ref[D_MODEL:2 * D_MODEL, :])
                    + _dot(v_s[...], wif_ref[2 * D_MODEL:3 * D_MODEL, :])
                    + vec(V_B_IF, slice(0, 2 * LANES)))
        yield
        for b in range(nb):
            bcum = _prefix_rows(_log_sigmoid(g_s[rows[b], LANES:2 * LANES]), jnp.add, 0.0)
            gq = g_s[rows[b], 0:LANES] - bcum
            cmax = _prefix_rows(gq, jnp.maximum, -jnp.inf)
            gate_terms.append((bcum, gq, cmax, gq.T))
            yield

    causal = (lax.broadcasted_iota(jnp.int32, (L, L), 0) >= lax.broadcasted_iota(jnp.int32, (L, L), 1))

    def mlstm_chunk(b, h, terms, out):
        sl = _blk(h)
        bcum, gq, cmax, gq_t = terms
        b_rep = _lane_bcast(bcum, h)
        gq_rep = _lane_bcast(gq, h)
        gq_row = gq_t[h:h + 1, :]
        m_prev = m_s[b, h, 0:1, :]
        mm = jnp.maximum(m_prev, _lane_bcast(cmax, h))
        mm_last = mm[L - 1:L, :]
        p = jnp.exp(jnp.where(causal, gq_row - _lane_tile(mm, L), -jnp.inf))
        sc = jnp.exp(m_prev - mm)
        qh = q_s[rows[b], sl]
        kh = (k_s[rows[b], sl] * (HEAD_DIM ** -0.5)).astype(BF16)
        vh = v_s[rows[b], sl]
        kt = kh.T
        st = st_s[b, h]
        stb = st.astype(BF16)
        yield
        qk = _dot(qh, jnp.concatenate([kt, stb[:, HEAD_DIM:]], axis=1))
        qc = _dot(qh, stb[:, :HEAD_DIM])
        yield
        s = qk[:, :L] * p
        den = jnp.sum(s, axis=-1, keepdims=True) + sc * qk[:, L:]
        inv = 0.5 / jnp.maximum(jnp.abs(den), jnp.exp(-(b_rep + mm)))
        wk = jnp.exp(gq_rep - mm_last)
        wkv = jnp.concatenate([_lane_tile(wk, HEAD_DIM) * vh.astype(F32), wk], axis=1).astype(BF16)
        yield
        num = _dot(s.astype(BF16), vh) + _lane_tile(sc, HEAD_DIM) * qc
        upd = _dot(kt, wkv)
        yield
        decay = sc[L - 1:L, :]
        st_s[b, h] = _lane_tile(decay, HEAD_DIM + LANES) * st + upd
        m_s[b, h] = jnp.broadcast_to(b_rep[L - 1:L, :] + mm_last, (SUBLANES, LANES))
        out[b] = num * _lane_tile(inv, HEAD_DIM)

    def mlstm_head(h, terms):
        sl = _blk(h)
        hparts = [None] * nb
        yield from _zip_stages([mlstm_chunk(b, h, terms[b], hparts) for b in range(nb)])
        hh = og_s[:, sl] * jnp.concatenate(hparts, axis=0)
        mu = jnp.mean(hh, axis=-1, keepdims=True)
        dl = hh - mu
        var = jnp.mean(dl * dl, axis=-1, keepdims=True)
        hn = dl * lax.rsqrt(var + EPS) * vec(V_M_NORM_G, sl)
        yield
        zh = proj(M_Z + h)
        yield
        y = (hn + vec(V_M_SKIP, sl) * xc_s[:, sl]) * (zh * _tanh1(zh))
        y_s[1, :, sl] = y.astype(BF16)

    def branch_out(n, j):
        gate2 = _tanh1(proj(GATE + N_HEADS * n + j))
        yield
        return gate2 * _dot(y_s[n], wbr_s[n, j])

    def branch_pair(j):
        first = yield from branch_out(0, j)
        yield
        second = yield from branch_out(2, j)
        acc_s[:, _blk(j)] = first + second

    def branch_last(j):
        last = yield from branch_out(1, j)
        mg_s[:, _blk(j)] = (acc_s[:, _blk(j)] + last).astype(BF16)

    heads = range(N_HEADS)
    _interleave([f(i) for i in heads for f in (lru_block, mlstm_front, xattn_head)] + [mlstm_gates()])
    _interleave([f(i) for i in heads for f in (lambda i: mlstm_head(i, gate_terms), branch_pair)], skew=0)
    _interleave([branch_last(j) for j in heads])
    ssq = None
    for j in range(N_HEADS):
        sl = _blk(j)
        xo = x_ref[:, :, sl].reshape(R, HEAD_DIM) + _dot(mg_s[...], wout_s[j])
        o_ref[:, :, sl] = xo.reshape(nb, tr, HEAD_DIM)
        part = jnp.sum(xo * xo, axis=-1, keepdims=True)
        ssq = part if ssq is None else ssq + part
    scale = lax.rsqrt(ssq * (1.0 / D_MODEL) + EPS)
    o_ref[...] = (o_ref[...].reshape(R, D_MODEL) * scale * vec(V_FINAL_G)).reshape(nb, tr, D_MODEL)


def _resident(a):
    nd = a.ndim
    return pl.BlockSpec(a.shape, lambda b, i: (0,) * nd, pipeline_mode=pl.Buffered(1))


def kernel(x, mem, norm_g, w_in, b_in, lru_conv_w, lru_conv_b, lru_w_a, lru_b_a, lru_w_x, lru_b_x,
           lru_lambda, m_conv_w, m_conv_b, m_w_q, m_w_k, m_w_v, m_w_o, m_b_o, m_w_if, m_b_if,
           m_norm_g, m_skip, mem_norm_g, w_mem_kv, w_branch, w_out, final_norm_g):
    B, S, D = x.shape
    assert D == D_MODEL and S % TIME_TILE == 0 and norm_g.shape[0] == 1
    tr = TIME_TILE
    H = N_HEADS
    pad_lanes = lambda a: jnp.pad(a, ((0, 0), (0, LANES - H)))
    w_if = jnp.concatenate([pad_lanes(m_w_if[0][:, :H]), pad_lanes(m_w_if[0][:, H:])], axis=1).astype(BF16)
    b_if2 = m_b_if[0].reshape(1, 2 * H).astype(F32)
    b_if = jnp.pad(jnp.concatenate([pad_lanes(b_if2[:, :H]), pad_lanes(b_if2[:, H:])], axis=1),
                   ((0, 0), (0, D - 2 * LANES)))

    rows_in = [norm_g, lru_conv_b, lru_b_a, lru_b_x, lru_lambda, m_conv_b, m_b_o, m_norm_g, m_skip,
               final_norm_g.reshape(1, D), b_if]
    prep_in = rows_in + [lru_conv_w, m_conv_w, b_in.reshape(IN_GROUPS, D),
                         lru_w_a, lru_w_x, m_w_q, m_w_k, m_w_v, m_w_o]
    whole = lambda a: pl.BlockSpec(a.shape, lambda c: (0,) * a.ndim)
    kv, vecs, head_w = pl.pallas_call(
        _prep_kernel,
        grid=(2,),
        in_specs=[whole(mem), whole(mem_norm_g), pl.BlockSpec((None, D, D), lambda c: (0, 0, c))]
        + [whole(a) for a in prep_in],
        out_specs=[pl.BlockSpec((B, H, N_MEM, HEAD_DIM), lambda c: (0, c, 0, 0)),
                   pl.BlockSpec((N_VEC_ROWS, D), lambda c: (0, 0)),
                   pl.BlockSpec((len(HEAD_W_SCALE), H, HEAD_DIM, HEAD_DIM), lambda c: (0, 0, 0, 0))],
        out_shape=[jax.ShapeDtypeStruct((B, 2 * H, N_MEM, HEAD_DIM), BF16),
                   jax.ShapeDtypeStruct((N_VEC_ROWS, D), F32),
                   jax.ShapeDtypeStruct((len(HEAD_W_SCALE), H, HEAD_DIM, HEAD_DIM), BF16)],
        compiler_params=pltpu.CompilerParams(dimension_semantics=("arbitrary",),
                                             vmem_limit_bytes=VMEM_LIMIT_BYTES),
        name="prep",
    )(mem, mem_norm_g, w_mem_kv, *prep_in)
    weights = [vecs, head_w, w_if, kv]
    hbm_weights = [w_in, w_branch, w_out]

    R = tr
    tok = pl.BlockSpec((1, tr, D), lambda b, i: (b, i, 0))
    kv_spec = pl.BlockSpec((1, 2 * H, N_MEM, HEAD_DIM), lambda b, i: (b, 0, 0, 0))
    nblk = D // MXU_COLS
    scratch = [pltpu.VMEM((IN_GROUPS * nblk, D, MXU_COLS), BF16),
               pltpu.VMEM((N_BRANCH, nblk, D, MXU_COLS), BF16),
               pltpu.VMEM((nblk, D, MXU_COLS), BF16),
               pltpu.VMEM((2, D, STAGE_COLS), F32),
               pltpu.SemaphoreType.DMA((2,)),
               pltpu.VMEM((R, D), BF16),
               pltpu.VMEM((1, SUBLANES, D), F32),
               pltpu.VMEM((1, SUBLANES, D), F32),
               pltpu.VMEM((1, SUBLANES, D), F32),
               pltpu.VMEM((R, D), BF16),
               pltpu.VMEM((R, D), BF16),
               pltpu.VMEM((R, D), BF16),
               pltpu.VMEM((R, D), F32),
               pltpu.VMEM((R, D), F32),
               pltpu.VMEM((R, 2 * LANES), F32),
               pltpu.VMEM((1, H, HEAD_DIM, HEAD_DIM + LANES), F32),
               pltpu.VMEM((1, H, SUBLANES, LANES), F32),
               pltpu.VMEM((N_BRANCH, R, D), BF16),
               pltpu.VMEM((R, D), F32),
               pltpu.VMEM((R, D), BF16)]
    return pl.pallas_call(
        functools.partial(_layer_kernel, nb=1, tr=tr),
        grid=(B, S // tr),
        in_specs=([tok] + [_resident(w) for w in weights[:-1]] + [kv_spec]
                  + [pl.BlockSpec(memory_space=pl.ANY)] * len(hbm_weights)),
        out_specs=tok,
        out_shape=jax.ShapeDtypeStruct((B, S, D), F32),
        scratch_shapes=scratch,
        compiler_params=pltpu.CompilerParams(
            dimension_semantics=("arbitrary", "arbitrary"),
            vmem_limit_bytes=VMEM_LIMIT_BYTES),
        name="hybrid_layer",
    )(x, *weights, *hbm_weights)
```

```python
import functools

import jax
import jax.numpy as jnp
from jax import lax
from jax.experimental import pallas as pl
from jax.experimental.pallas import tpu as pltpu

D_MODEL = 1024
N_HEADS = 4
HEAD_DIM = D_MODEL // N_HEADS
N_MEM = 256
N_BRANCH = 3
EPS = 1e-6
CONV_WIDTH = 4
LRU_C = 8.0
LANES = 128
SUBLANES = 8
MXU_COLS = 256
TIME_TILE = 128
ROW_CHUNK = 32
VMEM_LIMIT_BYTES = 60 * 1024 * 1024

LRU_X, LRU_Z, M_X, M_Z, XA_Q, XA_Z, GATE = 0, 4, 8, 12, 16, 20, 24
IN_GROUPS = 9
IN_SCALE = (1.0, 0.5, 1.0, 0.5, 1.0, 0.5, 0.5, 0.5, 0.5)
BRANCH_SCALE = 0.5
STAGE_COLS = 2 * MXU_COLS
(V_NORM_G, V_LRU_CONV_B, V_LRU_B_A, V_LRU_B_X, V_LRU_LAMBDA, V_M_CONV_B, V_M_B_O, V_M_NORM_G, V_M_SKIP,
 V_FINAL_G, V_B_IF) = range(11)
V_LRU_CONV_W = 11
V_M_CONV_W = V_LRU_CONV_W + CONV_WIDTH
V_B_IN = V_M_CONV_W + CONV_WIDTH
N_VEC_ROWS = V_B_IN + IN_GROUPS
HW_LRU_A, HW_LRU_X, HW_M_Q, HW_M_K, HW_M_V, HW_M_O = range(6)
HEAD_W_SCALE = (1.0, 1.0, 1.0, 1.0, 1.0, 0.5)
ROW_INPUTS = ((V_NORM_G, 1.0), (V_LRU_CONV_B, 0.5), (V_LRU_B_A, 0.5), (V_LRU_B_X, 0.5), (V_LRU_LAMBDA, 1.0),
              (V_M_CONV_B, 0.5), (V_M_B_O, 0.5), (V_M_NORM_G, 1.0), (V_M_SKIP, 1.0), (V_FINAL_G, 1.0),
              (V_B_IF, 1.0))
CONV_SCALE = 0.5

BF16 = jnp.bfloat16
F32 = jnp.float32


def _dot(a, b):
    return jnp.dot(a, b, preferred_element_type=F32)


def _dot_nt(a, b):
    return lax.dot_general(a, b, (((1,), (1,)), ((), ())), preferred_element_type=F32)


def _tanh1(v):
    return jnp.tanh(v) + 1.0


def _log_sigmoid(x):
    return jnp.minimum(x, 0.0) - jnp.log1p(jnp.exp(-jnp.abs(x)))


def _rms_norm(x, g):
    ms = jnp.mean(x * x, axis=-1, keepdims=True)
    return x * lax.rsqrt(ms + EPS) * g


def _blk(j):
    return slice(j * HEAD_DIM, (j + 1) * HEAD_DIM)


def _prefix_rows(x, op, identity):
    n = x.shape[0]
    row = lax.broadcasted_iota(jnp.int32, x.shape, 0)
    k = 1
    while k < n:
        x = op(x, jnp.where(row >= k, pltpu.roll(x, k, 0), identity))
        k *= 2
    return x


def _lane_tile(x, n):
    reps = n // LANES
    return x if reps == 1 else jnp.concatenate([x] * reps, axis=-1)


def _lane_bcast(x, lane):
    return jnp.broadcast_to(x[:, lane:lane + 1], x.shape)


def _shift_rows(x3, k):
    sub = lax.broadcasted_iota(jnp.int32, (x3.shape[0] - 1,) + x3.shape[1:], 1)
    r = pltpu.roll(x3, k, 1)
    return jnp.where(sub < k, r[:-1], r[1:])


def _causal_conv(xb, tail, cw, cb):
    assert CONV_WIDTH == 4
    t, c = xb.shape
    groups = t // SUBLANES
    full = jnp.concatenate([tail, xb], axis=0).reshape(groups + 1, SUBLANES, c)
    w0, w1, w2, w3 = (cw[j:j + 1, :] for j in range(CONV_WIDTH))
    prev = _shift_rows(full, 1)
    tail3 = full[:1]
    tail_prev = pltpu.roll(tail3, 1, 1)
    older = jnp.concatenate([w1 * tail3 + w0 * tail_prev, w1 * full[1:] + w0 * prev], axis=0)
    acc = cb + w3 * full[1:] + w2 * prev + _shift_rows(older, 2)
    return acc.reshape(t, c)


def _lru_scan(a, b, h0):
    t, c = a.shape
    row = lax.broadcasted_iota(jnp.int32, (SUBLANES, c), 0)
    out = []
    for j in range(t // SUBLANES):
        aj = a[j * SUBLANES:(j + 1) * SUBLANES, :]
        bj = b[j * SUBLANES:(j + 1) * SUBLANES, :]
        for k in (1, 2, 4):
            valid = row >= k
            a_sh = pltpu.roll(aj, k, 0)
            b_sh = pltpu.roll(bj, k, 0)
            bj = bj + aj * jnp.where(valid, b_sh, 0.0)
            aj = aj * jnp.where(valid, a_sh, 1.0)
        hj = bj + aj * h0
        out.append(hj)
        h0 = jnp.broadcast_to(hj[SUBLANES - 1:SUBLANES, :], (SUBLANES, c))
    return jnp.concatenate(out, axis=0), h0


def _zip_stages(gens):
    gens = list(gens)
    while gens:
        alive = []
        for g in gens:
            try:
                next(g)
                alive.append(g)
            except StopIteration:
                pass
        gens = alive
        if gens:
            yield


def _interleave(gens, skew=1):
    pending = list(gens)
    active = []
    rnd = 0
    while pending or active:
        while pending and (len(gens) - len(pending)) * skew <= rnd:
            active.append(pending.pop(0))
        alive = []
        for g in active:
            try:
                next(g)
                alive.append(g)
            except StopIteration:
                pass
        active = alive
        rnd += 1


def _scaled(v, scale):
    return v if scale == 1.0 else v * scale


def _prep_kernel(mem_ref, g_ref, w_ref,
                 r0, r1, r2, r3, r4, r5, r6, r7, r8, r9, r10, lcw_ref, mcw_ref, bin_ref,
                 h0, h1, h2, h3, h4, h5,
                 kv_ref, vec_ref, hw_ref):
    nb = mem_ref.shape[0]
    mn = _rms_norm(mem_ref[...].reshape(nb * N_MEM, D_MODEL), g_ref[...]).astype(BF16)
    w = w_ref[...].astype(BF16)
    for j in range(N_HEADS):
        kv_ref[:, j] = _dot(mn, w[:, _blk(j)]).astype(BF16).reshape(nb, N_MEM, HEAD_DIM)

    @pl.when(pl.program_id(0) == 0)
    def _():
        for (row, scale), ref in zip(ROW_INPUTS, (r0, r1, r2, r3, r4, r5, r6, r7, r8, r9, r10)):
            vec_ref[row:row + 1, :] = _scaled(ref[...], scale)
        vec_ref[V_LRU_CONV_W:V_LRU_CONV_W + CONV_WIDTH, :] = lcw_ref[0] * CONV_SCALE
        vec_ref[V_M_CONV_W:V_M_CONV_W + CONV_WIDTH, :] = mcw_ref[0] * CONV_SCALE
        for g in range(IN_GROUPS):
            vec_ref[V_B_IN + g:V_B_IN + g + 1, :] = _scaled(bin_ref[g:g + 1, :], IN_SCALE[g])
        for i, ref in enumerate((h0, h1, h2, h3, h4, h5)):
            hw_ref[i] = _scaled(ref[0], HEAD_W_SCALE[i]).astype(BF16)


def _load_weights(win_hbm, wbr_hbm, wout_hbm, win_s, wbr_s, wout_s, stage_s, sem):
    per = STAGE_COLS // MXU_COLS
    chunks = []
    for c in range(win_hbm.shape[-1] // STAGE_COLS):
        chunks.append((win_hbm.at[0, :, pl.ds(c * STAGE_COLS, STAGE_COLS)], win_s, c * per,
                       IN_SCALE[c * STAGE_COLS // D_MODEL]))
    for n in range(N_BRANCH):
        for c in range(D_MODEL // STAGE_COLS):
            chunks.append((wbr_hbm.at[0, n, :, pl.ds(c * STAGE_COLS, STAGE_COLS)], wbr_s.at[n], c * per,
                           BRANCH_SCALE))
    for c in range(D_MODEL // STAGE_COLS):
        chunks.append((wout_hbm.at[0, :, pl.ds(c * STAGE_COLS, STAGE_COLS)], wout_s, c * per, 1.0))

    def copy(i):
        return pltpu.make_async_copy(chunks[i][0], stage_s.at[i % 2], sem.at[i % 2])

    copy(0).start()
    for i, (_, dest, blk0, scale) in enumerate(chunks):
        if i + 1 < len(chunks):
            copy(i + 1).start()
        copy(i).wait()
        for j in range(per):
            w = stage_s[i % 2, :, j * MXU_COLS:(j + 1) * MXU_COLS]
            dest[blk0 + j] = (w if scale == 1.0 else w * scale).astype(BF16)


def _layer_kernel(x_ref, vec_ref, hw_ref, wif_ref, kv_ref, win_hbm, wbr_hbm, wout_hbm,
                  o_ref,
                  win_s, wbr_s, wout_s, stage_s, dma_sem,
                  hb_s, ltail_s, mtail_s, hcar_s, q_s, k_s, v_s, og_s, xc_s, g_s, st_s, m_s, y_s, acc_s, mg_s,
                  *, nb, tr):
    L = tr
    R = nb * tr
    rows = [slice(b * tr, (b + 1) * tr) for b in range(nb)]

    @pl.when(pl.program_id(0) == 0)
    def _():
        ltail_s[...] = jnp.zeros(ltail_s.shape, F32)
        mtail_s[...] = jnp.zeros(mtail_s.shape, F32)
        hcar_s[...] = jnp.zeros(hcar_s.shape, F32)
        st_s[...] = jnp.zeros(st_s.shape, F32)
        m_s[...] = jnp.zeros(m_s.shape, F32)
        _load_weights(win_hbm, wbr_hbm, wout_hbm, win_s, wbr_s, wout_s, stage_s, dma_sem)

    def vec(i, sl=slice(None)):
        return vec_ref[i:i + 1, sl]

    hb_s[...] = _rms_norm(x_ref[...].reshape(R, D_MODEL), vec(V_NORM_G)).astype(BF16)

    def proj(j):
        return _dot(hb_s[...], win_s[j]) + vec(V_B_IN + j // N_HEADS, _blk(j % N_HEADS))

    def conv(x, tail_ref, cw_row, cb_row, sl):
        cw = vec_ref[cw_row:cw_row + CONV_WIDTH, sl]
        parts = []
        for b in range(nb):
            xb = x[rows[b], :]
            parts.append(_causal_conv(xb, tail_ref[b, :, sl], cw, vec(cb_row, sl)))
            tail_ref[b, :, sl] = xb[tr - SUBLANES:, :]
        return jnp.concatenate(parts, axis=0)


    def lru_block(g):
        sl = _blk(g)
        px = proj(LRU_X + g)
        yield
        u = conv(px, ltail_s, V_LRU_CONV_W, V_LRU_CONV_B, sl)
        ub = u.astype(BF16)
        yield
        ls_c = (0.5 * LRU_C) * _log_sigmoid(vec(V_LRU_LAMBDA, sl))
        for b in range(nb):
            ubb = ub[rows[b], :]
            r_pre_b = _dot(ubb, hw_ref[HW_LRU_A, g]) + vec(V_LRU_B_A, sl)
            i_pre_b = _dot(ubb, hw_ref[HW_LRU_X, g]) + vec(V_LRU_B_X, sl)
            yield
            zh_b = (_dot(hb_s[rows[b], :], win_s[LRU_Z + g])
                    + vec(V_B_IN + (LRU_Z + g) // N_HEADS, _blk((LRU_Z + g) % N_HEADS)))
            yield
            h0 = hcar_s[b, :, sl]
            for c in range(tr // ROW_CHUNK):
                rs = slice(b * tr + c * ROW_CHUNK, b * tr + (c + 1) * ROW_CHUNK)
                ls = slice(c * ROW_CHUNK, (c + 1) * ROW_CHUNK)
                log_a = ls_c * _tanh1(r_pre_b[ls, :])
                a = jnp.exp(log_a)
                om = -jnp.tanh(log_a) * (1.0 + a * a)
                mult = jnp.where(om > 0.0, om * lax.rsqrt(om), 0.0)
                bb = mult * _tanh1(i_pre_b[ls, :]) * u[rs, :]
                h, h0 = _lru_scan(a, bb, h0)
                y_s[0, rs, sl] = (h * (zh_b[ls, :] * _tanh1(zh_b[ls, :]))).astype(BF16)
                yield
            hcar_s[b, :, sl] = h0

    def xattn_head(h):
        sl = _blk(h)
        q = proj(XA_Q + h).astype(BF16)
        yield
        scores = [_dot_nt(q[rows[b], :], kv_ref[b, h]) * (HEAD_DIM ** -0.5) for b in range(nb)]
        yield
        probs = []
        for s in scores:
            e = jnp.exp(s - jnp.max(s, axis=-1, keepdims=True))
            probs.append((e * (1.0 / jnp.sum(e, axis=-1, keepdims=True))).astype(BF16))
            yield
        o = jnp.concatenate([_dot(probs[b], kv_ref[b, N_HEADS + h]) for b in range(nb)], axis=0)
        yield
        zh = proj(XA_Z + h)
        yield
        y_s[2, :, sl] = (o * (zh * _tanh1(zh))).astype(BF16)

    def mlstm_front(h):
        sl = _blk(h)
        xm = proj(M_X + h)
        yield
        ch = conv(xm, mtail_s, V_M_CONV_W, V_M_CONV_B, sl)
        xc = ch * _tanh1(ch)
        xc_s[:, sl] = xc
        xcb = xc.astype(BF16)
        xmb = xm.astype(BF16)
        yield
        q_s[:, sl] = _dot(xcb, hw_ref[HW_M_Q, h]).astype(BF16)
        k_s[:, sl] = _dot(xcb, hw_ref[HW_M_K, h]).astype(BF16)
        yield
        v_s[:, sl] = _dot(xmb, hw_ref[HW_M_V, h]).astype(BF16)
        og_s[:, sl] = _tanh1(_dot(xmb, hw_ref[HW_M_O, h]) + vec(V_M_B_O, sl))
        fronts_done.append(h)

    fronts_done = []
    gate_terms = []

    def mlstm_gates():
        while len(fronts_done) < N_HEADS:
            yield
        g_s[...] = (_dot(q_s[...], wif_ref[0:D_MODEL, :])
                    + _dot(k_s[...], wif_ref[D_MODEL:2 * D_MODEL, :])
                    + _dot(v_s[...], wif_ref[2 * D_MODEL:3 * D_MODEL, :])
                    + vec(V_B_IF, slice(0, 2 * LANES)))
        yield
        for b in range(nb):
            bcum = _prefix_rows(_log_sigmoid(g_s[rows[b], LANES:2 * LANES]), jnp.add, 0.0)
            gq = g_s[rows[b], 0:LANES] - bcum
            cmax = _prefix_rows(gq, jnp.maximum, -jnp.inf)
            gate_terms.append((bcum, gq, cmax, gq.T))
            yield

    causal = (lax.broadcasted_iota(jnp.int32, (L, L), 0) >= lax.broadcasted_iota(jnp.int32, (L, L), 1))

    def mlstm_chunk(b, h, terms, out):
        sl = _blk(h)
        bcum, gq, cmax, gq_t = terms
        b_rep = _lane_bcast(bcum, h)
        gq_rep = _lane_bcast(gq, h)
        gq_row = gq_t[h:h + 1, :]
        m_prev = m_s[b, h, 0:1, :]
        mm = jnp.maximum(m_prev, _lane_bcast(cmax, h))
        mm_last = mm[L - 1:L, :]
        p = jnp.exp(jnp.where(causal, gq_row - _lane_tile(mm, L), -jnp.inf))
        sc = jnp.exp(m_prev - mm)
        qh = q_s[rows[b], sl]
        kh = (k_s[rows[b], sl] * (HEAD_DIM ** -0.5)).astype(BF16)
        vh = v_s[rows[b], sl]
        kt = kh.T
        st = st_s[b, h]
        stb = st.astype(BF16)
        yield
        qk = _dot(qh, jnp.concatenate([kt, stb[:, HEAD_DIM:]], axis=1))
        qc = _dot(qh, stb[:, :HEAD_DIM])
        yield
        s = qk[:, :L] * p
        den = jnp.sum(s, axis=-1, keepdims=True) + sc * qk[:, L:]
        inv = 0.5 / jnp.maximum(jnp.abs(den), jnp.exp(-(b_rep + mm)))
        wk = jnp.exp(gq_rep - mm_last)
        wkv = jnp.concatenate([_lane_tile(wk, HEAD_DIM) * vh.astype(F32), wk], axis=1).astype(BF16)
        yield
        num = _dot(s.astype(BF16), vh) + _lane_tile(sc, HEAD_DIM) * qc
        upd = _dot(kt, wkv)
        yield
        decay = sc[L - 1:L, :]
        st_s[b, h] = _lane_tile(decay, HEAD_DIM + LANES) * st + upd
        m_s[b, h] = jnp.broadcast_to(b_rep[L - 1:L, :] + mm_last, (SUBLANES, LANES))
        out[b] = num * _lane_tile(inv, HEAD_DIM)

    def mlstm_head(h, terms):
        sl = _blk(h)
        hparts = [None] * nb
        yield from _zip_stages([mlstm_chunk(b, h, terms[b], hparts) for b in range(nb)])
        hh = og_s[:, sl] * jnp.concatenate(hparts, axis=0)
        mu = jnp.mean(hh, axis=-1, keepdims=True)
        dl = hh - mu
        var = jnp.mean(dl * dl, axis=-1, keepdims=True)
        hn = dl * lax.rsqrt(var + EPS) * vec(V_M_NORM_G, sl)
        yield
        zh = proj(M_Z + h)
        yield
        y = (hn + vec(V_M_SKIP, sl) * xc_s[:, sl]) * (zh * _tanh1(zh))
        y_s[1, :, sl] = y.astype(BF16)

    def branch_out(n, j):
        gate2 = _tanh1(proj(GATE + N_HEADS * n + j))
        yield
        return gate2 * _dot(y_s[n], wbr_s[n, j])

    def branch_pair(j):
        first = yield from branch_out(0, j)
        yield
        second = yield from branch_out(2, j)
        acc_s[:, _blk(j)] = first + second

    def branch_last(j):
        last = yield from branch_out(1, j)
        mg_s[:, _blk(j)] = (acc_s[:, _blk(j)] + last).astype(BF16)

    heads = range(N_HEADS)
    _interleave([f(i) for i in heads for f in (lru_block, mlstm_front, xattn_head)] + [mlstm_gates()])
    _interleave([f(i) for i in heads for f in (lambda i: mlstm_head(i, gate_terms), branch_pair)], skew=0)
    _interleave([branch_last(j) for j in heads])
    ssq = None
    for j in range(N_HEADS):
        sl = _blk(j)
        xo = x_ref[:, :, sl].reshape(R, HEAD_DIM) + _dot(mg_s[...], wout_s[j])
        o_ref[:, :, sl] = xo.reshape(nb, tr, HEAD_DIM)
        part = jnp.sum(xo * xo, axis=-1, keepdims=True)
        ssq = part if ssq is None else ssq + part
    scale = lax.rsqrt(ssq * (1.0 / D_MODEL) + EPS)
    o_ref[...] = (o_ref[...].reshape(R, D_MODEL) * scale * vec(V_FINAL_G)).reshape(nb, tr, D_MODEL)


def _resident(a):
    nd = a.ndim
    return pl.BlockSpec(a.shape, lambda i: (0,) * nd, pipeline_mode=pl.Buffered(1))


def kernel(x, mem, norm_g, w_in, b_in, lru_conv_w, lru_conv_b, lru_w_a, lru_b_a, lru_w_x, lru_b_x,
           lru_lambda, m_conv_w, m_conv_b, m_w_q, m_w_k, m_w_v, m_w_o, m_b_o, m_w_if, m_b_if,
           m_norm_g, m_skip, mem_norm_g, w_mem_kv, w_branch, w_out, final_norm_g):
    B, S, D = x.shape
    assert D == D_MODEL and S % TIME_TILE == 0 and norm_g.shape[0] == 1
    tr = TIME_TILE
    H = N_HEADS
    pad_lanes = lambda a: jnp.pad(a, ((0, 0), (0, LANES - H)))
    w_if = jnp.concatenate([pad_lanes(m_w_if[0][:, :H]), pad_lanes(m_w_if[0][:, H:])], axis=1).astype(BF16)
    b_if2 = m_b_if[0].reshape(1, 2 * H).astype(F32)
    b_if = jnp.pad(jnp.concatenate([pad_lanes(b_if2[:, :H]), pad_lanes(b_if2[:, H:])], axis=1),
                   ((0, 0), (0, D - 2 * LANES)))

    rows_in = [norm_g, lru_conv_b, lru_b_a, lru_b_x, lru_lambda, m_conv_b, m_b_o, m_norm_g, m_skip,
               final_norm_g.reshape(1, D), b_if]
    prep_in = rows_in + [lru_conv_w, m_conv_w, b_in.reshape(IN_GROUPS, D),
                         lru_w_a, lru_w_x, m_w_q, m_w_k, m_w_v, m_w_o]
    whole = lambda a: pl.BlockSpec(a.shape, lambda c: (0,) * a.ndim)
    kv, vecs, head_w = pl.pallas_call(
        _prep_kernel,
        grid=(2,),
        in_specs=[whole(mem), whole(mem_norm_g), pl.BlockSpec((None, D, D), lambda c: (0, 0, c))]
        + [whole(a) for a in prep_in],
        out_specs=[pl.BlockSpec((B, H, N_MEM, HEAD_DIM), lambda c: (0, c, 0, 0)),
                   pl.BlockSpec((N_VEC_ROWS, D), lambda c: (0, 0)),
                   pl.BlockSpec((len(HEAD_W_SCALE), H, HEAD_DIM, HEAD_DIM), lambda c: (0, 0, 0, 0))],
        out_shape=[jax.ShapeDtypeStruct((B, 2 * H, N_MEM, HEAD_DIM), BF16),
                   jax.ShapeDtypeStruct((N_VEC_ROWS, D), F32),
                   jax.ShapeDtypeStruct((len(HEAD_W_SCALE), H, HEAD_DIM, HEAD_DIM), BF16)],
        compiler_params=pltpu.CompilerParams(dimension_semantics=("arbitrary",),
                                             vmem_limit_bytes=VMEM_LIMIT_BYTES),
        name="prep",
    )(mem, mem_norm_g, w_mem_kv, *prep_in)
    weights = [vecs, head_w, w_if, kv]
    hbm_weights = [w_in, w_branch, w_out]

    R = B * tr
    tok = pl.BlockSpec((B, tr, D), lambda i: (0, i, 0))
    nblk = D // MXU_COLS
    scratch = [pltpu.VMEM((IN_GROUPS * nblk, D, MXU_COLS), BF16),
               pltpu.VMEM((N_BRANCH, nblk, D, MXU_COLS), BF16),
               pltpu.VMEM((nblk, D, MXU_COLS), BF16),
               pltpu.VMEM((2, D, STAGE_COLS), F32),
               pltpu.SemaphoreType.DMA((2,)),
               pltpu.VMEM((R, D), BF16),
               pltpu.VMEM((B, SUBLANES, D), F32),
               pltpu.VMEM((B, SUBLANES, D), F32),
               pltpu.VMEM((B, SUBLANES, D), F32),
               pltpu.VMEM((R, D), BF16),
               pltpu.VMEM((R, D), BF16),
               pltpu.VMEM((R, D), BF16),
               pltpu.VMEM((R, D), F32),
               pltpu.VMEM((R, D), F32),
               pltpu.VMEM((R, 2 * LANES), F32),
               pltpu.VMEM((B, H, HEAD_DIM, HEAD_DIM + LANES), F32),
               pltpu.VMEM((B, H, SUBLANES, LANES), F32),
               pltpu.VMEM((N_BRANCH, R, D), BF16),
               pltpu.VMEM((R, D), F32),
               pltpu.VMEM((R, D), BF16)]
    return pl.pallas_call(
        functools.partial(_layer_kernel, nb=B, tr=tr),
        grid=(S // tr,),
        in_specs=([tok] + [_resident(w) for w in weights]
                  + [pl.BlockSpec(memory_space=pl.ANY)] * len(hbm_weights)),
        out_specs=tok,
        out_shape=jax.ShapeDtypeStruct((B, S, D), F32),
        scratch_shapes=scratch,
        compiler_params=pltpu.CompilerParams(
            dimension_semantics=("arbitrary",),
            vmem_limit_bytes=VMEM_LIMIT_BYTES),
        name="hybrid_layer",
    )(x, *weights, *hbm_weights)
```

```python
import functools

import jax
import jax.numpy as jnp
from jax import lax
from jax.experimental import pallas as pl
from jax.experimental.pallas import tpu as pltpu

D_MODEL = 1024
N_HEADS = 4
HEAD_DIM = D_MODEL // N_HEADS
N_MEM = 256
N_BRANCH = 3
EPS = 1e-6
CONV_WIDTH = 4
LRU_C = 8.0
LANES = 128
SUBLANES = 8
MXU_COLS = 256
TIME_TILE = 128
ROW_CHUNK = 32
VMEM_LIMIT_BYTES = 60 * 1024 * 1024

LRU_X, LRU_Z, M_X, M_Z, XA_Q, XA_Z, GATE = 0, 4, 8, 12, 16, 20, 24
IN_GROUPS = 9
IN_SCALE = (1.0, 0.5, 1.0, 0.5, 1.0, 0.5, 0.5, 0.5, 0.5)
BRANCH_SCALE = 0.5
STAGE_COLS = 2 * MXU_COLS
(V_NORM_G, V_LRU_CONV_B, V_LRU_B_A, V_LRU_B_X, V_LRU_LAMBDA, V_M_CONV_B, V_M_B_O, V_M_NORM_G, V_M_SKIP,
 V_FINAL_G, V_B_IF) = range(11)
V_LRU_CONV_W = 11
V_M_CONV_W = V_LRU_CONV_W + CONV_WIDTH
V_B_IN = V_M_CONV_W + CONV_WIDTH
N_VEC_ROWS = V_B_IN + IN_GROUPS
HW_LRU_A, HW_LRU_X, HW_M_Q, HW_M_K, HW_M_V, HW_M_O = range(6)
HEAD_W_SCALE = (1.0, 1.0, 1.0, 1.0, 1.0, 0.5)
ROW_INPUTS = ((V_NORM_G, 1.0), (V_LRU_CONV_B, 0.5), (V_LRU_B_A, 0.5), (V_LRU_B_X, 0.5), (V_LRU_LAMBDA, 1.0),
              (V_M_CONV_B, 0.5), (V_M_B_O, 0.5), (V_M_NORM_G, 1.0), (V_M_SKIP, 1.0), (V_FINAL_G, 1.0),
              (V_B_IF, 1.0))
CONV_SCALE = 0.5

BF16 = jnp.bfloat16
F32 = jnp.float32


def _dot(a, b):
    return jnp.dot(a, b, preferred_element_type=F32)


def _dot_nt(a, b):
    return lax.dot_general(a, b, (((1,), (1,)), ((), ())), preferred_element_type=F32)


def _tanh1(v):
    return jnp.tanh(v) + 1.0


def _log_sigmoid(x):
    return jnp.minimum(x, 0.0) - jnp.log1p(jnp.exp(-jnp.abs(x)))


def _rms_norm(x, g):
    ms = jnp.mean(x * x, axis=-1, keepdims=True)
    return x * lax.rsqrt(ms + EPS) * g


def _blk(j):
    return slice(j * HEAD_DIM, (j + 1) * HEAD_DIM)


def _prefix_rows(x, op, identity):
    n = x.shape[0]
    row = lax.broadcasted_iota(jnp.int32, x.shape, 0)
    k = 1
    while k < n:
        x = op(x, jnp.where(row >= k, pltpu.roll(x, k, 0), identity))
        k *= 2
    return x


def _lane_tile(x, n):
    reps = n // LANES
    return x if reps == 1 else jnp.concatenate([x] * reps, axis=-1)


def _lane_bcast(x, lane):
    return jnp.broadcast_to(x[:, lane:lane + 1], x.shape)


def _shift_rows(x3, k):
    sub = lax.broadcasted_iota(jnp.int32, (x3.shape[0] - 1,) + x3.shape[1:], 1)
    r = pltpu.roll(x3, k, 1)
    return jnp.where(sub < k, r[:-1], r[1:])


def _causal_conv(xb, tail, cw, cb):
    assert CONV_WIDTH == 4
    t, c = xb.shape
    groups = t // SUBLANES
    full = jnp.concatenate([tail, xb], axis=0).reshape(groups + 1, SUBLANES, c)
    w0, w1, w2, w3 = (cw[j:j + 1, :] for j in range(CONV_WIDTH))
    prev = _shift_rows(full, 1)
    tail3 = full[:1]
    tail_prev = pltpu.roll(tail3, 1, 1)
    older = jnp.concatenate([w1 * tail3 + w0 * tail_prev, w1 * full[1:] + w0 * prev], axis=0)
    acc = cb + w3 * full[1:] + w2 * prev + _shift_rows(older, 2)
    return acc.reshape(t, c)


def _lru_scan(a, b, h0):
    t, c = a.shape
    row = lax.broadcasted_iota(jnp.int32, (SUBLANES, c), 0)
    out = []
    for j in range(t // SUBLANES):
        aj = a[j * SUBLANES:(j + 1) * SUBLANES, :]
        bj = b[j * SUBLANES:(j + 1) * SUBLANES, :]
        for k in (1, 2, 4):
            valid = row >= k
            a_sh = pltpu.roll(aj, k, 0)
            b_sh = pltpu.roll(bj, k, 0)
            bj = bj + aj * jnp.where(valid, b_sh, 0.0)
            aj = aj * jnp.where(valid, a_sh, 1.0)
        hj = bj + aj * h0
        out.append(hj)
        h0 = jnp.broadcast_to(hj[SUBLANES - 1:SUBLANES, :], (SUBLANES, c))
    return jnp.concatenate(out, axis=0), h0


def _zip_stages(gens):
    gens = list(gens)
    while gens:
        alive = []
        for g in gens:
            try:
                next(g)
                alive.append(g)
            except StopIteration:
                pass
        gens = alive
        if gens:
            yield


def _interleave(gens, skew=1):
    pending = list(gens)
    active = []
    rnd = 0
    while pending or active:
        while pending and (len(gens) - len(pending)) * skew <= rnd:
            active.append(pending.pop(0))
        alive = []
        for g in active:
            try:
                next(g)
                alive.append(g)
            except StopIteration:
                pass
        active = alive
        rnd += 1


def _scaled(v, scale):
    return v if scale == 1.0 else v * scale


def _prep_kernel(mem_ref, g_ref, w_ref,
                 r0, r1, r2, r3, r4, r5, r6, r7, r8, r9, r10, lcw_ref, mcw_ref, bin_ref,
                 h0, h1, h2, h3, h4, h5,
                 kv_ref, vec_ref, hw_ref):
    nb = mem_ref.shape[0]
    mn = _rms_norm(mem_ref[...].reshape(nb * N_MEM, D_MODEL), g_ref[...]).astype(BF16)
    w = w_ref[...].astype(BF16)
    heads = [_dot(mn, w[:, _blk(j)]) for j in range(N_HEADS)]

    @pl.when(pl.program_id(0) == 0)
    def _():
        for j in range(N_HEADS):
            for b in range(nb):
                kv_ref[b, j] = heads[j][b * N_MEM:(b + 1) * N_MEM, :].T.astype(BF16)

    @pl.when(pl.program_id(0) == 1)
    def _():
        for j in range(N_HEADS):
            kv_ref[:, j] = heads[j].astype(BF16).reshape(nb, N_MEM, HEAD_DIM)

    @pl.when(pl.program_id(0) == 0)
    def _():
        for (row, scale), ref in zip(ROW_INPUTS, (r0, r1, r2, r3, r4, r5, r6, r7, r8, r9, r10)):
            vec_ref[row:row + 1, :] = _scaled(ref[...], scale)
        vec_ref[V_LRU_CONV_W:V_LRU_CONV_W + CONV_WIDTH, :] = lcw_ref[0] * CONV_SCALE
        vec_ref[V_M_CONV_W:V_M_CONV_W + CONV_WIDTH, :] = mcw_ref[0] * CONV_SCALE
        for g in range(IN_GROUPS):
            vec_ref[V_B_IN + g:V_B_IN + g + 1, :] = _scaled(bin_ref[g:g + 1, :], IN_SCALE[g])
        for i, ref in enumerate((h0, h1, h2, h3, h4, h5)):
            hw_ref[i] = _scaled(ref[0], HEAD_W_SCALE[i]).astype(BF16)


def _load_weights(win_hbm, wbr_hbm, wout_hbm, win_s, wbr_s, wout_s, stage_s, sem):
    per = STAGE_COLS // MXU_COLS
    chunks = []
    for c in range(win_hbm.shape[-1] // STAGE_COLS):
        chunks.append((win_hbm.at[0, :, pl.ds(c * STAGE_COLS, STAGE_COLS)], win_s, c * per,
                       IN_SCALE[c * STAGE_COLS // D_MODEL]))
    for n in range(N_BRANCH):
        for c in range(D_MODEL // STAGE_COLS):
            chunks.append((wbr_hbm.at[0, n, :, pl.ds(c * STAGE_COLS, STAGE_COLS)], wbr_s.at[n], c * per,
                           BRANCH_SCALE))
    for c in range(D_MODEL // STAGE_COLS):
        chunks.append((wout_hbm.at[0, :, pl.ds(c * STAGE_COLS, STAGE_COLS)], wout_s, c * per, 1.0))

    def copy(i):
        return pltpu.make_async_copy(chunks[i][0], stage_s.at[i % 2], sem.at[i % 2])

    copy(0).start()
    for i, (_, dest, blk0, scale) in enumerate(chunks):
        if i + 1 < len(chunks):
            copy(i + 1).start()
        copy(i).wait()
        for j in range(per):
            w = stage_s[i % 2, :, j * MXU_COLS:(j + 1) * MXU_COLS]
            dest[blk0 + j] = (w if scale == 1.0 else w * scale).astype(BF16)


def _layer_kernel(x_ref, vec_ref, hw_ref, wif_ref, kv_ref, win_hbm, wbr_hbm, wout_hbm,
                  o_ref,
                  win_s, wbr_s, wout_s, stage_s, dma_sem,
                  hb_s, ltail_s, mtail_s, hcar_s, q_s, k_s, v_s, og_s, xc_s, g_s, st_s, m_s, y_s, acc_s, mg_s,
                  *, nb, tr):
    L = tr
    R = nb * tr
    rows = [slice(b * tr, (b + 1) * tr) for b in range(nb)]

    @pl.when(pl.program_id(0) == 0)
    def _():
        ltail_s[...] = jnp.zeros(ltail_s.shape, F32)
        mtail_s[...] = jnp.zeros(mtail_s.shape, F32)
        hcar_s[...] = jnp.zeros(hcar_s.shape, F32)
        st_s[...] = jnp.zeros(st_s.shape, F32)
        m_s[...] = jnp.zeros(m_s.shape, F32)
        _load_weights(win_hbm, wbr_hbm, wout_hbm, win_s, wbr_s, wout_s, stage_s, dma_sem)

    def vec(i, sl=slice(None)):
        return vec_ref[i:i + 1, sl]

    hb_s[...] = _rms_norm(x_ref[...].reshape(R, D_MODEL), vec(V_NORM_G)).astype(BF16)

    def proj(j):
        return _dot(hb_s[...], win_s[j]) + vec(V_B_IN + j // N_HEADS, _blk(j % N_HEADS))

    def conv(x, tail_ref, cw_row, cb_row, sl):
        cw = vec_ref[cw_row:cw_row + CONV_WIDTH, sl]
        parts = []
        for b in range(nb):
            xb = x[rows[b], :]
            parts.append(_causal_conv(xb, tail_ref[b, :, sl], cw, vec(cb_row, sl)))
            tail_ref[b, :, sl] = xb[tr - SUBLANES:, :]
        return jnp.concatenate(parts, axis=0)


    def lru_block(g):
        sl = _blk(g)
        px = proj(LRU_X + g)
        yield
        u = conv(px, ltail_s, V_LRU_CONV_W, V_LRU_CONV_B, sl)
        ub = u.astype(BF16)
        yield
        r_pre = _dot(ub, hw_ref[HW_LRU_A, g]) + vec(V_LRU_B_A, sl)
        i_pre = _dot(ub, hw_ref[HW_LRU_X, g]) + vec(V_LRU_B_X, sl)
        yield
        zh = proj(LRU_Z + g)
        ls_c = (0.5 * LRU_C) * _log_sigmoid(vec(V_LRU_LAMBDA, sl))
        yield
        for b in range(nb):
            h0 = hcar_s[b, :, sl]
            for c in range(tr // ROW_CHUNK):
                rs = slice(b * tr + c * ROW_CHUNK, b * tr + (c + 1) * ROW_CHUNK)
                log_a = ls_c * _tanh1(r_pre[rs, :])
                a = jnp.exp(log_a)
                om = -jnp.tanh(log_a) * (1.0 + a * a)
                mult = jnp.where(om > 0.0, om * lax.rsqrt(om), 0.0)
                bb = mult * _tanh1(i_pre[rs, :]) * u[rs, :]
                h, h0 = _lru_scan(a, bb, h0)
                y_s[0, rs, sl] = (h * (zh[rs, :] * _tanh1(zh[rs, :]))).astype(BF16)
                yield
            hcar_s[b, :, sl] = h0

    def xattn_head(h):
        sl = _blk(h)
        q = proj(XA_Q + h).astype(BF16)
        yield
        scores = [_dot(q[rows[b], :], kv_ref[b, h]) * (HEAD_DIM ** -0.5) for b in range(nb)]
        yield
        probs = []
        for s in scores:
            e = jnp.exp(s - jnp.max(s, axis=-1, keepdims=True))
            probs.append((e * (1.0 / jnp.sum(e, axis=-1, keepdims=True))).astype(BF16))
            yield
        o = jnp.concatenate([_dot(probs[b], kv_ref[b, N_HEADS + h]) for b in range(nb)], axis=0)
        yield
        zh = proj(XA_Z + h)
        yield
        y_s[2, :, sl] = (o * (zh * _tanh1(zh))).astype(BF16)

    def mlstm_front(h):
        sl = _blk(h)
        xm = proj(M_X + h)
        yield
        ch = conv(xm, mtail_s, V_M_CONV_W, V_M_CONV_B, sl)
        xc = ch * _tanh1(ch)
        xc_s[:, sl] = xc
        xcb = xc.astype(BF16)
        xmb = xm.astype(BF16)
        yield
        q_s[:, sl] = _dot(xcb, hw_ref[HW_M_Q, h]).astype(BF16)
        k_s[:, sl] = _dot(xcb, hw_ref[HW_M_K, h]).astype(BF16)
        yield
        v_s[:, sl] = _dot(xmb, hw_ref[HW_M_V, h]).astype(BF16)
        og_s[:, sl] = _tanh1(_dot(xmb, hw_ref[HW_M_O, h]) + vec(V_M_B_O, sl))
        fronts_done.append(h)

    fronts_done = []
    gate_terms = []

    def mlstm_gates():
        while len(fronts_done) < N_HEADS:
            yield
        g_s[...] = (_dot(q_s[...], wif_ref[0:D_MODEL, :])
                    + _dot(k_s[...], wif_ref[D_MODEL:2 * D_MODEL, :])
                    + _dot(v_s[...], wif_ref[2 * D_MODEL:3 * D_MODEL, :])
                    + vec(V_B_IF, slice(0, 2 * LANES)))
        yield
        for b in range(nb):
            bcum = _prefix_rows(_log_sigmoid(g_s[rows[b], LANES:2 * LANES]), jnp.add, 0.0)
            gq = g_s[rows[b], 0:LANES] - bcum
            cmax = _prefix_rows(gq, jnp.maximum, -jnp.inf)
            gate_terms.append((bcum, gq, cmax, gq.T))
            yield

    causal = (lax.broadcasted_iota(jnp.int32, (L, L), 0) >= lax.broadcasted_iota(jnp.int32, (L, L), 1))

    def mlstm_chunk(b, h, terms, out):
        sl = _blk(h)
        bcum, gq, cmax, gq_t = terms
        b_rep = _lane_bcast(bcum, h)
        gq_rep = _lane_bcast(gq, h)
        gq_row = gq_t[h:h + 1, :]
        m_prev = m_s[b, h, 0:1, :]
        mm = jnp.maximum(m_prev, _lane_bcast(cmax, h))
        mm_last = mm[L - 1:L, :]
        p = jnp.exp(jnp.where(causal, gq_row - _lane_tile(mm, L), -jnp.inf))
        sc = jnp.exp(m_prev - mm)
        qh = q_s[rows[b], sl]
        kh = (k_s[rows[b], sl] * (HEAD_DIM ** -0.5)).astype(BF16)
        vh = v_s[rows[b], sl]
        kt = kh.T
        st = st_s[b, h]
        stb = st.astype(BF16)
        yield
        qk = _dot(qh, jnp.concatenate([kt, stb[:, HEAD_DIM:]], axis=1))
        qc = _dot(qh, stb[:, :HEAD_DIM])
        yield
        s = qk[:, :L] * p
        den = jnp.sum(s, axis=-1, keepdims=True) + sc * qk[:, L:]
        inv = 0.5 / jnp.maximum(jnp.abs(den), jnp.exp(-(b_rep + mm)))
        wk = jnp.exp(gq_rep - mm_last)
        wkv = jnp.concatenate([_lane_tile(wk, HEAD_DIM) * vh.astype(F32), wk], axis=1).astype(BF16)
        yield
        num = _dot(s.astype(BF16), vh) + _lane_tile(sc, HEAD_DIM) * qc
        upd = _dot(kt, wkv)
        yield
        decay = sc[L - 1:L, :]
        st_s[b, h] = _lane_tile(decay, HEAD_DIM + LANES) * st + upd
        m_s[b, h] = jnp.broadcast_to(b_rep[L - 1:L, :] + mm_last, (SUBLANES, LANES))
        out[b] = num * _lane_tile(inv, HEAD_DIM)

    def mlstm_head(h, terms):
        sl = _blk(h)
        hparts = [None] * nb
        yield from _zip_stages([mlstm_chunk(b, h, terms[b], hparts) for b in range(nb)])
        hh = og_s[:, sl] * jnp.concatenate(hparts, axis=0)
        mu = jnp.mean(hh, axis=-1, keepdims=True)
        dl = hh - mu
        var = jnp.mean(dl * dl, axis=-1, keepdims=True)
        hn = dl * lax.rsqrt(var + EPS) * vec(V_M_NORM_G, sl)
        yield
        zh = proj(M_Z + h)
        yield
        y = (hn + vec(V_M_SKIP, sl) * xc_s[:, sl]) * (zh * _tanh1(zh))
        y_s[1, :, sl] = y.astype(BF16)

    def branch_out(n, j):
        gate2 = _tanh1(proj(GATE + N_HEADS * n + j))
        yield
        return gate2 * _dot(y_s[n], wbr_s[n, j])

    def branch_pair(j):
        first = yield from branch_out(0, j)
        yield
        second = yield from branch_out(2, j)
        acc_s[:, _blk(j)] = first + second

    def branch_last(j):
        last = yield from branch_out(1, j)
        mg_s[:, _blk(j)] = (acc_s[:, _blk(j)] + last).astype(BF16)

    heads = range(N_HEADS)
    _interleave([f(i) for i in heads for f in (lru_block, mlstm_front, xattn_head)] + [mlstm_gates()])
    _interleave([f(i) for i in heads for f in (lambda i: mlstm_head(i, gate_terms), branch_pair)], skew=0)
    _interleave([branch_last(j) for j in heads])
    ssq = None
    for j in range(N_HEADS):
        sl = _blk(j)
        xo = x_ref[:, :, sl].reshape(R, HEAD_DIM) + _dot(mg_s[...], wout_s[j])
        o_ref[:, :, sl] = xo.reshape(nb, tr, HEAD_DIM)
        part = jnp.sum(xo * xo, axis=-1, keepdims=True)
        ssq = part if ssq is None else ssq + part
    scale = lax.rsqrt(ssq * (1.0 / D_MODEL) + EPS)
    o_ref[...] = (o_ref[...].reshape(R, D_MODEL) * scale * vec(V_FINAL_G)).reshape(nb, tr, D_MODEL)


def _resident(a):
    nd = a.ndim
    return pl.BlockSpec(a.shape, lambda i: (0,) * nd, pipeline_mode=pl.Buffered(1))


def kernel(x, mem, norm_g, w_in, b_in, lru_conv_w, lru_conv_b, lru_w_a, lru_b_a, lru_w_x, lru_b_x,
           lru_lambda, m_conv_w, m_conv_b, m_w_q, m_w_k, m_w_v, m_w_o, m_b_o, m_w_if, m_b_if,
           m_norm_g, m_skip, mem_norm_g, w_mem_kv, w_branch, w_out, final_norm_g):
    B, S, D = x.shape
    assert D == D_MODEL and S % TIME_TILE == 0 and norm_g.shape[0] == 1
    tr = TIME_TILE
    H = N_HEADS
    pad_lanes = lambda a: jnp.pad(a, ((0, 0), (0, LANES - H)))
    w_if = jnp.concatenate([pad_lanes(m_w_if[0][:, :H]), pad_lanes(m_w_if[0][:, H:])], axis=1).astype(BF16)
    b_if2 = m_b_if[0].reshape(1, 2 * H).astype(F32)
    b_if = jnp.pad(jnp.concatenate([pad_lanes(b_if2[:, :H]), pad_lanes(b_if2[:, H:])], axis=1),
                   ((0, 0), (0, D - 2 * LANES)))

    rows_in = [norm_g, lru_conv_b, lru_b_a, lru_b_x, lru_lambda, m_conv_b, m_b_o, m_norm_g, m_skip,
               final_norm_g.reshape(1, D), b_if]
    prep_in = rows_in + [lru_conv_w, m_conv_w, b_in.reshape(IN_GROUPS, D),
                         lru_w_a, lru_w_x, m_w_q, m_w_k, m_w_v, m_w_o]
    whole = lambda a: pl.BlockSpec(a.shape, lambda c: (0,) * a.ndim)
    kv, vecs, head_w = pl.pallas_call(
        _prep_kernel,
        grid=(2,),
        in_specs=[whole(mem), whole(mem_norm_g), pl.BlockSpec((None, D, D), lambda c: (0, 0, c))]
        + [whole(a) for a in prep_in],
        out_specs=[pl.BlockSpec((B, H, N_MEM, HEAD_DIM), lambda c: (0, c, 0, 0)),
                   pl.BlockSpec((N_VEC_ROWS, D), lambda c: (0, 0)),
                   pl.BlockSpec((len(HEAD_W_SCALE), H, HEAD_DIM, HEAD_DIM), lambda c: (0, 0, 0, 0))],
        out_shape=[jax.ShapeDtypeStruct((B, 2 * H, N_MEM, HEAD_DIM), BF16),
                   jax.ShapeDtypeStruct((N_VEC_ROWS, D), F32),
                   jax.ShapeDtypeStruct((len(HEAD_W_SCALE), H, HEAD_DIM, HEAD_DIM), BF16)],
        compiler_params=pltpu.CompilerParams(dimension_semantics=("arbitrary",),
                                             vmem_limit_bytes=VMEM_LIMIT_BYTES),
        name="prep",
    )(mem, mem_norm_g, w_mem_kv, *prep_in)
    weights = [vecs, head_w, w_if, kv]
    hbm_weights = [w_in, w_branch, w_out]

    R = B * tr
    tok = pl.BlockSpec((B, tr, D), lambda i: (0, i, 0))
    nblk = D // MXU_COLS
    scratch = [pltpu.VMEM((IN_GROUPS * nblk, D, MXU_COLS), BF16),
               pltpu.VMEM((N_BRANCH, nblk, D, MXU_COLS), BF16),
               pltpu.VMEM((nblk, D, MXU_COLS), BF16),
               pltpu.VMEM((2, D, STAGE_COLS), F32),
               pltpu.SemaphoreType.DMA((2,)),
               pltpu.VMEM((R, D), BF16),
               pltpu.VMEM((B, SUBLANES, D), F32),
               pltpu.VMEM((B, SUBLANES, D), F32),
               pltpu.VMEM((B, SUBLANES, D), F32),
               pltpu.VMEM((R, D), BF16),
               pltpu.VMEM((R, D), BF16),
               pltpu.VMEM((R, D), BF16),
               pltpu.VMEM((R, D), F32),
               pltpu.VMEM((R, D), F32),
               pltpu.VMEM((R, 2 * LANES), F32),
               pltpu.VMEM((B, H, HEAD_DIM, HEAD_DIM + LANES), F32),
               pltpu.VMEM((B, H, SUBLANES, LANES), F32),
               pltpu.VMEM((N_BRANCH, R, D), BF16),
               pltpu.VMEM((R, D), F32),
               pltpu.VMEM((R, D), BF16)]
    return pl.pallas_call(
        functools.partial(_layer_kernel, nb=B, tr=tr),
        grid=(S // tr,),
        in_specs=([tok] + [_resident(w) for w in weights]
                  + [pl.BlockSpec(memory_space=pl.ANY)] * len(hbm_weights)),
        out_specs=tok,
        out_shape=jax.ShapeDtypeStruct((B, S, D), F32),
        scratch_shapes=scratch,
        compiler_params=pltpu.CompilerParams(
            dimension_semantics=("arbitrary",),
            vmem_limit_bytes=VMEM_LIMIT_BYTES),
        name="hybrid_layer",
    )(x, *weights, *hbm_weights)
```

```python
import functools

import jax
import jax.numpy as jnp
from jax import lax
from jax.experimental import pallas as pl
from jax.experimental.pallas import tpu as pltpu

D_MODEL = 1024
N_HEADS = 4
HEAD_DIM = D_MODEL // N_HEADS
N_MEM = 256
N_BRANCH = 3
EPS = 1e-6
CONV_WIDTH = 4
LRU_C = 8.0
LANES = 128
SUBLANES = 8
MXU_COLS = 256
TIME_TILE = 128
ROW_CHUNK = 32
VMEM_LIMIT_BYTES = 60 * 1024 * 1024

LRU_X, LRU_Z, M_X, M_Z, XA_Q, XA_Z, GATE = 0, 4, 8, 12, 16, 20, 24
IN_GROUPS = 9
IN_SCALE = (1.0, 0.5, 1.0, 0.5, 1.0, 0.5, 0.5, 0.5, 0.5)
BRANCH_SCALE = 0.5
STAGE_COLS = 2 * MXU_COLS
STAGE_SLOTS = 4
(V_NORM_G, V_LRU_CONV_B, V_LRU_B_A, V_LRU_B_X, V_LRU_LAMBDA, V_M_CONV_B, V_M_B_O, V_M_NORM_G, V_M_SKIP,
 V_FINAL_G, V_B_IF) = range(11)
V_LRU_CONV_W = 11
V_M_CONV_W = V_LRU_CONV_W + CONV_WIDTH
V_B_IN = V_M_CONV_W + CONV_WIDTH
N_VEC_ROWS = V_B_IN + IN_GROUPS
HW_LRU_A, HW_LRU_X, HW_M_Q, HW_M_K, HW_M_V, HW_M_O = range(6)
HEAD_W_SCALE = (1.0, 1.0, 1.0, 1.0, 1.0, 0.5)
ROW_INPUTS = ((V_NORM_G, 1.0), (V_LRU_CONV_B, 0.5), (V_LRU_B_A, 0.5), (V_LRU_B_X, 0.5), (V_LRU_LAMBDA, 1.0),
              (V_M_CONV_B, 0.5), (V_M_B_O, 0.5), (V_M_NORM_G, 1.0), (V_M_SKIP, 1.0), (V_FINAL_G, 1.0),
              (V_B_IF, 1.0))
CONV_SCALE = 0.5

BF16 = jnp.bfloat16
F32 = jnp.float32


def _dot(a, b):
    return jnp.dot(a, b, preferred_element_type=F32)


def _dot_nt(a, b):
    return lax.dot_general(a, b, (((1,), (1,)), ((), ())), preferred_element_type=F32)


def _tanh1(v):
    return jnp.tanh(v) + 1.0


def _log_sigmoid(x):
    return jnp.minimum(x, 0.0) - jnp.log1p(jnp.exp(-jnp.abs(x)))


def _rms_norm(x, g):
    ms = jnp.mean(x * x, axis=-1, keepdims=True)
    return x * lax.rsqrt(ms + EPS) * g


def _blk(j):
    return slice(j * HEAD_DIM, (j + 1) * HEAD_DIM)


def _prefix_rows(x, op, identity):
    n = x.shape[0]
    row = lax.broadcasted_iota(jnp.int32, x.shape, 0)
    k = 1
    while k < n:
        x = op(x, jnp.where(row >= k, pltpu.roll(x, k, 0), identity))
        k *= 2
    return x


def _lane_tile(x, n):
    reps = n // LANES
    return x if reps == 1 else jnp.concatenate([x] * reps, axis=-1)


def _lane_bcast(x, lane):
    return jnp.broadcast_to(x[:, lane:lane + 1], x.shape)


def _shift_rows(x3, k):
    sub = lax.broadcasted_iota(jnp.int32, (x3.shape[0] - 1,) + x3.shape[1:], 1)
    r = pltpu.roll(x3, k, 1)
    return jnp.where(sub < k, r[:-1], r[1:])


def _causal_conv(xb, tail, cw, cb):
    assert CONV_WIDTH == 4
    t, c = xb.shape
    groups = t // SUBLANES
    full = jnp.concatenate([tail, xb], axis=0).reshape(groups + 1, SUBLANES, c)
    w0, w1, w2, w3 = (cw[j:j + 1, :] for j in range(CONV_WIDTH))
    prev = _shift_rows(full, 1)
    tail3 = full[:1]
    tail_prev = pltpu.roll(tail3, 1, 1)
    older = jnp.concatenate([w1 * tail3 + w0 * tail_prev, w1 * full[1:] + w0 * prev], axis=0)
    acc = cb + w3 * full[1:] + w2 * prev + _shift_rows(older, 2)
    return acc.reshape(t, c)


def _lru_scan(a, b, h0):
    t, c = a.shape
    row = lax.broadcasted_iota(jnp.int32, (SUBLANES, c), 0)
    out = []
    for j in range(t // SUBLANES):
        aj = a[j * SUBLANES:(j + 1) * SUBLANES, :]
        bj = b[j * SUBLANES:(j + 1) * SUBLANES, :]
        for k in (1, 2, 4):
            valid = row >= k
            a_sh = pltpu.roll(aj, k, 0)
            b_sh = pltpu.roll(bj, k, 0)
            bj = bj + aj * jnp.where(valid, b_sh, 0.0)
            aj = aj * jnp.where(valid, a_sh, 1.0)
        hj = bj + aj * h0
        out.append(hj)
        h0 = jnp.broadcast_to(hj[SUBLANES - 1:SUBLANES, :], (SUBLANES, c))
    return jnp.concatenate(out, axis=0), h0


def _zip_stages(gens):
    gens = list(gens)
    while gens:
        alive = []
        for g in gens:
            try:
                next(g)
                alive.append(g)
            except StopIteration:
                pass
        gens = alive
        if gens:
            yield


def _interleave(gens, skew=1):
    pending = list(gens)
    active = []
    rnd = 0
    while pending or active:
        while pending and (len(gens) - len(pending)) * skew <= rnd:
            active.append(pending.pop(0))
        alive = []
        for g in active:
            try:
                next(g)
                alive.append(g)
            except StopIteration:
                pass
        active = alive
        rnd += 1


def _scaled(v, scale):
    return v if scale == 1.0 else v * scale


def _prep_kernel(mem_ref, g_ref, w_ref,
                 r0, r1, r2, r3, r4, r5, r6, r7, r8, r9, r10, lcw_ref, mcw_ref, bin_ref,
                 h0, h1, h2, h3, h4, h5,
                 kv_ref, vec_ref, hw_ref):
    nb = mem_ref.shape[0]
    mn = _rms_norm(mem_ref[...].reshape(nb * N_MEM, D_MODEL), g_ref[...]).astype(BF16)
    w = w_ref[...].astype(BF16)
    for j in range(N_HEADS):
        kv_ref[:, j] = _dot(mn, w[:, _blk(j)]).astype(BF16).reshape(nb, N_MEM, HEAD_DIM)

    @pl.when(pl.program_id(0) == 0)
    def _():
        for (row, scale), ref in zip(ROW_INPUTS, (r0, r1, r2, r3, r4, r5, r6, r7, r8, r9, r10)):
            vec_ref[row:row + 1, :] = _scaled(ref[...], scale)
        vec_ref[V_LRU_CONV_W:V_LRU_CONV_W + CONV_WIDTH, :] = lcw_ref[0] * CONV_SCALE
        vec_ref[V_M_CONV_W:V_M_CONV_W + CONV_WIDTH, :] = mcw_ref[0] * CONV_SCALE
        for g in range(IN_GROUPS):
            vec_ref[V_B_IN + g:V_B_IN + g + 1, :] = _scaled(bin_ref[g:g + 1, :], IN_SCALE[g])
        for i, ref in enumerate((h0, h1, h2, h3, h4, h5)):
            hw_ref[i] = _scaled(ref[0], HEAD_W_SCALE[i]).astype(BF16)


def _load_weights(win_hbm, wbr_hbm, wout_hbm, win_s, wbr_s, wout_s, stage_s, sem):
    per = STAGE_COLS // MXU_COLS
    chunks = []
    for c in range(win_hbm.shape[-1] // STAGE_COLS):
        chunks.append((win_hbm.at[0, :, pl.ds(c * STAGE_COLS, STAGE_COLS)], win_s, c * per,
                       IN_SCALE[c * STAGE_COLS // D_MODEL]))
    for n in range(N_BRANCH):
        for c in range(D_MODEL // STAGE_COLS):
            chunks.append((wbr_hbm.at[0, n, :, pl.ds(c * STAGE_COLS, STAGE_COLS)], wbr_s.at[n], c * per,
                           BRANCH_SCALE))
    for c in range(D_MODEL // STAGE_COLS):
        chunks.append((wout_hbm.at[0, :, pl.ds(c * STAGE_COLS, STAGE_COLS)], wout_s, c * per, 1.0))

    def copy(i):
        return pltpu.make_async_copy(chunks[i][0], stage_s.at[i % STAGE_SLOTS], sem.at[i % STAGE_SLOTS])

    ahead = STAGE_SLOTS - 1
    for i in range(min(ahead, len(chunks))):
        copy(i).start()
    for i, (_, dest, blk0, scale) in enumerate(chunks):
        if i + ahead < len(chunks):
            copy(i + ahead).start()
        copy(i).wait()
        for j in range(per):
            w = stage_s[i % STAGE_SLOTS, :, j * MXU_COLS:(j + 1) * MXU_COLS]
            dest[blk0 + j] = (w if scale == 1.0 else w * scale).astype(BF16)


def _layer_kernel(x_ref, vec_ref, hw_ref, wif_ref, kv_ref, win_hbm, wbr_hbm, wout_hbm,
                  o_ref,
                  win_s, wbr_s, wout_s, stage_s, dma_sem,
                  hb_s, ltail_s, mtail_s, hcar_s, q_s, k_s, v_s, og_s, xc_s, g_s, st_s, m_s, y_s, acc_s, mg_s,
                  *, nb, tr):
    L = tr
    R = nb * tr
    rows = [slice(b * tr, (b + 1) * tr) for b in range(nb)]

    @pl.when(pl.program_id(0) == 0)
    def _():
        ltail_s[...] = jnp.zeros(ltail_s.shape, F32)
        mtail_s[...] = jnp.zeros(mtail_s.shape, F32)
        hcar_s[...] = jnp.zeros(hcar_s.shape, F32)
        st_s[...] = jnp.zeros(st_s.shape, F32)
        m_s[...] = jnp.zeros(m_s.shape, F32)
        _load_weights(win_hbm, wbr_hbm, wout_hbm, win_s, wbr_s, wout_s, stage_s, dma_sem)

    def vec(i, sl=slice(None)):
        return vec_ref[i:i + 1, sl]

    hb_s[...] = _rms_norm(x_ref[...].reshape(R, D_MODEL), vec(V_NORM_G)).astype(BF16)

    def proj(j):
        return _dot(hb_s[...], win_s[j]) + vec(V_B_IN + j // N_HEADS, _blk(j % N_HEADS))

    def conv(x, tail_ref, cw_row, cb_row, sl):
        cw = vec_ref[cw_row:cw_row + CONV_WIDTH, sl]
        parts = []
        for b in range(nb):
            xb = x[rows[b], :]
            parts.append(_causal_conv(xb, tail_ref[b, :, sl], cw, vec(cb_row, sl)))
            tail_ref[b, :, sl] = xb[tr - SUBLANES:, :]
        return jnp.concatenate(parts, axis=0)


    def lru_block(g):
        sl = _blk(g)
        px = proj(LRU_X + g)
        yield
        u = conv(px, ltail_s, V_LRU_CONV_W, V_LRU_CONV_B, sl)
        ub = u.astype(BF16)
        yield
        r_pre = _dot(ub, hw_ref[HW_LRU_A, g]) + vec(V_LRU_B_A, sl)
        i_pre = _dot(ub, hw_ref[HW_LRU_X, g]) + vec(V_LRU_B_X, sl)
        yield
        zh = proj(LRU_Z + g)
        ls_c = (0.5 * LRU_C) * _log_sigmoid(vec(V_LRU_LAMBDA, sl))
        yield
        for b in range(nb):
            h0 = hcar_s[b, :, sl]
            for c in range(tr // ROW_CHUNK):
                rs = slice(b * tr + c * ROW_CHUNK, b * tr + (c + 1) * ROW_CHUNK)
                log_a = ls_c * _tanh1(r_pre[rs, :])
                a = jnp.exp(log_a)
                om = -jnp.tanh(log_a) * (1.0 + a * a)
                mult = jnp.where(om > 0.0, om * lax.rsqrt(om), 0.0)
                bb = mult * _tanh1(i_pre[rs, :]) * u[rs, :]
                h, h0 = _lru_scan(a, bb, h0)
                y_s[0, rs, sl] = (h * (zh[rs, :] * _tanh1(zh[rs, :]))).astype(BF16)
                yield
            hcar_s[b, :, sl] = h0

    def xattn_head(h):
        sl = _blk(h)
        q = proj(XA_Q + h).astype(BF16)
        yield
        scores = [_dot_nt(q[rows[b], :], kv_ref[b, h]) * (HEAD_DIM ** -0.5) for b in range(nb)]
        yield
        probs = []
        for s in scores:
            e = jnp.exp(s - jnp.max(s, axis=-1, keepdims=True))
            probs.append((e * (1.0 / jnp.sum(e, axis=-1, keepdims=True))).astype(BF16))
            yield
        o = jnp.concatenate([_dot(probs[b], kv_ref[b, N_HEADS + h]) for b in range(nb)], axis=0)
        yield
        zh = proj(XA_Z + h)
        yield
        y_s[2, :, sl] = (o * (zh * _tanh1(zh))).astype(BF16)

    def mlstm_front(h):
        sl = _blk(h)
        xm = proj(M_X + h)
        yield
        ch = conv(xm, mtail_s, V_M_CONV_W, V_M_CONV_B, sl)
        xc = ch * _tanh1(ch)
        xc_s[:, sl] = xc
        xcb = xc.astype(BF16)
        xmb = xm.astype(BF16)
        yield
        q_s[:, sl] = _dot(xcb, hw_ref[HW_M_Q, h]).astype(BF16)
        k_s[:, sl] = _dot(xcb, hw_ref[HW_M_K, h]).astype(BF16)
        yield
        v_s[:, sl] = _dot(xmb, hw_ref[HW_M_V, h]).astype(BF16)
        og_s[:, sl] = _tanh1(_dot(xmb, hw_ref[HW_M_O, h]) + vec(V_M_B_O, sl))
        fronts_done.append(h)

    fronts_done = []
    gate_terms = []

    def mlstm_gates():
        while len(fronts_done) < N_HEADS:
            yield
        g_s[...] = (_dot(q_s[...], wif_ref[0:D_MODEL, :])
                    + _dot(k_s[...], wif_ref[D_MODEL:2 * D_MODEL, :])
                    + _dot(v_s[...], wif_ref[2 * D_MODEL:3 * D_MODEL, :])
                    + vec(V_B_IF, slice(0, 2 * LANES)))
        yield
        for b in range(nb):
            bcum = _prefix_rows(_log_sigmoid(g_s[rows[b], LANES:2 * LANES]), jnp.add, 0.0)
            gq = g_s[rows[b], 0:LANES] - bcum
            cmax = _prefix_rows(gq, jnp.maximum, -jnp.inf)
            gate_terms.append((bcum, gq, cmax, gq.T))
            yield

    causal = (lax.broadcasted_iota(jnp.int32, (L, L), 0) >= lax.broadcasted_iota(jnp.int32, (L, L), 1))

    def mlstm_chunk(b, h, terms, out):
        sl = _blk(h)
        bcum, gq, cmax, gq_t = terms
        b_rep = _lane_bcast(bcum, h)
        gq_rep = _lane_bcast(gq, h)
        gq_row = gq_t[h:h + 1, :]
        m_prev = m_s[b, h, 0:1, :]
        mm = jnp.maximum(m_prev, _lane_bcast(cmax, h))
        mm_last = mm[L - 1:L, :]
        p = jnp.exp(jnp.where(causal, gq_row - _lane_tile(mm, L), -jnp.inf))
        sc = jnp.exp(m_prev - mm)
        qh = q_s[rows[b], sl]
        kh = (k_s[rows[b], sl] * (HEAD_DIM ** -0.5)).astype(BF16)
        vh = v_s[rows[b], sl]
        kt = kh.T
        st = st_s[b, h]
        stb = st.astype(BF16)
        yield
        qk = _dot(qh, jnp.concatenate([kt, stb[:, HEAD_DIM:]], axis=1))
        qc = _dot(qh, stb[:, :HEAD_DIM])
        yield
        s = qk[:, :L] * p
        den = jnp.sum(s, axis=-1, keepdims=True) + sc * qk[:, L:]
        inv = 0.5 / jnp.maximum(jnp.abs(den), jnp.exp(-(b_rep + mm)))
        wk = jnp.exp(gq_rep - mm_last)
        wkv = jnp.concatenate([_lane_tile(wk, HEAD_DIM) * vh.astype(F32), wk], axis=1).astype(BF16)
        yield
        num = _dot(s.astype(BF16), vh) + _lane_tile(sc, HEAD_DIM) * qc
        upd = _dot(kt, wkv)
        yield
        decay = sc[L - 1:L, :]
        st_s[b, h] = _lane_tile(decay, HEAD_DIM + LANES) * st + upd
        m_s[b, h] = jnp.broadcast_to(b_rep[L - 1:L, :] + mm_last, (SUBLANES, LANES))
        out[b] = num * _lane_tile(inv, HEAD_DIM)

    def mlstm_head(h, terms):
        sl = _blk(h)
        hparts = [None] * nb
        yield from _zip_stages([mlstm_chunk(b, h, terms[b], hparts) for b in range(nb)])
        hh = og_s[:, sl] * jnp.concatenate(hparts, axis=0)
        mu = jnp.mean(hh, axis=-1, keepdims=True)
        dl = hh - mu
        var = jnp.mean(dl * dl, axis=-1, keepdims=True)
        hn = dl * lax.rsqrt(var + EPS) * vec(V_M_NORM_G, sl)
        yield
        zh = proj(M_Z + h)
        yield
        y = (hn + vec(V_M_SKIP, sl) * xc_s[:, sl]) * (zh * _tanh1(zh))
        y_s[1, :, sl] = y.astype(BF16)

    def branch_out(n, j):
        gate2 = _tanh1(proj(GATE + N_HEADS * n + j))
        yield
        return gate2 * _dot(y_s[n], wbr_s[n, j])

    def branch_pair(j):
        first = yield from branch_out(0, j)
        yield
        second = yield from branch_out(2, j)
        acc_s[:, _blk(j)] = first + second

    def branch_last(j):
        last = yield from branch_out(1, j)
        mg_s[:, _blk(j)] = (acc_s[:, _blk(j)] + last).astype(BF16)

    heads = range(N_HEADS)
    _interleave([f(i) for i in heads for f in (lru_block, mlstm_front, xattn_head)] + [mlstm_gates()])
    _interleave([f(i) for i in heads for f in (lambda i: mlstm_head(i, gate_terms), branch_pair)], skew=0)
    _interleave([branch_last(j) for j in heads])
    ssq = None
    for j in range(N_HEADS):
        sl = _blk(j)
        xo = x_ref[:, :, sl].reshape(R, HEAD_DIM) + _dot(mg_s[...], wout_s[j])
        o_ref[:, :, sl] = xo.reshape(nb, tr, HEAD_DIM)
        part = jnp.sum(xo * xo, axis=-1, keepdims=True)
        ssq = part if ssq is None else ssq + part
    scale = lax.rsqrt(ssq * (1.0 / D_MODEL) + EPS)
    o_ref[...] = (o_ref[...].reshape(R, D_MODEL) * scale * vec(V_FINAL_G)).reshape(nb, tr, D_MODEL)


def _resident(a):
    nd = a.ndim
    return pl.BlockSpec(a.shape, lambda i: (0,) * nd, pipeline_mode=pl.Buffered(1))


def kernel(x, mem, norm_g, w_in, b_in, lru_conv_w, lru_conv_b, lru_w_a, lru_b_a, lru_w_x, lru_b_x,
           lru_lambda, m_conv_w, m_conv_b, m_w_q, m_w_k, m_w_v, m_w_o, m_b_o, m_w_if, m_b_if,
           m_norm_g, m_skip, mem_norm_g, w_mem_kv, w_branch, w_out, final_norm_g):
    B, S, D = x.shape
    assert D == D_MODEL and S % TIME_TILE == 0 and norm_g.shape[0] == 1
    tr = TIME_TILE
    H = N_HEADS
    pad_lanes = lambda a: jnp.pad(a, ((0, 0), (0, LANES - H)))
    w_if = jnp.concatenate([pad_lanes(m_w_if[0][:, :H]), pad_lanes(m_w_if[0][:, H:])], axis=1).astype(BF16)
    b_if2 = m_b_if[0].reshape(1, 2 * H).astype(F32)
    b_if = jnp.pad(jnp.concatenate([pad_lanes(b_if2[:, :H]), pad_lanes(b_if2[:, H:])], axis=1),
                   ((0, 0), (0, D - 2 * LANES)))

    rows_in = [norm_g, lru_conv_b, lru_b_a, lru_b_x, lru_lambda, m_conv_b, m_b_o, m_norm_g, m_skip,
               final_norm_g.reshape(1, D), b_if]
    prep_in = rows_in + [lru_conv_w, m_conv_w, b_in.reshape(IN_GROUPS, D),
                         lru_w_a, lru_w_x, m_w_q, m_w_k, m_w_v, m_w_o]
    whole = lambda a: pl.BlockSpec(a.shape, lambda c: (0,) * a.ndim)
    kv, vecs, head_w = pl.pallas_call(
        _prep_kernel,
        grid=(2,),
        in_specs=[whole(mem), whole(mem_norm_g), pl.BlockSpec((None, D, D), lambda c: (0, 0, c))]
        + [whole(a) for a in prep_in],
        out_specs=[pl.BlockSpec((B, H, N_MEM, HEAD_DIM), lambda c: (0, c, 0, 0)),
                   pl.BlockSpec((N_VEC_ROWS, D), lambda c: (0, 0)),
                   pl.BlockSpec((len(HEAD_W_SCALE), H, HEAD_DIM, HEAD_DIM), lambda c: (0, 0, 0, 0))],
        out_shape=[jax.ShapeDtypeStruct((B, 2 * H, N_MEM, HEAD_DIM), BF16),
                   jax.ShapeDtypeStruct((N_VEC_ROWS, D), F32),
                   jax.ShapeDtypeStruct((len(HEAD_W_SCALE), H, HEAD_DIM, HEAD_DIM), BF16)],
        compiler_params=pltpu.CompilerParams(dimension_semantics=("arbitrary",),
                                             vmem_limit_bytes=VMEM_LIMIT_BYTES),
        name="prep",
    )(mem, mem_norm_g, w_mem_kv, *prep_in)
    weights = [vecs, head_w, w_if, kv]
    hbm_weights = [w_in, w_branch, w_out]

    R = B * tr
    tok = pl.BlockSpec((B, tr, D), lambda i: (0, i, 0))
    nblk = D // MXU_COLS
    scratch = [pltpu.VMEM((IN_GROUPS * nblk, D, MXU_COLS), BF16),
               pltpu.VMEM((N_BRANCH, nblk, D, MXU_COLS), BF16),
               pltpu.VMEM((nblk, D, MXU_COLS), BF16),
               pltpu.VMEM((STAGE_SLOTS, D, STAGE_COLS), F32),
               pltpu.SemaphoreType.DMA((STAGE_SLOTS,)),
               pltpu.VMEM((R, D), BF16),
               pltpu.VMEM((B, SUBLANES, D), F32),
               pltpu.VMEM((B, SUBLANES, D), F32),
               pltpu.VMEM((B, SUBLANES, D), F32),
               pltpu.VMEM((R, D), BF16),
               pltpu.VMEM((R, D), BF16),
               pltpu.VMEM((R, D), BF16),
               pltpu.VMEM((R, D), F32),
               pltpu.VMEM((R, D), F32),
               pltpu.VMEM((R, 2 * LANES), F32),
               pltpu.VMEM((B, H, HEAD_DIM, HEAD_DIM + LANES), F32),
               pltpu.VMEM((B, H, SUBLANES, LANES), F32),
               pltpu.VMEM((N_BRANCH, R, D), BF16),
               pltpu.VMEM((R, D), F32),
               pltpu.VMEM((R, D), BF16)]
    return pl.pallas_call(
        functools.partial(_layer_kernel, nb=B, tr=tr),
        grid=(S // tr,),
        in_specs=([tok] + [_resident(w) for w in weights]
                  + [pl.BlockSpec(memory_space=pl.ANY)] * len(hbm_weights)),
        out_specs=tok,
        out_shape=jax.ShapeDtypeStruct((B, S, D), F32),
        scratch_shapes=scratch,
        compiler_params=pltpu.CompilerParams(
            dimension_semantics=("arbitrary",),
            vmem_limit_bytes=VMEM_LIMIT_BYTES),
        name="hybrid_layer",
    )(x, *weights, *hbm_weights)
```

```python
import functools

import jax
import jax.numpy as jnp
from jax import lax
from jax.experimental import pallas as pl
from jax.experimental.pallas import tpu as pltpu

D_MODEL = 1024
N_HEADS = 4
HEAD_DIM = D_MODEL // N_HEADS
N_MEM = 256
N_BRANCH = 3
EPS = 1e-6
CONV_WIDTH = 4
LRU_C = 8.0
LANES = 128
SUBLANES = 8
MXU_COLS = 256
TIME_TILE = 128
ROW_CHUNK = 32
VMEM_LIMIT_BYTES = 60 * 1024 * 1024

LRU_X, LRU_Z, M_X, M_Z, XA_Q, XA_Z, GATE = 0, 4, 8, 12, 16, 20, 24
IN_GROUPS = 9
IN_SCALE = (1.0, 0.5, 1.0, 0.5, 1.0, 0.5, 0.5, 0.5, 0.5)
BRANCH_SCALE = 0.5
STAGE_COLS = MXU_COLS
STAGE_SLOTS = 8
(V_NORM_G, V_LRU_CONV_B, V_LRU_B_A, V_LRU_B_X, V_LRU_LAMBDA, V_M_CONV_B, V_M_B_O, V_M_NORM_G, V_M_SKIP,
 V_FINAL_G, V_B_IF) = range(11)
V_LRU_CONV_W = 11
V_M_CONV_W = V_LRU_CONV_W + CONV_WIDTH
V_B_IN = V_M_CONV_W + CONV_WIDTH
N_VEC_ROWS = V_B_IN + IN_GROUPS
HW_LRU_A, HW_LRU_X, HW_M_Q, HW_M_K, HW_M_V, HW_M_O = range(6)
HEAD_W_SCALE = (1.0, 1.0, 1.0, 1.0, 1.0, 0.5)
ROW_INPUTS = ((V_NORM_G, 1.0), (V_LRU_CONV_B, 0.5), (V_LRU_B_A, 0.5), (V_LRU_B_X, 0.5), (V_LRU_LAMBDA, 1.0),
              (V_M_CONV_B, 0.5), (V_M_B_O, 0.5), (V_M_NORM_G, 1.0), (V_M_SKIP, 1.0), (V_FINAL_G, 1.0),
              (V_B_IF, 1.0))
CONV_SCALE = 0.5

BF16 = jnp.bfloat16
F32 = jnp.float32


def _dot(a, b):
    return jnp.dot(a, b, preferred_element_type=F32)


def _dot_nt(a, b):
    return lax.dot_general(a, b, (((1,), (1,)), ((), ())), preferred_element_type=F32)


def _tanh1(v):
    return jnp.tanh(v) + 1.0


def _log_sigmoid(x):
    return jnp.minimum(x, 0.0) - jnp.log1p(jnp.exp(-jnp.abs(x)))


def _rms_norm(x, g):
    ms = jnp.mean(x * x, axis=-1, keepdims=True)
    return x * lax.rsqrt(ms + EPS) * g


def _blk(j):
    return slice(j * HEAD_DIM, (j + 1) * HEAD_DIM)


def _prefix_rows(x, op, identity):
    n = x.shape[0]
    row = lax.broadcasted_iota(jnp.int32, x.shape, 0)
    k = 1
    while k < n:
        x = op(x, jnp.where(row >= k, pltpu.roll(x, k, 0), identity))
        k *= 2
    return x


def _lane_tile(x, n):
    reps = n // LANES
    return x if reps == 1 else jnp.concatenate([x] * reps, axis=-1)


def _lane_bcast(x, lane):
    return jnp.broadcast_to(x[:, lane:lane + 1], x.shape)


def _shift_rows(x3, k):
    sub = lax.broadcasted_iota(jnp.int32, (x3.shape[0] - 1,) + x3.shape[1:], 1)
    r = pltpu.roll(x3, k, 1)
    return jnp.where(sub < k, r[:-1], r[1:])


def _causal_conv(xb, tail, cw, cb):
    assert CONV_WIDTH == 4
    t, c = xb.shape
    groups = t // SUBLANES
    full = jnp.concatenate([tail, xb], axis=0).reshape(groups + 1, SUBLANES, c)
    w0, w1, w2, w3 = (cw[j:j + 1, :] for j in range(CONV_WIDTH))
    prev = _shift_rows(full, 1)
    tail3 = full[:1]
    tail_prev = pltpu.roll(tail3, 1, 1)
    older = jnp.concatenate([w1 * tail3 + w0 * tail_prev, w1 * full[1:] + w0 * prev], axis=0)
    acc = cb + w3 * full[1:] + w2 * prev + _shift_rows(older, 2)
    return acc.reshape(t, c)


def _lru_scan(a, b, h0):
    t, c = a.shape
    row = lax.broadcasted_iota(jnp.int32, (SUBLANES, c), 0)
    out = []
    for j in range(t // SUBLANES):
        aj = a[j * SUBLANES:(j + 1) * SUBLANES, :]
        bj = b[j * SUBLANES:(j + 1) * SUBLANES, :]
        for k in (1, 2, 4):
            valid = row >= k
            a_sh = pltpu.roll(aj, k, 0)
            b_sh = pltpu.roll(bj, k, 0)
            bj = bj + aj * jnp.where(valid, b_sh, 0.0)
            aj = aj * jnp.where(valid, a_sh, 1.0)
        hj = bj + aj * h0
        out.append(hj)
        h0 = jnp.broadcast_to(hj[SUBLANES - 1:SUBLANES, :], (SUBLANES, c))
    return jnp.concatenate(out, axis=0), h0


def _zip_stages(gens):
    gens = list(gens)
    while gens:
        alive = []
        for g in gens:
            try:
                next(g)
                alive.append(g)
            except StopIteration:
                pass
        gens = alive
        if gens:
            yield


def _interleave(gens, skew=1):
    pending = list(gens)
    active = []
    rnd = 0
    while pending or active:
        while pending and (len(gens) - len(pending)) * skew <= rnd:
            active.append(pending.pop(0))
        alive = []
        for g in active:
            try:
                next(g)
                alive.append(g)
            except StopIteration:
                pass
        active = alive
        rnd += 1


def _scaled(v, scale):
    return v if scale == 1.0 else v * scale


def _prep_kernel(mem_ref, g_ref, w_ref,
                 r0, r1, r2, r3, r4, r5, r6, r7, r8, r9, r10, lcw_ref, mcw_ref, bin_ref,
                 h0, h1, h2, h3, h4, h5,
                 kv_ref, vec_ref, hw_ref):
    nb = mem_ref.shape[0]
    mn = _rms_norm(mem_ref[...].reshape(nb * N_MEM, D_MODEL), g_ref[...]).astype(BF16)
    w = w_ref[...].astype(BF16)
    for j in range(N_HEADS):
        kv_ref[:, j] = _dot(mn, w[:, _blk(j)]).astype(BF16).reshape(nb, N_MEM, HEAD_DIM)

    @pl.when(pl.program_id(0) == 0)
    def _():
        for (row, scale), ref in zip(ROW_INPUTS, (r0, r1, r2, r3, r4, r5, r6, r7, r8, r9, r10)):
            vec_ref[row:row + 1, :] = _scaled(ref[...], scale)
        vec_ref[V_LRU_CONV_W:V_LRU_CONV_W + CONV_WIDTH, :] = lcw_ref[0] * CONV_SCALE
        vec_ref[V_M_CONV_W:V_M_CONV_W + CONV_WIDTH, :] = mcw_ref[0] * CONV_SCALE
        for g in range(IN_GROUPS):
            vec_ref[V_B_IN + g:V_B_IN + g + 1, :] = _scaled(bin_ref[g:g + 1, :], IN_SCALE[g])
        for i, ref in enumerate((h0, h1, h2, h3, h4, h5)):
            hw_ref[i] = _scaled(ref[0], HEAD_W_SCALE[i]).astype(BF16)


def _load_weights(win_hbm, wbr_hbm, wout_hbm, win_s, wbr_s, wout_s, stage_s, sem):
    per = STAGE_COLS // MXU_COLS
    chunks = []
    for c in range(win_hbm.shape[-1] // STAGE_COLS):
        chunks.append((win_hbm.at[0, :, pl.ds(c * STAGE_COLS, STAGE_COLS)], win_s, c * per,
                       IN_SCALE[c * STAGE_COLS // D_MODEL]))
    for n in range(N_BRANCH):
        for c in range(D_MODEL // STAGE_COLS):
            chunks.append((wbr_hbm.at[0, n, :, pl.ds(c * STAGE_COLS, STAGE_COLS)], wbr_s.at[n], c * per,
                           BRANCH_SCALE))
    for c in range(D_MODEL // STAGE_COLS):
        chunks.append((wout_hbm.at[0, :, pl.ds(c * STAGE_COLS, STAGE_COLS)], wout_s, c * per, 1.0))

    def copy(i):
        return pltpu.make_async_copy(chunks[i][0], stage_s.at[i % STAGE_SLOTS], sem.at[i % STAGE_SLOTS])

    ahead = STAGE_SLOTS - 1
    for i in range(min(ahead, len(chunks))):
        copy(i).start()
    for i, (_, dest, blk0, scale) in enumerate(chunks):
        if i + ahead < len(chunks):
            copy(i + ahead).start()
        copy(i).wait()
        for j in range(per):
            w = stage_s[i % STAGE_SLOTS, :, j * MXU_COLS:(j + 1) * MXU_COLS]
            dest[blk0 + j] = (w if scale == 1.0 else w * scale).astype(BF16)


def _layer_kernel(x_ref, vec_ref, hw_ref, wif_ref, kv_ref, win_hbm, wbr_hbm, wout_hbm,
                  o_ref,
                  win_s, wbr_s, wout_s, stage_s, dma_sem,
                  hb_s, ltail_s, mtail_s, hcar_s, q_s, k_s, v_s, og_s, xc_s, g_s, st_s, m_s, y_s, acc_s, mg_s,
                  *, nb, tr):
    L = tr
    R = nb * tr
    rows = [slice(b * tr, (b + 1) * tr) for b in range(nb)]

    @pl.when(pl.program_id(0) == 0)
    def _():
        ltail_s[...] = jnp.zeros(ltail_s.shape, F32)
        mtail_s[...] = jnp.zeros(mtail_s.shape, F32)
        hcar_s[...] = jnp.zeros(hcar_s.shape, F32)
        st_s[...] = jnp.zeros(st_s.shape, F32)
        m_s[...] = jnp.zeros(m_s.shape, F32)
        _load_weights(win_hbm, wbr_hbm, wout_hbm, win_s, wbr_s, wout_s, stage_s, dma_sem)

    def vec(i, sl=slice(None)):
        return vec_ref[i:i + 1, sl]

    hb_s[...] = _rms_norm(x_ref[...].reshape(R, D_MODEL), vec(V_NORM_G)).astype(BF16)

    def proj(j):
        return _dot(hb_s[...], win_s[j]) + vec(V_B_IN + j // N_HEADS, _blk(j % N_HEADS))

    def conv(x, tail_ref, cw_row, cb_row, sl):
        cw = vec_ref[cw_row:cw_row + CONV_WIDTH, sl]
        parts = []
        for b in range(nb):
            xb = x[rows[b], :]
            parts.append(_causal_conv(xb, tail_ref[b, :, sl], cw, vec(cb_row, sl)))
            tail_ref[b, :, sl] = xb[tr - SUBLANES:, :]
        return jnp.concatenate(parts, axis=0)


    def lru_block(g):
        sl = _blk(g)
        px = proj(LRU_X + g)
        yield
        u = conv(px, ltail_s, V_LRU_CONV_W, V_LRU_CONV_B, sl)
        ub = u.astype(BF16)
        yield
        r_pre = _dot(ub, hw_ref[HW_LRU_A, g]) + vec(V_LRU_B_A, sl)
        i_pre = _dot(ub, hw_ref[HW_LRU_X, g]) + vec(V_LRU_B_X, sl)
        yield
        zh = proj(LRU_Z + g)
        ls_c = (0.5 * LRU_C) * _log_sigmoid(vec(V_LRU_LAMBDA, sl))
        yield
        for b in range(nb):
            h0 = hcar_s[b, :, sl]
            for c in range(tr // ROW_CHUNK):
                rs = slice(b * tr + c * ROW_CHUNK, b * tr + (c + 1) * ROW_CHUNK)
                log_a = ls_c * _tanh1(r_pre[rs, :])
                a = jnp.exp(log_a)
                om = -jnp.tanh(log_a) * (1.0 + a * a)
                mult = jnp.where(om > 0.0, om * lax.rsqrt(om), 0.0)
                bb = mult * _tanh1(i_pre[rs, :]) * u[rs, :]
                h, h0 = _lru_scan(a, bb, h0)
                y_s[0, rs, sl] = (h * (zh[rs, :] * _tanh1(zh[rs, :]))).astype(BF16)
                yield
            hcar_s[b, :, sl] = h0

    def xattn_head(h):
        sl = _blk(h)
        q = proj(XA_Q + h).astype(BF16)
        yield
        scores = [_dot_nt(q[rows[b], :], kv_ref[b, h]) * (HEAD_DIM ** -0.5) for b in range(nb)]
        yield
        probs = []
        for s in scores:
            e = jnp.exp(s - jnp.max(s, axis=-1, keepdims=True))
            probs.append((e * (1.0 / jnp.sum(e, axis=-1, keepdims=True))).astype(BF16))
            yield
        o = jnp.concatenate([_dot(probs[b], kv_ref[b, N_HEADS + h]) for b in range(nb)], axis=0)
        yield
        zh = proj(XA_Z + h)
        yield
        y_s[2, :, sl] = (o * (zh * _tanh1(zh))).astype(BF16)

    def mlstm_front(h):
        sl = _blk(h)
        xm = proj(M_X + h)
        yield
        ch = conv(xm, mtail_s, V_M_CONV_W, V_M_CONV_B, sl)
        xc = ch * _tanh1(ch)
        xc_s[:, sl] = xc
        xcb = xc.astype(BF16)
        xmb = xm.astype(BF16)
        yield
        q_s[:, sl] = _dot(xcb, hw_ref[HW_M_Q, h]).astype(BF16)
        k_s[:, sl] = _dot(xcb, hw_ref[HW_M_K, h]).astype(BF16)
        yield
        v_s[:, sl] = _dot(xmb, hw_ref[HW_M_V, h]).astype(BF16)
        og_s[:, sl] = _tanh1(_dot(xmb, hw_ref[HW_M_O, h]) + vec(V_M_B_O, sl))
        fronts_done.append(h)

    fronts_done = []
    gate_terms = []

    def mlstm_gates():
        while len(fronts_done) < N_HEADS:
            yield
        g_s[...] = (_dot(q_s[...], wif_ref[0:D_MODEL, :])
                    + _dot(k_s[...], wif_ref[D_MODEL:2 * D_MODEL, :])
                    + _dot(v_s[...], wif_ref[2 * D_MODEL:3 * D_MODEL, :])
                    + vec(V_B_IF, slice(0, 2 * LANES)))
        yield
        for b in range(nb):
            bcum = _prefix_rows(_log_sigmoid(g_s[rows[b], LANES:2 * LANES]), jnp.add, 0.0)
            gq = g_s[rows[b], 0:LANES] - bcum
            cmax = _prefix_rows(gq, jnp.maximum, -jnp.inf)
            gate_terms.append((bcum, gq, cmax, gq.T))
            yield

    causal = (lax.broadcasted_iota(jnp.int32, (L, L), 0) >= lax.broadcasted_iota(jnp.int32, (L, L), 1))

    def mlstm_chunk(b, h, terms, out):
        sl = _blk(h)
        bcum, gq, cmax, gq_t = terms
        b_rep = _lane_bcast(bcum, h)
        gq_rep = _lane_bcast(gq, h)
        gq_row = gq_t[h:h + 1, :]
        m_prev = m_s[b, h, 0:1, :]
        mm = jnp.maximum(m_prev, _lane_bcast(cmax, h))
        mm_last = mm[L - 1:L, :]
        p = jnp.exp(jnp.where(causal, gq_row - _lane_tile(mm, L), -jnp.inf))
        sc = jnp.exp(m_prev - mm)
        qh = q_s[rows[b], sl]
        kh = (k_s[rows[b], sl] * (HEAD_DIM ** -0.5)).astype(BF16)
        vh = v_s[rows[b], sl]
        kt = kh.T
        st = st_s[b, h]
        stb = st.astype(BF16)
        yield
        qk = _dot(qh, jnp.concatenate([kt, stb[:, HEAD_DIM:]], axis=1))
        qc = _dot(qh, stb[:, :HEAD_DIM])
        yield
        s = qk[:, :L] * p
        den = jnp.sum(s, axis=-1, keepdims=True) + sc * qk[:, L:]
        inv = 0.5 / jnp.maximum(jnp.abs(den), jnp.exp(-(b_rep + mm)))
        wk = jnp.exp(gq_rep - mm_last)
        wkv = jnp.concatenate([_lane_tile(wk, HEAD_DIM) * vh.astype(F32), wk], axis=1).astype(BF16)
        yield
        num = _dot(s.astype(BF16), vh) + _lane_tile(sc, HEAD_DIM) * qc
        upd = _dot(kt, wkv)
        yield
        decay = sc[L - 1:L, :]
        st_s[b, h] = _lane_tile(decay, HEAD_DIM + LANES) * st + upd
        m_s[b, h] = jnp.broadcast_to(b_rep[L - 1:L, :] + mm_last, (SUBLANES, LANES))
        out[b] = num * _lane_tile(inv, HEAD_DIM)

    def mlstm_head(h, terms):
        sl = _blk(h)
        hparts = [None] * nb
        yield from _zip_stages([mlstm_chunk(b, h, terms[b], hparts) for b in range(nb)])
        hh = og_s[:, sl] * jnp.concatenate(hparts, axis=0)
        mu = jnp.mean(hh, axis=-1, keepdims=True)
        dl = hh - mu
        var = jnp.mean(dl * dl, axis=-1, keepdims=True)
        hn = dl * lax.rsqrt(var + EPS) * vec(V_M_NORM_G, sl)
        yield
        zh = proj(M_Z + h)
        yield
        y = (hn + vec(V_M_SKIP, sl) * xc_s[:, sl]) * (zh * _tanh1(zh))
        y_s[1, :, sl] = y.astype(BF16)

    def branch_out(n, j):
        gate2 = _tanh1(proj(GATE + N_HEADS * n + j))
        yield
        return gate2 * _dot(y_s[n], wbr_s[n, j])

    def branch_pair(j):
        first = yield from branch_out(0, j)
        yield
        second = yield from branch_out(2, j)
        acc_s[:, _blk(j)] = first + second

    def branch_last(j):
        last = yield from branch_out(1, j)
        mg_s[:, _blk(j)] = (acc_s[:, _blk(j)] + last).astype(BF16)

    heads = range(N_HEADS)
    _interleave([f(i) for i in heads for f in (lru_block, mlstm_front, xattn_head)] + [mlstm_gates()])
    _interleave([f(i) for i in heads for f in (lambda i: mlstm_head(i, gate_terms), branch_pair)], skew=0)
    _interleave([branch_last(j) for j in heads])
    ssq = None
    for j in range(N_HEADS):
        sl = _blk(j)
        xo = x_ref[:, :, sl].reshape(R, HEAD_DIM) + _dot(mg_s[...], wout_s[j])
        o_ref[:, :, sl] = xo.reshape(nb, tr, HEAD_DIM)
        part = jnp.sum(xo * xo, axis=-1, keepdims=True)
        ssq = part if ssq is None else ssq + part
    scale = lax.rsqrt(ssq * (1.0 / D_MODEL) + EPS)
    o_ref[...] = (o_ref[...].reshape(R, D_MODEL) * scale * vec(V_FINAL_G)).reshape(nb, tr, D_MODEL)


def _resident(a):
    nd = a.ndim
    return pl.BlockSpec(a.shape, lambda i: (0,) * nd, pipeline_mode=pl.Buffered(1))


def kernel(x, mem, norm_g, w_in, b_in, lru_conv_w, lru_conv_b, lru_w_a, lru_b_a, lru_w_x, lru_b_x,
           lru_lambda, m_conv_w, m_conv_b, m_w_q, m_w_k, m_w_v, m_w_o, m_b_o, m_w_if, m_b_if,
           m_norm_g, m_skip, mem_norm_g, w_mem_kv, w_branch, w_out, final_norm_g):
    B, S, D = x.shape
    assert D == D_MODEL and S % TIME_TILE == 0 and norm_g.shape[0] == 1
    tr = TIME_TILE
    H = N_HEADS
    pad_lanes = lambda a: jnp.pad(a, ((0, 0), (0, LANES - H)))
    w_if = jnp.concatenate([pad_lanes(m_w_if[0][:, :H]), pad_lanes(m_w_if[0][:, H:])], axis=1).astype(BF16)
    b_if2 = m_b_if[0].reshape(1, 2 * H).astype(F32)
    b_if = jnp.pad(jnp.concatenate([pad_lanes(b_if2[:, :H]), pad_lanes(b_if2[:, H:])], axis=1),
                   ((0, 0), (0, D - 2 * LANES)))

    rows_in = [norm_g, lru_conv_b, lru_b_a, lru_b_x, lru_lambda, m_conv_b, m_b_o, m_norm_g, m_skip,
               final_norm_g.reshape(1, D), b_if]
    prep_in = rows_in + [lru_conv_w, m_conv_w, b_in.reshape(IN_GROUPS, D),
                         lru_w_a, lru_w_x, m_w_q, m_w_k, m_w_v, m_w_o]
    whole = lambda a: pl.BlockSpec(a.shape, lambda c: (0,) * a.ndim)
    kv, vecs, head_w = pl.pallas_call(
        _prep_kernel,
        grid=(2,),
        in_specs=[whole(mem), whole(mem_norm_g), pl.BlockSpec((None, D, D), lambda c: (0, 0, c))]
        + [whole(a) for a in prep_in],
        out_specs=[pl.BlockSpec((B, H, N_MEM, HEAD_DIM), lambda c: (0, c, 0, 0)),
                   pl.BlockSpec((N_VEC_ROWS, D), lambda c: (0, 0)),
                   pl.BlockSpec((len(HEAD_W_SCALE), H, HEAD_DIM, HEAD_DIM), lambda c: (0, 0, 0, 0))],
        out_shape=[jax.ShapeDtypeStruct((B, 2 * H, N_MEM, HEAD_DIM), BF16),
                   jax.ShapeDtypeStruct((N_VEC_ROWS, D), F32),
                   jax.ShapeDtypeStruct((len(HEAD_W_SCALE), H, HEAD_DIM, HEAD_DIM), BF16)],
        compiler_params=pltpu.CompilerParams(dimension_semantics=("arbitrary",),
                                             vmem_limit_bytes=VMEM_LIMIT_BYTES),
        name="prep",
    )(mem, mem_norm_g, w_mem_kv, *prep_in)
    weights = [vecs, head_w, w_if, kv]
    hbm_weights = [w_in, w_branch, w_out]

    R = B * tr
    tok = pl.BlockSpec((B, tr, D), lambda i: (0, i, 0))
    nblk = D // MXU_COLS
    scratch = [pltpu.VMEM((IN_GROUPS * nblk, D, MXU_COLS), BF16),
               pltpu.VMEM((N_BRANCH, nblk, D, MXU_COLS), BF16),
               pltpu.VMEM((nblk, D, MXU_COLS), BF16),
               pltpu.VMEM((STAGE_SLOTS, D, STAGE_COLS), F32),
               pltpu.SemaphoreType.DMA((STAGE_SLOTS,)),
               pltpu.VMEM((R, D), BF16),
               pltpu.VMEM((B, SUBLANES, D), F32),
               pltpu.VMEM((B, SUBLANES, D), F32),
               pltpu.VMEM((B, SUBLANES, D), F32),
               pltpu.VMEM((R, D), BF16),
               pltpu.VMEM((R, D), BF16),
               pltpu.VMEM((R, D), BF16),
               pltpu.VMEM((R, D), F32),
               pltpu.VMEM((R, D), F32),
               pltpu.VMEM((R, 2 * LANES), F32),
               pltpu.VMEM((B, H, HEAD_DIM, HEAD_DIM + LANES), F32),
               pltpu.VMEM((B, H, SUBLANES, LANES), F32),
               pltpu.VMEM((N_BRANCH, R, D), BF16),
               pltpu.VMEM((R, D), F32),
               pltpu.VMEM((R, D), BF16)]
    return pl.pallas_call(
        functools.partial(_layer_kernel, nb=B, tr=tr),
        grid=(S // tr,),
        in_specs=([tok] + [_resident(w) for w in weights]
                  + [pl.BlockSpec(memory_space=pl.ANY)] * len(hbm_weights)),
        out_specs=tok,
        out_shape=jax.ShapeDtypeStruct((B, S, D), F32),
        scratch_shapes=scratch,
        compiler_params=pltpu.CompilerParams(
            dimension_semantics=("arbitrary",),
            vmem_limit_bytes=VMEM_LIMIT_BYTES),
        name="hybrid_layer",
    )(x, *weights, *hbm_weights)
```
